```python
import math
import jax, jax.numpy as jnp
from jax import lax
import numpy as np

D_MODEL = 1024
BATCH = 8
SEQ = 4096
DEPTH = 1

N_HEADS = 8
QK_NOPE_DIM = 64
QK_ROPE_DIM = 32
QK_HEAD_DIM = QK_NOPE_DIM + QK_ROPE_DIM
V_HEAD_DIM = 64
Q_LORA_RANK = 384
KV_LORA_RANK = 256
ROPE_THETA = 10000.0
Q_BLOCK = 128
SSM_WIDTH = 512
SSM_GROUP = 16
SSM_GROUPS = SSM_WIDTH // SSM_GROUP
SSM_STATE = 64
DT_MIN = 0.001
DT_MAX = 0.1
N_BRANCHES = 2
D_FF = 2816
CONV_WIDTH = 3
EPS = 1e-6

OFF_CQ = 0
OFF_CKV = OFF_CQ + Q_LORA_RANK
OFF_KR = OFF_CKV + KV_LORA_RANK
OFF_U = OFF_KR + QK_ROPE_DIM
OFF_GATE = OFF_U + SSM_WIDTH
D_IN = OFF_GATE + N_BRANCHES * D_MODEL

kernel_name = "hybrid_mla_s5_gated_convffn"


def _rmsnorm(x, g):
    xf = x.astype(jnp.float32)
    y = xf * lax.rsqrt(jnp.mean(xf * xf, axis=-1, keepdims=True) + EPS)
    return (y * g.astype(jnp.float32)).astype(x.dtype)


def _rope(x, cos, sin):
    half = x.shape[-1] // 2
    x1, x2 = x[..., :half], x[..., half:]
    return jnp.concatenate([x1 * cos - x2 * sin, x2 * cos + x1 * sin], axis=-1)


def _rope_tables(positions, dtype):
    inv_freq = ROPE_THETA ** (-jnp.arange(0, QK_ROPE_DIM, 2, dtype=jnp.float32) / QK_ROPE_DIM)
    ang = positions.astype(jnp.float32)[..., None] * inv_freq
    return jnp.cos(ang).astype(dtype), jnp.sin(ang).astype(dtype)


def _mla(c_q, c_kv, k_r, cos, sin, q_norm, w_uq, kv_norm, w_uk, w_uv):
    B, L, _ = c_q.shape
    q = (_rmsnorm(c_q, q_norm) @ w_uq).reshape(B, L, N_HEADS, QK_HEAD_DIM)
    q_nope = q[..., :QK_NOPE_DIM]
    q_rope = _rope(q[..., QK_NOPE_DIM:], cos[:, :, None, :], sin[:, :, None, :])
    ckv = _rmsnorm(c_kv, kv_norm)
    k_nope = (ckv @ w_uk).reshape(B, L, N_HEADS, QK_NOPE_DIM)
    v = (ckv @ w_uv).reshape(B, L, N_HEADS, V_HEAD_DIM)
    k_rope = _rope(k_r, cos, sin)
    scale = 1.0 / math.sqrt(QK_HEAD_DIM)
    nblk = L // Q_BLOCK
    qn_blk = q_nope.reshape(B, nblk, Q_BLOCK, N_HEADS, QK_NOPE_DIM).transpose(1, 0, 2, 3, 4)
    qr_blk = q_rope.reshape(B, nblk, Q_BLOCK, N_HEADS, QK_ROPE_DIM).transpose(1, 0, 2, 3, 4)
    k_idx = jnp.arange(L)

    def block(args):
        qn, qr, i = args
        s = jnp.einsum('bqhd,bkhd->bhqk', qn, k_nope) + jnp.einsum('bqhd,bkd->bhqk', qr, k_rope)
        s = s.astype(jnp.float32) * scale
        q_idx = i * Q_BLOCK + jnp.arange(Q_BLOCK)
        s = jnp.where(k_idx[None, :] <= q_idx[:, None], s, -jnp.inf)
        p = jax.nn.softmax(s, axis=-1).astype(v.dtype)
        return jnp.einsum('bhqk,bkhd->bqhd', p, v)

    out = lax.map(block, (qn_blk, qr_blk, jnp.arange(nblk)))
    return out.transpose(1, 0, 2, 3, 4).reshape(B, L, N_HEADS * V_HEAD_DIM)


def _ssm_combine(e1, e2):
    a1r, a1i, b1r, b1i = e1
    a2r, a2i, b2r, b2i = e2
    return (a2r * a1r - a2i * a1i,
            a2r * a1i + a2i * a1r,
            a2r * b1r - a2i * b1i + b2r,
            a2r * b1i + a2i * b1r + b2i)


def _s5(u, lam_re, lam_im, log_dt, b_re, b_im, c_re, c_im, d_skip, w_glu, b_glu):
    B, L, _ = u.shape
    uf = u.astype(jnp.float32).reshape(B, L, SSM_GROUPS, SSM_GROUP)
    dt = jnp.exp(log_dt.astype(jnp.float32))[:, None]
    lr = lam_re.astype(jnp.float32)
    li = lam_im.astype(jnp.float32)
    mag = jnp.exp(lr * dt)
    ang = li * dt
    a_re, a_im = mag * jnp.cos(ang), mag * jnp.sin(ang)
    den = lr * lr + li * li
    n_re, n_im = a_re - 1.0, a_im
    z_re = (n_re * lr + n_im * li) / den
    z_im = (n_im * lr - n_re * li) / den
    br, bi = b_re.astype(jnp.float32), b_im.astype(jnp.float32)
    bb_re = z_re[..., None] * br - z_im[..., None] * bi
    bb_im = z_re[..., None] * bi + z_im[..., None] * br
    bu_re = jnp.einsum('blgh,gph->blgp', uf, bb_re)
    bu_im = jnp.einsum('blgh,gph->blgp', uf, bb_im)
    a_seq_re = jnp.broadcast_to(a_re, (L, SSM_GROUPS, SSM_STATE))
    a_seq_im = jnp.broadcast_to(a_im, (L, SSM_GROUPS, SSM_STATE))

    def scan_one(bur, bui):
        _, _, sr, si = lax.associative_scan(_ssm_combine, (a_seq_re, a_seq_im, bur, bui), axis=0)
        return sr, si

    s_re, s_im = jax.vmap(scan_one)(bu_re, bu_im)
    y = (jnp.einsum('blgp,ghp->blgh', s_re, c_re.astype(jnp.float32))
         - jnp.einsum('blgp,ghp->blgh', s_im, c_im.astype(jnp.float32))
         + d_skip.astype(jnp.float32) * uf)
    y = jax.nn.gelu(y.reshape(B, L, SSM_WIDTH), approximate=True)
    y = y * jax.nn.sigmoid(y @ w_glu.astype(jnp.float32) + b_glu.astype(jnp.float32))
    return y.astype(u.dtype)


def _causal_dwconv(h, w, b):
    L = h.shape[1]
    hp = jnp.pad(h, ((0, 0), (CONV_WIDTH - 1, 0), (0, 0)))
    out = b
    for k in range(CONV_WIDTH):
        out = out + w[k] * hp[:, k:k + L, :]
    return out


def _conv_ffn(x, w_up, conv_w, conv_b, w_down):
    h = _causal_dwconv(x @ w_up, conv_w, conv_b)
    gate, val = h[..., :D_FF], h[..., D_FF:]
    return (jax.nn.gelu(gate, approximate=True) * val) @ w_down


def setup_inputs(seed: int = 0) -> dict:
    key = jax.random.key(seed)
    ks = iter(jax.random.split(key, 40))

    def nrm(shape, scale):
        return jax.random.normal(next(ks), shape, jnp.float32) * scale

    def gain(n):
        return 1.0 + nrm((DEPTH, n), 0.02)

    x = jax.random.normal(next(ks), (BATCH, SEQ, D_MODEL), jnp.float32)
    offs = jax.random.randint(next(ks), (BATCH, 1), 0, 1024, dtype=jnp.int32)
    positions = offs + jnp.arange(SEQ, dtype=jnp.int32)[None, :]
    lam_im0 = math.pi * jnp.arange(SSM_STATE, dtype=jnp.float32)
    return {
        "x": x,
        "positions": positions,
        "mix_norm_pre": gain(D_MODEL),
        "w_in": nrm((DEPTH, D_MODEL, D_IN), D_MODEL ** -0.5),
        "q_norm": gain(Q_LORA_RANK),
        "w_uq": nrm((DEPTH, Q_LORA_RANK, N_HEADS * QK_HEAD_DIM), Q_LORA_RANK ** -0.5),
        "kv_norm": gain(KV_LORA_RANK),
        "w_uk": nrm((DEPTH, KV_LORA_RANK, N_HEADS * QK_NOPE_DIM), KV_LORA_RANK ** -0.5),
        "w_uv": nrm((DEPTH, KV_LORA_RANK, N_HEADS * V_HEAD_DIM), KV_LORA_RANK ** -0.5),
        "ssm_lambda_re": -0.5 + nrm((DEPTH, SSM_GROUPS, SSM_STATE), 0.01),
        "ssm_lambda_im": lam_im0 + nrm((DEPTH, SSM_GROUPS, SSM_STATE), 0.01),
        "ssm_log_dt": jax.random.uniform(next(ks), (DEPTH, SSM_GROUPS), jnp.float32,
                                          math.log(DT_MIN), math.log(DT_MAX)),
        "ssm_b_re": nrm((DEPTH, SSM_GROUPS, SSM_STATE, SSM_GROUP), (2.0 * SSM_GROUP) ** -0.5),
        "ssm_b_im": nrm((DEPTH, SSM_GROUPS, SSM_STATE, SSM_GROUP), (2.0 * SSM_GROUP) ** -0.5),
        "ssm_c_re": nrm((DEPTH, SSM_GROUPS, SSM_GROUP, SSM_STATE), (2.0 * SSM_STATE) ** -0.5),
        "ssm_c_im": nrm((DEPTH, SSM_GROUPS, SSM_GROUP, SSM_STATE), (2.0 * SSM_STATE) ** -0.5),
        "ssm_d": nrm((DEPTH, SSM_GROUPS, SSM_GROUP), 1.0),
        "w_glu": nrm((DEPTH, SSM_WIDTH, SSM_WIDTH), SSM_WIDTH ** -0.5),
        "b_glu": nrm((DEPTH, SSM_WIDTH), 0.01),
        "w_branch_attn": nrm((DEPTH, N_HEADS * V_HEAD_DIM, D_MODEL), (N_HEADS * V_HEAD_DIM) ** -0.5),
        "w_branch_ssm": nrm((DEPTH, SSM_WIDTH, D_MODEL), SSM_WIDTH ** -0.5),
        "b_gate": nrm((DEPTH, N_BRANCHES * D_MODEL), 0.01),
        "w_out": nrm((DEPTH, D_MODEL, D_MODEL), D_MODEL ** -0.5),
        "mix_norm_post": gain(D_MODEL),
        "ffn_norm_pre": gain(D_MODEL),
        "w_up": nrm((DEPTH, D_MODEL, 2 * D_FF), D_MODEL ** -0.5),
        "conv_w": nrm((DEPTH, CONV_WIDTH, 2 * D_FF), CONV_WIDTH ** -0.5),
        "conv_b": nrm((DEPTH, 2 * D_FF), 0.01),
        "w_down": nrm((DEPTH, D_FF, D_MODEL), D_FF ** -0.5),
        "ffn_norm_post": gain(D_MODEL),
    }


def reference(x, positions, mix_norm_pre, w_in, q_norm, w_uq, kv_norm, w_uk, w_uv,
              ssm_lambda_re, ssm_lambda_im, ssm_log_dt, ssm_b_re, ssm_b_im, ssm_c_re, ssm_c_im,
              ssm_d, w_glu, b_glu, w_branch_attn, w_branch_ssm, b_gate, w_out, mix_norm_post,
              ffn_norm_pre, w_up, conv_w, conv_b, w_down, ffn_norm_post):
    B, L, _ = x.shape
    cos, sin = _rope_tables(positions, x.dtype)
    for layer in range(DEPTH):
        hn = _rmsnorm(x, mix_norm_pre[layer])
        proj = hn @ w_in[layer]
        c_q = proj[..., OFF_CQ:OFF_CKV]
        c_kv = proj[..., OFF_CKV:OFF_KR]
        k_r = proj[..., OFF_KR:OFF_U]
        u = proj[..., OFF_U:OFF_GATE]
        gate_logits = proj[..., OFF_GATE:] + b_gate[layer]
        attn = _mla(c_q, c_kv, k_r, cos, sin, q_norm[layer], w_uq[layer],
                    kv_norm[layer], w_uk[layer], w_uv[layer])
        ssm = _s5(u, ssm_lambda_re[layer], ssm_lambda_im[layer], ssm_log_dt[layer],
                  ssm_b_re[layer], ssm_b_im[layer], ssm_c_re[layer], ssm_c_im[layer],
                  ssm_d[layer], w_glu[layer], b_glu[layer])
        gates = jax.nn.sigmoid(gate_logits.astype(jnp.float32)).astype(x.dtype)
        gates = gates.reshape(B, L, N_BRANCHES, D_MODEL)
        merged = (gates[:, :, 0] * (attn @ w_branch_attn[layer])
                  + gates[:, :, 1] * (ssm @ w_branch_ssm[layer]))
        x = x + _rmsnorm(merged @ w_out[layer], mix_norm_post[layer])
        hn = _rmsnorm(x, ffn_norm_pre[layer])
        ff = _conv_ffn(hn, w_up[layer], conv_w[layer], conv_b[layer], w_down[layer])
        x = x + _rmsnorm(ff, ffn_norm_post[layer])
    return x
```

```python
import functools
import math

import jax
import jax.numpy as jnp
from jax import lax
from jax.experimental import pallas as pl
from jax.experimental.pallas import tpu as pltpu

D_MODEL = 1024
N_HEADS = 8
QK_NOPE_DIM = 64
QK_ROPE_DIM = 32
QK_HEAD_DIM = QK_NOPE_DIM + QK_ROPE_DIM
V_HEAD_DIM = 64
Q_LORA_RANK = 384
KV_LORA_RANK = 256
ROPE_THETA = 10000.0
SSM_WIDTH = 512
SSM_GROUP = 16
SSM_GROUPS = SSM_WIDTH // SSM_GROUP
SSM_STATE = 64
D_FF = 2816
CONV_WIDTH = 3
EPS = 1e-6

LANES = 128
HEAD_PAD = LANES
ROPE_LANE0 = QK_NOPE_DIM
ROPE_HALF = QK_ROPE_DIM // 2
N_PAIRS = N_HEADS // 2
SSM_HALVES = 2
SSM_HALF_IN = SSM_WIDTH // SSM_HALVES
SSM_HALF_STATE = SSM_GROUPS * SSM_STATE // SSM_HALVES

P_CQ = 0
P_CKV = P_CQ + Q_LORA_RANK
P_KR = P_CKV + KV_LORA_RANK
P_U = P_KR + LANES
P_GATE = P_U + SSM_WIDTH
P_END = P_GATE + 2 * D_MODEL

Q_SCALE = (1.0 / math.sqrt(QK_HEAD_DIM)) * math.log2(math.e)
NEG_BIG = -1e30

ROW_TILE = 512
ATTN_TILE = 512
SSM_CHUNK = 64
FFN_COL_CHUNK = 256
HALO = 16
VMEM_LIMIT = 56 * 1024 * 1024

BF16 = jnp.bfloat16
F32 = jnp.float32


def _rms(x, g):
    return x * lax.rsqrt(jnp.mean(x * x, axis=-1, keepdims=True) + EPS) * g


def _dot(a, b):
    return jnp.dot(a, b, preferred_element_type=F32)


def _s5_prep_kernel(lr_ref, li_ref, logdt_ref, br_ref, bi_ref, are_ref, aim_ref, bbr_ref, bbi_ref):
    lr = lr_ref[...]
    li = li_ref[...]
    dt = jnp.exp(logdt_ref[...])
    mag = jnp.exp(lr * dt)
    ang = li * dt
    a_re = mag * jnp.cos(ang)
    a_im = mag * jnp.sin(ang)
    den = lr * lr + li * li
    n_re = a_re - 1.0
    n_im = a_im
    z_re = (n_re * lr + n_im * li) / den
    z_im = (n_im * lr - n_re * li) / den
    are_ref[...] = a_re
    aim_ref[...] = a_im
    br = br_ref[...]
    bi = bi_ref[...]
    zr = z_re[:, None, :]
    zi = z_im[:, None, :]
    bbr_ref[...] = zr * br - zi * bi
    bbi_ref[...] = zr * bi + zi * br


def _s5_prep(lam_re, lam_im, log_dt, b_re_t, b_im_t):
    g, p = lam_re.shape
    h = b_re_t.shape[1]
    return pl.pallas_call(
        _s5_prep_kernel,
        out_shape=(jax.ShapeDtypeStruct((g, p), F32), jax.ShapeDtypeStruct((g, p), F32),
                   jax.ShapeDtypeStruct((g, h, p), F32), jax.ShapeDtypeStruct((g, h, p), F32)),
        name="s5_prep",
    )(lam_re, lam_im, log_dt, b_re_t, b_im_t)


def _in_proj_kernel(x_ref, pos_ref, invf_ref, gpre_ref, win_ref, qn_ref, wuq_ref, kvn_ref, wuk_ref,
                    wuv_ref, bg_ref, q_ref, k_ref, v_ref, u_ref, gate_ref):
    x = x_ref[...]
    hn = _rms(x, gpre_ref[...]).astype(BF16)

    ang = pos_ref[...] * invf_ref[...]
    cos = jnp.cos(ang)
    sin = jnp.sin(ang)
    lane = lax.broadcasted_iota(jnp.int32, (1, LANES), 1)
    sin_hi = jnp.where(lane >= ROPE_LANE0 + ROPE_HALF, sin, 0.0)
    sin_lo = jnp.where(lane < ROPE_LANE0 + ROPE_HALF, -sin, 0.0)

    def rope(t):
        return (t * cos + pltpu.roll(t, ROPE_HALF, 1) * sin_hi
                + pltpu.roll(t, LANES - ROPE_HALF, 1) * sin_lo)

    cq = _dot(hn, win_ref[:, P_CQ:P_CKV])
    cqn = _rms(cq, qn_ref[...]).astype(BF16)
    q = _dot(cqn, wuq_ref[...])
    for h in range(N_HEADS):
        sl = slice(h * HEAD_PAD, (h + 1) * HEAD_PAD)
        q_ref[:, sl] = (rope(q[:, sl]) * Q_SCALE).astype(BF16)

    ckv = _dot(hn, win_ref[:, P_CKV:P_KR])
    ckvn = _rms(ckv, kvn_ref[...]).astype(BF16)
    v_ref[...] = _dot(ckvn, wuv_ref[...]).astype(BF16)
    kn = _dot(ckvn, wuk_ref[...])
    kr = rope(_dot(hn, win_ref[:, P_KR:P_U]))
    for h in range(N_HEADS):
        sl = slice(h * HEAD_PAD, (h + 1) * HEAD_PAD)
        k_ref[:, sl] = (kn[:, sl] + kr).astype(BF16)

    u_ref[...] = _dot(hn, win_ref[:, P_U:P_GATE]).astype(BF16)
    logits = _dot(hn, win_ref[:, P_GATE:P_END]) + bg_ref[...]
    gate_ref[...] = jax.nn.sigmoid(logits).astype(BF16)


def _in_proj(x, pos_f, invf, g_pre, w_in_p, q_norm, w_uq_p, kv_norm, w_uk_p, w_uv, b_gate):
    b, l, d = x.shape
    tl = ROW_TILE
    const = lambda *_: (0, 0)
    return pl.pallas_call(
        _in_proj_kernel,
        grid=(b, l // tl),
        in_specs=[
            pl.BlockSpec((None, tl, d), lambda bi, i: (bi, i, 0)),
            pl.BlockSpec((None, tl, 1), lambda bi, i: (bi, i, 0)),
            pl.BlockSpec((1, LANES), const),
            pl.BlockSpec((1, d), const),
            pl.BlockSpec((d, P_END), const),
            pl.BlockSpec((1, Q_LORA_RANK), const),
            pl.BlockSpec((Q_LORA_RANK, N_HEADS * HEAD_PAD), const),
            pl.BlockSpec((1, KV_LORA_RANK), const),
            pl.BlockSpec((KV_LORA_RANK, N_HEADS * HEAD_PAD), const),
            pl.BlockSpec((KV_LORA_RANK, N_HEADS * V_HEAD_DIM), const),
            pl.BlockSpec((1, 2 * D_MODEL), const),
        ],
        out_specs=[
            pl.BlockSpec((None, tl, N_HEADS * HEAD_PAD), lambda bi, i: (bi, i, 0)),
            pl.BlockSpec((None, tl, N_HEADS * HEAD_PAD), lambda bi, i: (bi, i, 0)),
            pl.BlockSpec((None, tl, N_HEADS * V_HEAD_DIM), lambda bi, i: (bi, i, 0)),
            pl.BlockSpec((tl, SSM_WIDTH), lambda bi, i: (i, bi)),
            pl.BlockSpec((None, tl, 2 * D_MODEL), lambda bi, i: (bi, i, 0)),
        ],
        out_shape=[
            jax.ShapeDtypeStruct((b, l, N_HEADS * HEAD_PAD), BF16),
            jax.ShapeDtypeStruct((b, l, N_HEADS * HEAD_PAD), BF16),
            jax.ShapeDtypeStruct((b, l, N_HEADS * V_HEAD_DIM), BF16),
            jax.ShapeDtypeStruct((l, b * SSM_WIDTH), BF16),
            jax.ShapeDtypeStruct((b, l, 2 * D_MODEL), BF16),
        ],
        compiler_params=pltpu.CompilerParams(
            dimension_semantics=("parallel", "parallel"), vmem_limit_bytes=VMEM_LIMIT),
        name="in_proj",
    )(x, pos_f, invf, g_pre, w_in_p, q_norm, w_uq_p, kv_norm, w_uk_p, w_uv, b_gate)


def _attn_kernel(q_ref, k_ref, v_ref, o_ref, *, tile):
    i = pl.program_id(2)
    row = lax.broadcasted_iota(jnp.int32, (tile, tile), 0)
    col = lax.broadcasted_iota(jnp.int32, (tile, tile), 1)
    causal = col <= row

    def one_head(hh):
        sl = slice(hh * HEAD_PAD, (hh + 1) * HEAD_PAD)
        q = q_ref[:, sl]

        def block(j, carry, masked):
            m, l, acc = carry
            r0 = pl.multiple_of(j * tile, tile)
            kj = k_ref[pl.ds(r0, tile), sl]
            vj = v_ref[pl.ds(r0, tile), :]
            s = lax.dot_general(q, kj, (((1,), (1,)), ((), ())), preferred_element_type=F32)
            if masked:
                s = jnp.where(causal, s, NEG_BIG)
            m_new = jnp.maximum(m, jnp.max(s, axis=-1, keepdims=True))
            p = jnp.exp2(s - m_new)
            alpha = jnp.exp2(m - m_new)
            l = alpha * l + jnp.sum(p, axis=-1, keepdims=True)
            acc = alpha * acc + _dot(p.astype(BF16), vj)
            return m_new, l, acc

        init = (jnp.full((tile, 1), NEG_BIG, F32), jnp.zeros((tile, 1), F32),
                jnp.zeros((tile, 2 * V_HEAD_DIM), F32))
        carry = lax.fori_loop(0, i, lambda j, c: block(j, c, False), init)
        _, l, acc = block(i, carry, True)
        return acc / l

    o0 = one_head(0)
    o1 = one_head(1)
    lane = lax.broadcasted_iota(jnp.int32, (1, 2 * V_HEAD_DIM), 1)
    o_ref[...] = jnp.where(lane < V_HEAD_DIM, o0, o1).astype(BF16)


def _attention(q_all, k_all, v_all):
    b, l, _ = q_all.shape
    t = ATTN_TILE
    return pl.pallas_call(
        functools.partial(_attn_kernel, tile=t),
        grid=(b, N_PAIRS, l // t),
        in_specs=[
            pl.BlockSpec((None, t, 2 * HEAD_PAD), lambda bi, p, i: (bi, i, p)),
            pl.BlockSpec((None, l, 2 * HEAD_PAD), lambda bi, p, i: (bi, 0, p)),
            pl.BlockSpec((None, l, 2 * V_HEAD_DIM), lambda bi, p, i: (bi, 0, p)),
        ],
        out_specs=pl.BlockSpec((None, t, 2 * V_HEAD_DIM), lambda bi, p, i: (bi, i, p)),
        out_shape=jax.ShapeDtypeStruct((b, l, N_HEADS * V_HEAD_DIM), BF16),
        compiler_params=pltpu.CompilerParams(
            dimension_semantics=("parallel", "parallel", "parallel"), vmem_limit_bytes=VMEM_LIMIT),
        name="attention",
    )(q_all, k_all, v_all)


def _s5_kernel(u_ref, are_ref, aim_ref, wb_ref, wcr_ref, wci_ref, d_ref, wglu_ref, bglu_ref,
               o_ref, state_ref, sre_ref, sim_ref, *, steps, batch):
    @pl.when(pl.program_id(0) == 0)
    def _():
        state_ref[...] = jnp.zeros_like(state_ref)

    u = u_ref[...]
    ys = []
    for k in range(SSM_HALVES):
        bu = _dot(u[:, k * SSM_HALF_IN:(k + 1) * SSM_HALF_IN], wb_ref[k])
        sre_ref[...] = bu[:, :SSM_HALF_STATE]
        sim_ref[...] = bu[:, SSM_HALF_STATE:]
        ar = are_ref[:, k * SSM_HALF_STATE:(k + 1) * SSM_HALF_STATE]
        ai = aim_ref[:, k * SSM_HALF_STATE:(k + 1) * SSM_HALF_STATE]

        def step(t, carry):
            sr, si = carry
            r0 = pl.multiple_of(t * batch, batch)
            nr = ar * sr - ai * si + sre_ref[pl.ds(r0, batch), :]
            ni = ar * si + ai * sr + sim_ref[pl.ds(r0, batch), :]
            sre_ref[pl.ds(r0, batch), :] = nr
            sim_ref[pl.ds(r0, batch), :] = ni
            return nr, ni

        sr, si = lax.fori_loop(0, steps, step, (state_ref[k, 0], state_ref[k, 1]), unroll=4)
        state_ref[k, 0] = sr
        state_ref[k, 1] = si
        ys.append(_dot(sre_ref[...].astype(BF16), wcr_ref[k])
                  - _dot(sim_ref[...].astype(BF16), wci_ref[k]))

    y = jnp.concatenate(ys, axis=1) + d_ref[...] * u.astype(F32)
    g = jax.nn.gelu(y, approximate=True)
    z = _dot(g.astype(BF16), wglu_ref[...]) + bglu_ref[...]
    o_ref[...] = (g * jax.nn.sigmoid(z)).astype(BF16)


def _s5(u_rows, a_re_t, a_im_t, wb, wcr, wci, d_skip, w_glu, b_glu, batch):
    rows_total, width = u_rows.shape
    steps = SSM_CHUNK
    rows = steps * batch
    const2 = lambda i: (0, 0)
    const3 = lambda i: (0, 0, 0)
    return pl.pallas_call(
        functools.partial(_s5_kernel, steps=steps, batch=batch),
        grid=(rows_total // rows,),
        in_specs=[
            pl.BlockSpec((rows, width), lambda i: (i, 0)),
            pl.BlockSpec(a_re_t.shape, const2),
            pl.BlockSpec(a_im_t.shape, const2),
            pl.BlockSpec(wb.shape, const3),
            pl.BlockSpec(wcr.shape, const3),
            pl.BlockSpec(wci.shape, const3),
            pl.BlockSpec((1, width), const2),
            pl.BlockSpec((width, width), const2),
            pl.BlockSpec((1, width), const2),
        ],
        out_specs=pl.BlockSpec((rows, width), lambda i: (i, 0)),
        out_shape=jax.ShapeDtypeStruct((rows_total, width), BF16),
        scratch_shapes=[
            pltpu.VMEM((SSM_HALVES, 2, batch, SSM_HALF_STATE), F32),
            pltpu.VMEM((rows, SSM_HALF_STATE), F32),
            pltpu.VMEM((rows, SSM_HALF_STATE), F32),
        ],
        compiler_params=pltpu.CompilerParams(
            dimension_semantics=("arbitrary",), vmem_limit_bytes=VMEM_LIMIT),
        name="s5",
    )(u_rows, a_re_t, a_im_t, wb, wcr, wci, d_skip, w_glu, b_glu)


def _merge_kernel(x_ref, attn_ref, ssm_ref, gate_ref, wba_ref, wbs_ref, wout_ref, gpost_ref, o_ref):
    ga = gate_ref[:, :D_MODEL].astype(F32)
    gs = gate_ref[:, D_MODEL:].astype(F32)
    merged = ga * _dot(attn_ref[...], wba_ref[...]) + gs * _dot(ssm_ref[...], wbs_ref[...])
    m2 = _dot(merged.astype(BF16), wout_ref[...])
    o_ref[...] = x_ref[...] + _rms(m2, gpost_ref[...])


def _merge(x, attn, ssm_tm, gates, w_ba, w_bs, w_out, g_post):
    b, l, d = x.shape
    tl = ROW_TILE
    const = lambda *_: (0, 0)
    return pl.pallas_call(
        _merge_kernel,
        grid=(b, l // tl),
        in_specs=[
            pl.BlockSpec((None, tl, d), lambda bi, i: (bi, i, 0)),
            pl.BlockSpec((None, tl, N_HEADS * V_HEAD_DIM), lambda bi, i: (bi, i, 0)),
            pl.BlockSpec((tl, SSM_WIDTH), lambda bi, i: (i, bi)),
            pl.BlockSpec((None, tl, 2 * d), lambda bi, i: (bi, i, 0)),
            pl.BlockSpec(w_ba.shape, const),
            pl.BlockSpec(w_bs.shape, const),
            pl.BlockSpec(w_out.shape, const),
            pl.BlockSpec((1, d), const),
        ],
        out_specs=pl.BlockSpec((None, tl, d), lambda bi, i: (bi, i, 0)),
        out_shape=jax.ShapeDtypeStruct((b, l, d), F32),
        compiler_params=pltpu.CompilerParams(
            dimension_semantics=("parallel", "parallel"), vmem_limit_bytes=VMEM_LIMIT),
        name="merge",
    )(x, attn, ssm_tm, gates, w_ba, w_bs, w_out, g_post)


def _ffn_kernel(x_ref, halo_ref, gpre_ref, wup_ref, cw_ref, cb_ref, wdown_ref, gpost_ref, o_ref,
                act_ref, *, tl):
    i = pl.program_id(1)
    x = x_ref[...]
    g = gpre_ref[...]
    hn_t = _rms(x, g).astype(BF16)
    keep = (i > 0).astype(F32)
    hn_h = (_rms(halo_ref[...], g) * keep).astype(BF16)
    hn = jnp.concatenate([hn_h, hn_t], axis=0)

    def conv(h, c0):
        cols = slice(c0, c0 + FFN_COL_CHUNK)
        out = cb_ref[:, cols] + cw_ref[2:3, cols] * h[HALO:]
        out = out + cw_ref[1:2, cols] * pltpu.roll(h, 1, 0)[HALO:]
        return out + cw_ref[0:1, cols] * pltpu.roll(h, 2, 0)[HALO:]

    for j in range(D_FF // FFN_COL_CHUNK):
        c0 = j * FFN_COL_CHUNK
        hg = _dot(hn, wup_ref[:, c0:c0 + FFN_COL_CHUNK])
        hv = _dot(hn, wup_ref[:, D_FF + c0:D_FF + c0 + FFN_COL_CHUNK])
        act = jax.nn.gelu(conv(hg, c0), approximate=True) * conv(hv, D_FF + c0)
        act_ref[:, c0:c0 + FFN_COL_CHUNK] = act.astype(BF16)

    ff = _dot(act_ref[...], wdown_ref[...])
    o_ref[...] = x + _rms(ff, gpost_ref[...])


def _conv_ffn(x1, g_pre, w_up, conv_w, conv_b, w_down, g_post):
    b, l, d = x1.shape
    tl = ROW_TILE
    const = lambda *_: (0, 0)
    halo_blocks = tl // HALO
    return pl.pallas_call(
        functools.partial(_ffn_kernel, tl=tl),
        grid=(b, l // tl),
        in_specs=[
            pl.BlockSpec((None, tl, d), lambda bi, i: (bi, i, 0)),
            pl.BlockSpec((None, HALO, d), lambda bi, i: (bi, jnp.maximum(i * halo_blocks - 1, 0), 0)),
            pl.BlockSpec((1, d), const),
            pl.BlockSpec(w_up.shape, const, pipeline_mode=pl.Buffered(1)),
            pl.BlockSpec(conv_w.shape, const),
            pl.BlockSpec(conv_b.shape, const),
            pl.BlockSpec(w_down.shape, const, pipeline_mode=pl.Buffered(1)),
            pl.BlockSpec((1, d), const),
        ],
        out_specs=pl.BlockSpec((None, tl, d), lambda bi, i: (bi, i, 0)),
        out_shape=jax.ShapeDtypeStruct((b, l, d), F32),
        scratch_shapes=[pltpu.VMEM((tl, D_FF), BF16)],
        compiler_params=pltpu.CompilerParams(
            dimension_semantics=("parallel", "parallel"), vmem_limit_bytes=VMEM_LIMIT),
        name="conv_ffn",
    )(x1, x1, g_pre, w_up, conv_w, conv_b, w_down, g_post)


def _pad_heads(w, per_head):
    k = w.shape[0]
    w = w.reshape(k, N_HEADS, per_head)
    w = jnp.pad(w, ((0, 0), (0, 0), (0, HEAD_PAD - per_head)))
    return w.reshape(k, N_HEADS * HEAD_PAD)


def _block_diag(blocks):
    g, r, c = blocks.shape
    idx = jnp.arange(g)
    out = jnp.zeros((g, r, g, c), blocks.dtype).at[idx, :, idx, :].set(blocks)
    return out.reshape(g * r, g * c)


def _diag_halves(full, rows, cols):
    return jnp.stack([full[k * rows:(k + 1) * rows, k * cols:(k + 1) * cols] for k in range(SSM_HALVES)])


def _layer(x, pos_f, invf, p):
    b, l, d = x.shape
    row = lambda v: v.reshape(1, -1)

    w_in = p["w_in"]
    kr_cols = jnp.pad(w_in[:, P_KR:P_KR + QK_ROPE_DIM],
                      ((0, 0), (ROPE_LANE0, LANES - ROPE_LANE0 - QK_ROPE_DIM)))
    off_u = P_KR + QK_ROPE_DIM
    w_in_p = jnp.concatenate([w_in[:, :P_KR], kr_cols, w_in[:, off_u:]], axis=1).astype(BF16)
    w_uq_p = _pad_heads(p["w_uq"], QK_HEAD_DIM).astype(BF16)
    w_uk_p = _pad_heads(p["w_uk"], QK_NOPE_DIM).astype(BF16)

    q_all, k_all, v_all, u_tm, gates = _in_proj(
        x, pos_f, invf, row(p["mix_norm_pre"]), w_in_p, row(p["q_norm"]), w_uq_p, row(p["kv_norm"]),
        w_uk_p, p["w_uv"].astype(BF16), row(p["b_gate"]))

    attn = _attention(q_all, k_all, v_all)

    a_re, a_im, bb_re, bb_im = _s5_prep(
        p["ssm_lambda_re"], p["ssm_lambda_im"], p["ssm_log_dt"].reshape(SSM_GROUPS, 1),
        p["ssm_b_re"].transpose(0, 2, 1), p["ssm_b_im"].transpose(0, 2, 1))
    a_re_t = jnp.broadcast_to(a_re.reshape(1, -1), (b, SSM_GROUPS * SSM_STATE))
    a_im_t = jnp.broadcast_to(a_im.reshape(1, -1), (b, SSM_GROUPS * SSM_STATE))
    wb = jnp.concatenate([_diag_halves(_block_diag(bb_re), SSM_HALF_IN, SSM_HALF_STATE),
                          _diag_halves(_block_diag(bb_im), SSM_HALF_IN, SSM_HALF_STATE)],
                         axis=2).astype(BF16)
    wcr = _diag_halves(_block_diag(p["ssm_c_re"].transpose(0, 2, 1)), SSM_HALF_STATE, SSM_HALF_IN).astype(BF16)
    wci = _diag_halves(_block_diag(p["ssm_c_im"].transpose(0, 2, 1)), SSM_HALF_STATE, SSM_HALF_IN).astype(BF16)
    ssm_rows = _s5(u_tm.reshape(l * b, SSM_WIDTH), a_re_t, a_im_t, wb, wcr, wci,
                   row(p["ssm_d"]), p["w_glu"].astype(BF16), row(p["b_glu"]), b)
    ssm_tm = ssm_rows.reshape(l, b * SSM_WIDTH)

    x1 = _merge(x, attn, ssm_tm, gates, p["w_branch_attn"].astype(BF16), p["w_branch_ssm"].astype(BF16),
                p["w_out"].astype(BF16), row(p["mix_norm_post"]))
    return _conv_ffn(x1, row(p["ffn_norm_pre"]), p["w_up"].astype(BF16), p["conv_w"], row(p["conv_b"]),
                     p["w_down"].astype(BF16), row(p["ffn_norm_post"]))


def kernel(x, positions, mix_norm_pre, w_in, q_norm, w_uq, kv_norm, w_uk, w_uv, ssm_lambda_re, ssm_lambda_im, ssm_log_dt, ssm_b_re, ssm_b_im, ssm_c_re, ssm_c_im, ssm_d, w_glu, b_glu, w_branch_attn, w_branch_ssm, b_gate, w_out, mix_norm_post, ffn_norm_pre, w_up, conv_w, conv_b, w_down, ffn_norm_post):
    b, l, _ = x.shape
    params = dict(mix_norm_pre=mix_norm_pre, w_in=w_in, q_norm=q_norm, w_uq=w_uq, kv_norm=kv_norm,
                  w_uk=w_uk, w_uv=w_uv, ssm_lambda_re=ssm_lambda_re, ssm_lambda_im=ssm_lambda_im,
                  ssm_log_dt=ssm_log_dt, ssm_b_re=ssm_b_re, ssm_b_im=ssm_b_im, ssm_c_re=ssm_c_re,
                  ssm_c_im=ssm_c_im, ssm_d=ssm_d, w_glu=w_glu, b_glu=b_glu, w_branch_attn=w_branch_attn,
                  w_branch_ssm=w_branch_ssm, b_gate=b_gate, w_out=w_out, mix_norm_post=mix_norm_post,
                  ffn_norm_pre=ffn_norm_pre, w_up=w_up, conv_w=conv_w, conv_b=conv_b, w_down=w_down,
                  ffn_norm_post=ffn_norm_post)
    inv_freq = ROPE_THETA ** (-jnp.arange(0, QK_ROPE_DIM, 2, dtype=F32) / QK_ROPE_DIM)
    invf = jnp.zeros((1, LANES), F32)
    invf = invf.at[0, ROPE_LANE0:ROPE_LANE0 + ROPE_HALF].set(inv_freq)
    invf = invf.at[0, ROPE_LANE0 + ROPE_HALF:ROPE_LANE0 + QK_ROPE_DIM].set(inv_freq)
    pos_f = positions.astype(F32).reshape(b, l, 1)
    for layer in range(mix_norm_pre.shape[0]):
        x = _layer(x, pos_f, invf, {k: v[layer] for k, v in params.items()})
    return x
```

```python
import functools
import math

import jax
import jax.numpy as jnp
from jax import lax
from jax.experimental import pallas as pl
from jax.experimental.pallas import tpu as pltpu

D_MODEL = 1024
N_HEADS = 8
QK_NOPE_DIM = 64
QK_ROPE_DIM = 32
QK_HEAD_DIM = QK_NOPE_DIM + QK_ROPE_DIM
V_HEAD_DIM = 64
Q_LORA_RANK = 384
KV_LORA_RANK = 256
ROPE_THETA = 10000.0
SSM_WIDTH = 512
SSM_GROUP = 16
SSM_GROUPS = SSM_WIDTH // SSM_GROUP
SSM_STATE = 64
D_FF = 2816
CONV_WIDTH = 3
EPS = 1e-6

LANES = 128
HEAD_PAD = LANES
ROPE_LANE0 = QK_NOPE_DIM
ROPE_HALF = QK_ROPE_DIM // 2
N_PAIRS = N_HEADS // 2
SUM_ROW = (V_HEAD_DIM, 0)
SSM_HALVES = 2
SSM_HALF_IN = SSM_WIDTH // SSM_HALVES
SSM_HALF_STATE = SSM_GROUPS * SSM_STATE // SSM_HALVES

P_CQ = 0
P_CKV = P_CQ + Q_LORA_RANK
P_KR = P_CKV + KV_LORA_RANK
P_U = P_KR + LANES
P_GATE = P_U + SSM_WIDTH
P_END = P_GATE + 2 * D_MODEL

Q_SCALE = (1.0 / math.sqrt(QK_HEAD_DIM)) * math.log2(math.e)
NEG_BIG = -1e30

ROW_TILE = 512
ATTN_TILE = ROW_TILE
ATTN_KEY_BLOCK = ATTN_TILE // 2
SSM_CHUNK = 64
FFN_COL_CHUNK = 256
HALO = 16
VMEM_LIMIT = 56 * 1024 * 1024

BF16 = jnp.bfloat16
F32 = jnp.float32


def _rms(x, g):
    return x * lax.rsqrt(jnp.mean(x * x, axis=-1, keepdims=True) + EPS) * g


def _dot(a, b):
    return jnp.dot(a, b, preferred_element_type=F32)


def _s5_prep_kernel(lr_ref, li_ref, logdt_ref, br_ref, bi_ref, are_ref, aim_ref, bbr_ref, bbi_ref):
    lr = lr_ref[...]
    li = li_ref[...]
    dt = jnp.exp(logdt_ref[...])
    mag = jnp.exp(lr * dt)
    ang = li * dt
    a_re = mag * jnp.cos(ang)
    a_im = mag * jnp.sin(ang)
    den = lr * lr + li * li
    n_re = a_re - 1.0
    n_im = a_im
    z_re = (n_re * lr + n_im * li) / den
    z_im = (n_im * lr - n_re * li) / den
    are_ref[...] = a_re
    aim_ref[...] = a_im
    br = br_ref[...]
    bi = bi_ref[...]
    zr = z_re[:, None, :]
    zi = z_im[:, None, :]
    bbr_ref[...] = zr * br - zi * bi
    bbi_ref[...] = zr * bi + zi * br


def _s5_prep(lam_re, lam_im, log_dt, b_re_t, b_im_t):
    g, p = lam_re.shape
    h = b_re_t.shape[1]
    return pl.pallas_call(
        _s5_prep_kernel,
        out_shape=(jax.ShapeDtypeStruct((g, p), F32), jax.ShapeDtypeStruct((g, p), F32),
                   jax.ShapeDtypeStruct((g, h, p), F32), jax.ShapeDtypeStruct((g, h, p), F32)),
        name="s5_prep",
    )(lam_re, lam_im, log_dt, b_re_t, b_im_t)


def _in_proj_kernel(x_ref, pos_ref, invf_ref, gpre_ref, win_ref, qn_ref, wuq_ref, kvn_ref, wuk_ref,
                    wuv_ref, bg_ref, q_ref, k_ref, vt_ref, u_ref, gate_ref):
    x = x_ref[...]
    hn = _rms(x, gpre_ref[...]).astype(BF16)

    ang = pos_ref[...] * invf_ref[...]
    cos = jnp.cos(ang)
    sin = jnp.sin(ang)
    lane = lax.broadcasted_iota(jnp.int32, (1, LANES), 1)
    sin_hi = jnp.where(lane >= ROPE_LANE0 + ROPE_HALF, sin, 0.0)
    sin_lo = jnp.where(lane < ROPE_LANE0 + ROPE_HALF, -sin, 0.0)

    def rope(t):
        return (t * cos + pltpu.roll(t, ROPE_HALF, 1) * sin_hi
                + pltpu.roll(t, LANES - ROPE_HALF, 1) * sin_lo)

    cq = _dot(hn, win_ref[:, P_CQ:P_CKV])
    cqn = _rms(cq, qn_ref[...]).astype(BF16)
    q = _dot(cqn, wuq_ref[...])
    for h in range(N_HEADS):
        sl = slice(h * HEAD_PAD, (h + 1) * HEAD_PAD)
        q_ref[:, sl] = (rope(q[:, sl]) * Q_SCALE).astype(BF16)

    ckv = _dot(hn, win_ref[:, P_CKV:P_KR])
    ckvn = _rms(ckv, kvn_ref[...]).astype(BF16)
    v_t = _dot(ckvn, wuv_ref[...]).T.astype(BF16)
    for c in range(vt_ref.shape[0]):
        vt_ref[c] = v_t[:, c * ATTN_KEY_BLOCK:(c + 1) * ATTN_KEY_BLOCK]
    kn = _dot(ckvn, wuk_ref[...])
    kr = rope(_dot(hn, win_ref[:, P_KR:P_U]))
    for h in range(N_HEADS):
        sl = slice(h * HEAD_PAD, (h + 1) * HEAD_PAD)
        k_ref[:, sl] = (kn[:, sl] + kr).astype(BF16)

    u_ref[...] = _dot(hn, win_ref[:, P_U:P_GATE]).astype(BF16)
    logits = _dot(hn, win_ref[:, P_GATE:P_END]) + bg_ref[...]
    gate_ref[...] = jax.nn.sigmoid(logits).astype(BF16)


def _in_proj(x, pos_f, invf, g_pre, w_in_p, q_norm, w_uq_p, kv_norm, w_uk_p, w_uv, b_gate):
    b, l, d = x.shape
    tl = ROW_TILE
    const = lambda *_: (0, 0)
    return pl.pallas_call(
        _in_proj_kernel,
        grid=(b, l // tl),
        in_specs=[
            pl.BlockSpec((None, tl, d), lambda bi, i: (bi, i, 0)),
            pl.BlockSpec((None, tl, 1), lambda bi, i: (bi, i, 0)),
            pl.BlockSpec((1, LANES), const),
            pl.BlockSpec((1, d), const),
            pl.BlockSpec((d, P_END), const),
            pl.BlockSpec((1, Q_LORA_RANK), const),
            pl.BlockSpec((Q_LORA_RANK, N_HEADS * HEAD_PAD), const),
            pl.BlockSpec((1, KV_LORA_RANK), const),
            pl.BlockSpec((KV_LORA_RANK, N_HEADS * HEAD_PAD), const),
            pl.BlockSpec((KV_LORA_RANK, N_HEADS * V_HEAD_DIM), const),
            pl.BlockSpec((1, 2 * D_MODEL), const),
        ],
        out_specs=[
            pl.BlockSpec((None, tl, N_HEADS * HEAD_PAD), lambda bi, i: (bi, i, 0)),
            pl.BlockSpec((None, tl, N_HEADS * HEAD_PAD), lambda bi, i: (bi, i, 0)),
            pl.BlockSpec((None, tl // ATTN_KEY_BLOCK, N_HEADS * V_HEAD_DIM, ATTN_KEY_BLOCK),
                         lambda bi, i: (bi, i, 0, 0)),
            pl.BlockSpec((tl, SSM_WIDTH), lambda bi, i: (i, bi)),
            pl.BlockSpec((None, tl, 2 * D_MODEL), lambda bi, i: (bi, i, 0)),
        ],
        out_shape=[
            jax.ShapeDtypeStruct((b, l, N_HEADS * HEAD_PAD), BF16),
            jax.ShapeDtypeStruct((b, l, N_HEADS * HEAD_PAD), BF16),
            jax.ShapeDtypeStruct((b, l // ATTN_KEY_BLOCK, N_HEADS * V_HEAD_DIM, ATTN_KEY_BLOCK), BF16),
            jax.ShapeDtypeStruct((l, b * SSM_WIDTH), BF16),
            jax.ShapeDtypeStruct((b, l, 2 * D_MODEL), BF16),
        ],
        compiler_params=pltpu.CompilerParams(
            dimension_semantics=("parallel", "parallel"), vmem_limit_bytes=VMEM_LIMIT),
        name="in_proj",
    )(x, pos_f, invf, g_pre, w_in_p, q_norm, w_uq_p, kv_norm, w_uk_p, w_uv, b_gate)


def _attn_kernel(q_ref, k_ref, vt_ref, o_ref, s_ref, mx_ref, m_ref, acc_ref, *, tq, tk):
    i = pl.program_id(2)
    key = lax.broadcasted_iota(jnp.int32, (tk, tq), 0)
    qry = lax.broadcasted_iota(jnp.int32, (tk, tq), 1)
    diag_masks = (key <= qry, key + tk <= qry)
    qs = [q_ref[:, hh * HEAD_PAD:(hh + 1) * HEAD_PAD] for hh in range(2)]

    def scores(j, slot, mask=None):
        r0 = pl.multiple_of(j * tk, tk)
        for hh in range(2):
            kj = k_ref[pl.ds(r0, tk), hh * HEAD_PAD:(hh + 1) * HEAD_PAD]
            s = lax.dot_general(kj, qs[hh], (((1,), (1,)), ((), ())), preferred_element_type=F32)
            if mask is not None:
                s = jnp.where(mask, s, NEG_BIG)
            s_ref[slot, hh] = s
            mx_ref[slot, hh] = jnp.max(s, axis=0, keepdims=True)

    vrow = lax.broadcasted_iota(jnp.int32, (2 * V_HEAD_DIM, tk), 0)
    one = jnp.ones((), BF16)

    def softmax_pv(j, slot):
        vt = vt_ref[j]
        for hh in range(2):
            m = m_ref[hh]
            m_new = jnp.maximum(m, mx_ref[slot, hh])
            alpha = jnp.exp2(m - m_new)
            p = jnp.exp2(s_ref[slot, hh] - m_new)
            vt_h = jnp.where(vrow == SUM_ROW[hh], one, vt)
            acc_ref[hh] = alpha * acc_ref[hh] + _dot(vt_h, p.astype(BF16))
            m_ref[hh] = m_new

    m_ref[...] = jnp.full(m_ref.shape, NEG_BIG, F32)
    acc_ref[...] = jnp.zeros(acc_ref.shape, F32)

    @pl.when(i == 0)
    def _():
        scores(0, 0, diag_masks[0])

    @pl.when(i > 0)
    def _():
        scores(0, 0)

    def two_blocks(t, last):
        scores(2 * t + 1, 1)
        softmax_pv(2 * t, 0)
        scores(2 * t + 2, 0, diag_masks[0] if last else None)
        softmax_pv(2 * t + 1, 1)

    def body(t, carry):
        two_blocks(t, False)
        return carry

    lax.fori_loop(0, i - 1, body, 0)

    @pl.when(i > 0)
    def _():
        two_blocks(i - 1, True)

    scores(2 * i + 1, 1, diag_masks[1])
    softmax_pv(2 * i, 0)
    softmax_pv(2 * i + 1, 1)

    a0 = acc_ref[0]
    a1 = acc_ref[1]
    o0 = a0[:V_HEAD_DIM] / a0[SUM_ROW[0]:SUM_ROW[0] + 1]
    o1 = a1[V_HEAD_DIM:] / a1[SUM_ROW[1]:SUM_ROW[1] + 1]
    out_t = jnp.concatenate([o0, o1], axis=0)
    o_ref[...] = out_t.T.astype(BF16)


def _attention(q_all, k_all, v_t):
    b, l, _ = q_all.shape
    tq, tk = ATTN_TILE, ATTN_KEY_BLOCK
    return pl.pallas_call(
        functools.partial(_attn_kernel, tq=tq, tk=tk),
        grid=(b, N_PAIRS, l // tq),
        in_specs=[
            pl.BlockSpec((None, tq, 2 * HEAD_PAD), lambda bi, p, i: (bi, i, p)),
            pl.BlockSpec((None, l, 2 * HEAD_PAD), lambda bi, p, i: (bi, 0, p)),
            pl.BlockSpec((None, l // tk, 2 * V_HEAD_DIM, tk), lambda bi, p, i: (bi, 0, p, 0)),
        ],
        out_specs=pl.BlockSpec((None, tq, 2 * V_HEAD_DIM), lambda bi, p, i: (bi, i, p)),
        out_shape=jax.ShapeDtypeStruct((b, l, N_HEADS * V_HEAD_DIM), BF16),
        scratch_shapes=[
            pltpu.VMEM((2, 2, tk, tq), F32),
            pltpu.VMEM((2, 2, 1, tq), F32),
            pltpu.VMEM((2, 1, tq), F32),
            pltpu.VMEM((2, 2 * V_HEAD_DIM, tq), F32),
        ],
        compiler_params=pltpu.CompilerParams(
            dimension_semantics=("parallel", "parallel", "parallel"), vmem_limit_bytes=VMEM_LIMIT),
        name="attention",
    )(q_all, k_all, v_t)


def _s5_kernel(u_ref, are_ref, aim_ref, wb_ref, wcr_ref, wci_ref, d_ref, wglu_ref, bglu_ref,
               o_ref, state_ref, sre_ref, sim_ref, tb_ref, *, steps, batch):
    @pl.when(pl.program_id(0) == 0)
    def _():
        state_ref[...] = jnp.zeros_like(state_ref)

    for b in range(batch):
        tb_ref[:, b, :] = u_ref[:, b * SSM_WIDTH:(b + 1) * SSM_WIDTH].astype(F32)
    u = tb_ref[...].reshape(steps * batch, SSM_WIDTH).astype(BF16)
    ys = []
    for k in range(SSM_HALVES):
        bu = _dot(u[:, k * SSM_HALF_IN:(k + 1) * SSM_HALF_IN], wb_ref[k])
        sre_ref[...] = bu[:, :SSM_HALF_STATE]
        sim_ref[...] = bu[:, SSM_HALF_STATE:]
        ar = are_ref[:, k * SSM_HALF_STATE:(k + 1) * SSM_HALF_STATE]
        ai = aim_ref[:, k * SSM_HALF_STATE:(k + 1) * SSM_HALF_STATE]

        def step(t, carry):
            sr, si = carry
            r0 = pl.multiple_of(t * batch, batch)
            nr = ar * sr - ai * si + sre_ref[pl.ds(r0, batch), :]
            ni = ar * si + ai * sr + sim_ref[pl.ds(r0, batch), :]
            sre_ref[pl.ds(r0, batch), :] = nr
            sim_ref[pl.ds(r0, batch), :] = ni
            return nr, ni

        sr, si = lax.fori_loop(0, steps, step, (state_ref[k, 0], state_ref[k, 1]), unroll=4)
        state_ref[k, 0] = sr
        state_ref[k, 1] = si
        ys.append(_dot(sre_ref[...].astype(BF16), wcr_ref[k])
                  - _dot(sim_ref[...].astype(BF16), wci_ref[k]))

    y = jnp.concatenate(ys, axis=1) + d_ref[...] * u.astype(F32)
    g = jax.nn.gelu(y, approximate=True)
    z = _dot(g.astype(BF16), wglu_ref[...]) + bglu_ref[...]
    tb_ref[...] = (g * jax.nn.sigmoid(z)).reshape(steps, batch, SSM_WIDTH)
    for b in range(batch):
        o_ref[:, b * SSM_WIDTH:(b + 1) * SSM_WIDTH] = tb_ref[:, b, :].astype(BF16)


def _s5(u_tm, a_re_t, a_im_t, wb, wcr, wci, d_skip, w_glu, b_glu, batch):
    length = u_tm.shape[0]
    width = SSM_WIDTH
    steps = SSM_CHUNK
    rows = steps * batch
    const2 = lambda i: (0, 0)
    const3 = lambda i: (0, 0, 0)
    return pl.pallas_call(
        functools.partial(_s5_kernel, steps=steps, batch=batch),
        grid=(length // steps,),
        in_specs=[
            pl.BlockSpec((steps, batch * width), lambda i: (i, 0)),
            pl.BlockSpec(a_re_t.shape, const2),
            pl.BlockSpec(a_im_t.shape, const2),
            pl.BlockSpec(wb.shape, const3),
            pl.BlockSpec(wcr.shape, const3),
            pl.BlockSpec(wci.shape, const3),
            pl.BlockSpec((1, width), const2),
            pl.BlockSpec((width, width), const2),
            pl.BlockSpec((1, width), const2),
        ],
        out_specs=pl.BlockSpec((steps, batch * width), lambda i: (i, 0)),
        out_shape=jax.ShapeDtypeStruct((length, batch * width), BF16),
        scratch_shapes=[
            pltpu.VMEM((SSM_HALVES, 2, batch, SSM_HALF_STATE), F32),
            pltpu.VMEM((rows, SSM_HALF_STATE), F32),
            pltpu.VMEM((rows, SSM_HALF_STATE), F32),
            pltpu.VMEM((steps, batch, width), F32),
        ],
        compiler_params=pltpu.CompilerParams(
            dimension_semantics=("arbitrary",), vmem_limit_bytes=VMEM_LIMIT),
        name="s5",
    )(u_tm, a_re_t, a_im_t, wb, wcr, wci, d_skip, w_glu, b_glu)


def _merge_kernel(x_ref, attn_ref, ssm_ref, gate_ref, wba_ref, wbs_ref, wout_ref, gpost_ref, o_ref):
    ga = gate_ref[:, :D_MODEL].astype(F32)
    gs = gate_ref[:, D_MODEL:].astype(F32)
    merged = ga * _dot(attn_ref[...], wba_ref[...]) + gs * _dot(ssm_ref[...], wbs_ref[...])
    m2 = _dot(merged.astype(BF16), wout_ref[...])
    o_ref[...] = x_ref[...] + _rms(m2, gpost_ref[...])


def _merge(x, attn, ssm_tm, gates, w_ba, w_bs, w_out, g_post):
    b, l, d = x.shape
    tl = ROW_TILE
    const = lambda *_: (0, 0)
    return pl.pallas_call(
        _merge_kernel,
        grid=(b, l // tl),
        in_specs=[
            pl.BlockSpec((None, tl, d), lambda bi, i: (bi, i, 0)),
            pl.BlockSpec((None, tl, N_HEADS * V_HEAD_DIM), lambda bi, i: (bi, i, 0)),
            pl.BlockSpec((tl, SSM_WIDTH), lambda bi, i: (i, bi)),
            pl.BlockSpec((None, tl, 2 * d), lambda bi, i: (bi, i, 0)),
            pl.BlockSpec(w_ba.shape, const),
            pl.BlockSpec(w_bs.shape, const),
            pl.BlockSpec(w_out.shape, const),
            pl.BlockSpec((1, d), const),
        ],
        out_specs=pl.BlockSpec((None, tl, d), lambda bi, i: (bi, i, 0)),
        out_shape=jax.ShapeDtypeStruct((b, l, d), F32),
        compiler_params=pltpu.CompilerParams(
            dimension_semantics=("parallel", "parallel"), vmem_limit_bytes=VMEM_LIMIT),
        name="merge",
    )(x, attn, ssm_tm, gates, w_ba, w_bs, w_out, g_post)


def _ffn_kernel(x_ref, halo_ref, gpre_ref, wup_ref, cw_ref, cb_ref, wdown_ref, gpost_ref, o_ref,
                act_ref, *, tl):
    i = pl.program_id(1)
    x = x_ref[...]
    g = gpre_ref[...]
    hn_t = _rms(x, g).astype(BF16)
    keep = (i > 0).astype(F32)
    hn_h = (_rms(halo_ref[...], g) * keep).astype(BF16)
    hn = jnp.concatenate([hn_h, hn_t], axis=0)

    def conv(h, c0):
        cols = slice(c0, c0 + FFN_COL_CHUNK)
        out = cb_ref[:, cols] + cw_ref[2:3, cols] * h[HALO:]
        out = out + cw_ref[1:2, cols] * pltpu.roll(h, 1, 0)[HALO:]
        return out + cw_ref[0:1, cols] * pltpu.roll(h, 2, 0)[HALO:]

    for j in range(D_FF // FFN_COL_CHUNK):
        c0 = j * FFN_COL_CHUNK
        hg = _dot(hn, wup_ref[:, c0:c0 + FFN_COL_CHUNK])
        hv = _dot(hn, wup_ref[:, D_FF + c0:D_FF + c0 + FFN_COL_CHUNK])
        act = jax.nn.gelu(conv(hg, c0), approximate=True) * conv(hv, D_FF + c0)
        act_ref[:, c0:c0 + FFN_COL_CHUNK] = act.astype(BF16)

    ff = _dot(act_ref[...], wdown_ref[...])
    o_ref[...] = x + _rms(ff, gpost_ref[...])


def _conv_ffn(x1, g_pre, w_up, conv_w, conv_b, w_down, g_post):
    b, l, d = x1.shape
    tl = ROW_TILE
    const = lambda *_: (0, 0)
    halo_blocks = tl // HALO
    return pl.pallas_call(
        functools.partial(_ffn_kernel, tl=tl),
        grid=(b, l // tl),
        in_specs=[
            pl.BlockSpec((None, tl, d), lambda bi, i: (bi, i, 0)),
            pl.BlockSpec((None, HALO, d), lambda bi, i: (bi, jnp.maximum(i * halo_blocks - 1, 0), 0)),
            pl.BlockSpec((1, d), const),
            pl.BlockSpec(w_up.shape, const, pipeline_mode=pl.Buffered(1)),
            pl.BlockSpec(conv_w.shape, const),
            pl.BlockSpec(conv_b.shape, const),
            pl.BlockSpec(w_down.shape, const, pipeline_mode=pl.Buffered(1)),
            pl.BlockSpec((1, d), const),
        ],
        out_specs=pl.BlockSpec((None, tl, d), lambda bi, i: (bi, i, 0)),
        out_shape=jax.ShapeDtypeStruct((b, l, d), F32),
        scratch_shapes=[pltpu.VMEM((tl, D_FF), BF16)],
        compiler_params=pltpu.CompilerParams(
            dimension_semantics=("parallel", "parallel"), vmem_limit_bytes=VMEM_LIMIT),
        name="conv_ffn",
    )(x1, x1, g_pre, w_up, conv_w, conv_b, w_down, g_post)


def _pad_heads(w, per_head):
    k = w.shape[0]
    w = w.reshape(k, N_HEADS, per_head)
    w = jnp.pad(w, ((0, 0), (0, 0), (0, HEAD_PAD - per_head)))
    return w.reshape(k, N_HEADS * HEAD_PAD)


def _block_diag(blocks):
    g, r, c = blocks.shape
    idx = jnp.arange(g)
    out = jnp.zeros((g, r, g, c), blocks.dtype).at[idx, :, idx, :].set(blocks)
    return out.reshape(g * r, g * c)


def _diag_halves(full, rows, cols):
    return jnp.stack([full[k * rows:(k + 1) * rows, k * cols:(k + 1) * cols] for k in range(SSM_HALVES)])


def _layer(x, pos_f, invf, p):
    b, l, d = x.shape
    row = lambda v: v.reshape(1, -1)

    w_in = p["w_in"]
    kr_cols = jnp.pad(w_in[:, P_KR:P_KR + QK_ROPE_DIM],
                      ((0, 0), (ROPE_LANE0, LANES - ROPE_LANE0 - QK_ROPE_DIM)))
    off_u = P_KR + QK_ROPE_DIM
    w_in_p = jnp.concatenate([w_in[:, :P_KR], kr_cols, w_in[:, off_u:]], axis=1).astype(BF16)
    w_uq_p = _pad_heads(p["w_uq"], QK_HEAD_DIM).astype(BF16)
    w_uk_p = _pad_heads(p["w_uk"], QK_NOPE_DIM).astype(BF16)

    q_all, k_all, v_t, u_tm, gates = _in_proj(
        x, pos_f, invf, row(p["mix_norm_pre"]), w_in_p, row(p["q_norm"]), w_uq_p, row(p["kv_norm"]),
        w_uk_p, p["w_uv"].astype(BF16), row(p["b_gate"]))

    attn = _attention(q_all, k_all, v_t)

    a_re, a_im, bb_re, bb_im = _s5_prep(
        p["ssm_lambda_re"], p["ssm_lambda_im"], p["ssm_log_dt"].reshape(SSM_GROUPS, 1),
        p["ssm_b_re"].transpose(0, 2, 1), p["ssm_b_im"].transpose(0, 2, 1))
    a_re_t = jnp.broadcast_to(a_re.reshape(1, -1), (b, SSM_GROUPS * SSM_STATE))
    a_im_t = jnp.broadcast_to(a_im.reshape(1, -1), (b, SSM_GROUPS * SSM_STATE))
    wb = jnp.concatenate([_diag_halves(_block_diag(bb_re), SSM_HALF_IN, SSM_HALF_STATE),
                          _diag_halves(_block_diag(bb_im), SSM_HALF_IN, SSM_HALF_STATE)],
                         axis=2).astype(BF16)
    wcr = _diag_halves(_block_diag(p["ssm_c_re"].transpose(0, 2, 1)), SSM_HALF_STATE, SSM_HALF_IN).astype(BF16)
    wci = _diag_halves(_block_diag(p["ssm_c_im"].transpose(0, 2, 1)), SSM_HALF_STATE, SSM_HALF_IN).astype(BF16)
    ssm_tm = _s5(u_tm, a_re_t, a_im_t, wb, wcr, wci,
                 row(p["ssm_d"]), p["w_glu"].astype(BF16), row(p["b_glu"]), b)

    x1 = _merge(x, attn, ssm_tm, gates, p["w_branch_attn"].astype(BF16), p["w_branch_ssm"].astype(BF16),
                p["w_out"].astype(BF16), row(p["mix_norm_post"]))
    return _conv_ffn(x1, row(p["ffn_norm_pre"]), p["w_up"].astype(BF16), p["conv_w"], row(p["conv_b"]),
                     p["w_down"].astype(BF16), row(p["ffn_norm_post"]))


def kernel(x, positions, mix_norm_pre, w_in, q_norm, w_uq, kv_norm, w_uk, w_uv, ssm_lambda_re, ssm_lambda_im, ssm_log_dt, ssm_b_re, ssm_b_im, ssm_c_re, ssm_c_im, ssm_d, w_glu, b_glu, w_branch_attn, w_branch_ssm, b_gate, w_out, mix_norm_post, ffn_norm_pre, w_up, conv_w, conv_b, w_down, ffn_norm_post):
    b, l, _ = x.shape
    params = dict(mix_norm_pre=mix_norm_pre, w_in=w_in, q_norm=q_norm, w_uq=w_uq, kv_norm=kv_norm,
                  w_uk=w_uk, w_uv=w_uv, ssm_lambda_re=ssm_lambda_re, ssm_lambda_im=ssm_lambda_im,
                  ssm_log_dt=ssm_log_dt, ssm_b_re=ssm_b_re, ssm_b_im=ssm_b_im, ssm_c_re=ssm_c_re,
                  ssm_c_im=ssm_c_im, ssm_d=ssm_d, w_glu=w_glu, b_glu=b_glu, w_branch_attn=w_branch_attn,
                  w_branch_ssm=w_branch_ssm, b_gate=b_gate, w_out=w_out, mix_norm_post=mix_norm_post,
                  ffn_norm_pre=ffn_norm_pre, w_up=w_up, conv_w=conv_w, conv_b=conv_b, w_down=w_down,
                  ffn_norm_post=ffn_norm_post)
    inv_freq = ROPE_THETA ** (-jnp.arange(0, QK_ROPE_DIM, 2, dtype=F32) / QK_ROPE_DIM)
    invf = jnp.zeros((1, LANES), F32)
    invf = invf.at[0, ROPE_LANE0:ROPE_LANE0 + ROPE_HALF].set(inv_freq)
    invf = invf.at[0, ROPE_LANE0 + ROPE_HALF:ROPE_LANE0 + QK_ROPE_DIM].set(inv_freq)
    pos_f = positions.astype(F32).reshape(b, l, 1)
    for layer in range(mix_norm_pre.shape[0]):
        x = _layer(x, pos_f, invf, {k: v[layer] for k, v in params.items()})
    return x
```

```python
import functools
import math

import jax
import jax.numpy as jnp
from jax import lax
from jax.experimental import pallas as pl
from jax.experimental.pallas import tpu as pltpu

D_MODEL = 1024
N_HEADS = 8
QK_NOPE_DIM = 64
QK_ROPE_DIM = 32
QK_HEAD_DIM = QK_NOPE_DIM + QK_ROPE_DIM
V_HEAD_DIM = 64
Q_LORA_RANK = 384
KV_LORA_RANK = 256
ROPE_THETA = 10000.0
SSM_WIDTH = 512
SSM_GROUP = 16
SSM_GROUPS = SSM_WIDTH // SSM_GROUP
SSM_STATE = 64
D_FF = 2816
CONV_WIDTH = 3
EPS = 1e-6

LANES = 128
HEAD_PAD = LANES
ROPE_LANE0 = QK_NOPE_DIM
ROPE_HALF = QK_ROPE_DIM // 2
N_PAIRS = N_HEADS // 2
SUM_ROW = (V_HEAD_DIM, 0)
SSM_HALVES = 2
SSM_HALF_IN = SSM_WIDTH // SSM_HALVES
SSM_HALF_STATE = SSM_GROUPS * SSM_STATE // SSM_HALVES

P_CQ = 0
P_CKV = P_CQ + Q_LORA_RANK
P_KR = P_CKV + KV_LORA_RANK
P_U = P_KR + LANES
P_GATE = P_U + SSM_WIDTH
P_END = P_GATE + 2 * D_MODEL

Q_SCALE = (1.0 / math.sqrt(QK_HEAD_DIM)) * math.log2(math.e)
NEG_BIG = -1e30

ROW_TILE = 512
ATTN_TILE = ROW_TILE
ATTN_KEY_BLOCK = ATTN_TILE // 2
SSM_CHUNK = 64
FFN_COL_CHUNK = 256
HALO = 16
VMEM_LIMIT = 56 * 1024 * 1024

BF16 = jnp.bfloat16
F32 = jnp.float32


def _rms(x, g):
    return x * lax.rsqrt(jnp.mean(x * x, axis=-1, keepdims=True) + EPS) * g


def _dot(a, b):
    return jnp.dot(a, b, preferred_element_type=F32)


def _s5_prep_kernel(lr_ref, li_ref, logdt_ref, br_ref, bi_ref, cr_ref, ci_ref,
                    are_ref, aim_ref, wb_ref, wcr_ref, wci_ref, *, batch):
    lr = lr_ref[...]
    li = li_ref[...]
    dt = jnp.exp(logdt_ref[...])
    mag = jnp.exp(lr * dt)
    ang = li * dt
    a_re = mag * jnp.cos(ang)
    a_im = mag * jnp.sin(ang)
    den = lr * lr + li * li
    n_re = a_re - 1.0
    n_im = a_im
    z_re = (n_re * lr + n_im * li) / den
    z_im = (n_im * lr - n_re * li) / den
    are_ref[...] = jnp.broadcast_to(a_re, (batch, a_re.shape[1]))
    aim_ref[...] = jnp.broadcast_to(a_im, (batch, a_im.shape[1]))
    br = br_ref[...]
    bi = bi_ref[...]
    bb_re = z_re * br - z_im * bi
    bb_im = z_re * bi + z_im * br

    groups_per_half = SSM_GROUPS // SSM_HALVES
    row_group = jnp.right_shift(lax.broadcasted_iota(jnp.int32, (SSM_HALF_IN, SSM_HALF_STATE), 0),
                                SSM_GROUP.bit_length() - 1)
    col_group = jnp.right_shift(lax.broadcasted_iota(jnp.int32, (SSM_HALF_IN, SSM_HALF_STATE), 1),
                                SSM_STATE.bit_length() - 1)
    on_diag = row_group == col_group

    def diag(v):
        return jnp.where(on_diag, jnp.concatenate([v] * groups_per_half, axis=0), 0.0)

    for k in range(SSM_HALVES):
        cols = slice(k * SSM_HALF_STATE, (k + 1) * SSM_HALF_STATE)
        wb_ref[k, :, :SSM_HALF_STATE] = diag(bb_re[:, cols]).astype(BF16)
        wb_ref[k, :, SSM_HALF_STATE:] = diag(bb_im[:, cols]).astype(BF16)
        wcr_ref[k] = diag(cr_ref[:, cols]).astype(BF16)
        wci_ref[k] = diag(ci_ref[:, cols]).astype(BF16)


def _s5_prep(lam_re, lam_im, log_dt, b_re, b_im, c_re, c_im, batch):
    n = SSM_GROUPS * SSM_STATE
    lanes = lambda v: v.reshape(1, n)
    chan_major = lambda v, perm: v.transpose(perm).reshape(SSM_GROUP, n)
    return pl.pallas_call(
        functools.partial(_s5_prep_kernel, batch=batch),
        out_shape=(jax.ShapeDtypeStruct((batch, n), F32), jax.ShapeDtypeStruct((batch, n), F32),
                   jax.ShapeDtypeStruct((SSM_HALVES, SSM_HALF_IN, 2 * SSM_HALF_STATE), BF16),
                   jax.ShapeDtypeStruct((SSM_HALVES, SSM_HALF_IN, SSM_HALF_STATE), BF16),
                   jax.ShapeDtypeStruct((SSM_HALVES, SSM_HALF_IN, SSM_HALF_STATE), BF16)),
        name="s5_prep",
    )(lanes(lam_re), lanes(lam_im), lanes(jnp.repeat(log_dt, SSM_STATE)),
      chan_major(b_re, (2, 0, 1)), chan_major(b_im, (2, 0, 1)),
      chan_major(c_re, (1, 0, 2)), chan_major(c_im, (1, 0, 2)))


def _in_proj_kernel(x_ref, pos_ref, gpre_ref, win_ref, qn_ref, wuq_ref, kvn_ref, wuk_ref,
                    wuv_ref, bg_ref, q_ref, k_ref, vt_ref, u_ref, gate_ref):
    x = x_ref[...]
    tl = x.shape[0]
    hn = _rms(x, gpre_ref[...]).astype(BF16)

    freq_idx = lax.broadcasted_iota(jnp.int32, (ROPE_HALF, 1), 0).astype(F32)
    inv_freq = jnp.exp(freq_idx * (-2.0 * math.log(ROPE_THETA) / QK_ROPE_DIM))
    ang = inv_freq * pos_ref[...]
    c16 = jnp.cos(ang)
    s16 = jnp.sin(ang)
    ones = jnp.ones((ROPE_LANE0, tl), F32)
    zeros = lambda n: jnp.zeros((n, tl), F32)
    cos = jnp.concatenate([ones, zeros(ROPE_HALF), c16, c16, zeros(ROPE_HALF)], axis=0).T
    sin = jnp.concatenate([zeros(ROPE_LANE0 + ROPE_HALF), s16, -s16, zeros(ROPE_HALF)], axis=0).T
    cos_q = cos * Q_SCALE
    sin_q = sin * Q_SCALE

    low = _dot(hn, win_ref[:, P_CQ:P_U])
    cq = low[:, P_CQ:P_CKV]
    ckv = low[:, P_CKV:P_KR]
    kr = low[:, P_KR:P_U]
    cqn = _rms(cq, qn_ref[...]).astype(BF16)
    q = _dot(cqn, wuq_ref[...])
    for h in range(N_HEADS):
        sl = slice(h * HEAD_PAD, (h + 1) * HEAD_PAD)
        t = q[:, sl]
        q_ref[:, sl] = (t * cos_q + pltpu.roll(t, ROPE_HALF, 1) * sin_q).astype(BF16)

    ckvn = _rms(ckv, kvn_ref[...]).astype(BF16)
    v_t = lax.dot_general(wuv_ref[...], ckvn, (((1,), (1,)), ((), ())),
                          preferred_element_type=F32).astype(BF16)
    for c in range(vt_ref.shape[0]):
        vt_ref[c] = v_t[:, c * ATTN_KEY_BLOCK:(c + 1) * ATTN_KEY_BLOCK]
    kn = _dot(ckvn, wuk_ref[...])
    kr = kr * cos + pltpu.roll(kr, ROPE_HALF, 1) * sin
    for h in range(N_HEADS):
        sl = slice(h * HEAD_PAD, (h + 1) * HEAD_PAD)
        k_ref[:, sl] = (kn[:, sl] + kr).astype(BF16)

    u_ref[...] = _dot(hn, win_ref[:, P_U:P_GATE]).astype(BF16)
    logits = _dot(hn, win_ref[:, P_GATE:P_END]) + bg_ref[...]
    gate_ref[...] = jax.nn.sigmoid(logits).astype(BF16)


def _in_proj(x, pos_f, g_pre, w_in_p, q_norm, w_uq_p, kv_norm, w_uk_p, w_uv, b_gate):
    b, l, d = x.shape
    tl = ROW_TILE
    const = lambda *_: (0, 0)
    return pl.pallas_call(
        _in_proj_kernel,
        grid=(b, l // tl),
        in_specs=[
            pl.BlockSpec((None, tl, d), lambda bi, i: (bi, i, 0)),
            pl.BlockSpec((None, 1, tl), lambda bi, i: (bi, 0, i)),
            pl.BlockSpec((1, d), const),
            pl.BlockSpec((d, P_END), const),
            pl.BlockSpec((1, Q_LORA_RANK), const),
            pl.BlockSpec((Q_LORA_RANK, N_HEADS * HEAD_PAD), const),
            pl.BlockSpec((1, KV_LORA_RANK), const),
            pl.BlockSpec((KV_LORA_RANK, N_HEADS * HEAD_PAD), const),
            pl.BlockSpec((N_HEADS * V_HEAD_DIM, KV_LORA_RANK), const),
            pl.BlockSpec((1, 2 * D_MODEL), const),
        ],
        out_specs=[
            pl.BlockSpec((None, tl, N_HEADS * HEAD_PAD), lambda bi, i: (bi, i, 0)),
            pl.BlockSpec((None, tl, N_HEADS * HEAD_PAD), lambda bi, i: (bi, i, 0)),
            pl.BlockSpec((None, tl // ATTN_KEY_BLOCK, N_HEADS * V_HEAD_DIM, ATTN_KEY_BLOCK),
                         lambda bi, i: (bi, i, 0, 0)),
            pl.BlockSpec((tl, SSM_WIDTH), lambda bi, i: (i, bi)),
            pl.BlockSpec((None, tl, 2 * D_MODEL), lambda bi, i: (bi, i, 0)),
        ],
        out_shape=[
            jax.ShapeDtypeStruct((b, l, N_HEADS * HEAD_PAD), BF16),
            jax.ShapeDtypeStruct((b, l, N_HEADS * HEAD_PAD), BF16),
            jax.ShapeDtypeStruct((b, l // ATTN_KEY_BLOCK, N_HEADS * V_HEAD_DIM, ATTN_KEY_BLOCK), BF16),
            jax.ShapeDtypeStruct((l, b * SSM_WIDTH), BF16),
            jax.ShapeDtypeStruct((b, l, 2 * D_MODEL), BF16),
        ],
        compiler_params=pltpu.CompilerParams(
            dimension_semantics=("parallel", "parallel"), vmem_limit_bytes=VMEM_LIMIT),
        name="in_proj",
    )(x, pos_f, g_pre, w_in_p, q_norm, w_uq_p, kv_norm, w_uk_p, w_uv, b_gate)


def _attn_kernel(q_ref, k_ref, vt_ref, o_ref, s_ref, mx_ref, m_ref, acc_ref, *, tq, tk):
    i = pl.program_id(2)
    key = lax.broadcasted_iota(jnp.int32, (tk, tq), 0)
    qry = lax.broadcasted_iota(jnp.int32, (tk, tq), 1)
    diag_masks = (key <= qry, key + tk <= qry)
    qs = [q_ref[:, hh * HEAD_PAD:(hh + 1) * HEAD_PAD] for hh in range(2)]

    def scores(j, slot, mask=None):
        r0 = pl.multiple_of(j * tk, tk)
        for hh in range(2):
            kj = k_ref[pl.ds(r0, tk), hh * HEAD_PAD:(hh + 1) * HEAD_PAD]
            s = lax.dot_general(kj, qs[hh], (((1,), (1,)), ((), ())), preferred_element_type=F32)
            if mask is not None:
                s = jnp.where(mask, s, NEG_BIG)
            s_ref[slot, hh] = s
            mx_ref[slot, hh] = jnp.max(s, axis=0, keepdims=True)

    vrow = lax.broadcasted_iota(jnp.int32, (2 * V_HEAD_DIM, tk), 0)
    one = jnp.ones((), BF16)

    def softmax_pv(j, slot):
        vt = vt_ref[j]
        for hh in range(2):
            m = m_ref[hh]
            m_new = jnp.maximum(m, mx_ref[slot, hh])
            alpha = jnp.exp2(m - m_new)
            p = jnp.exp2(s_ref[slot, hh] - m_new)
            vt_h = jnp.where(vrow == SUM_ROW[hh], one, vt)
            acc_ref[hh] = alpha * acc_ref[hh] + _dot(vt_h, p.astype(BF16))
            m_ref[hh] = m_new

    m_ref[...] = jnp.full(m_ref.shape, NEG_BIG, F32)
    acc_ref[...] = jnp.zeros(acc_ref.shape, F32)

    @pl.when(i == 0)
    def _():
        scores(0, 0, diag_masks[0])

    @pl.when(i > 0)
    def _():
        scores(0, 0)

    def two_blocks(t, last):
        scores(2 * t + 1, 1)
        softmax_pv(2 * t, 0)
        scores(2 * t + 2, 0, diag_masks[0] if last else None)
        softmax_pv(2 * t + 1, 1)

    def body(t, carry):
        two_blocks(t, False)
        return carry

    lax.fori_loop(0, i - 1, body, 0)

    @pl.when(i > 0)
    def _():
        two_blocks(i - 1, True)

    scores(2 * i + 1, 1, diag_masks[1])
    softmax_pv(2 * i, 0)
    softmax_pv(2 * i + 1, 1)

    a0 = acc_ref[0]
    a1 = acc_ref[1]
    o0 = a0[:V_HEAD_DIM] / a0[SUM_ROW[0]:SUM_ROW[0] + 1]
    o1 = a1[V_HEAD_DIM:] / a1[SUM_ROW[1]:SUM_ROW[1] + 1]
    out_t = jnp.concatenate([o0, o1], axis=0)
    o_ref[...] = out_t.T.astype(BF16)


def _attention(q_all, k_all, v_t):
    b, l, _ = q_all.shape
    tq, tk = ATTN_TILE, ATTN_KEY_BLOCK
    return pl.pallas_call(
        functools.partial(_attn_kernel, tq=tq, tk=tk),
        grid=(b, N_PAIRS, l // tq),
        in_specs=[
            pl.BlockSpec((None, tq, 2 * HEAD_PAD), lambda bi, p, i: (bi, i, p)),
            pl.BlockSpec((None, l, 2 * HEAD_PAD), lambda bi, p, i: (bi, 0, p)),
            pl.BlockSpec((None, l // tk, 2 * V_HEAD_DIM, tk), lambda bi, p, i: (bi, 0, p, 0)),
        ],
        out_specs=pl.BlockSpec((None, tq, 2 * V_HEAD_DIM), lambda bi, p, i: (bi, i, p)),
        out_shape=jax.ShapeDtypeStruct((b, l, N_HEADS * V_HEAD_DIM), BF16),
        scratch_shapes=[
            pltpu.VMEM((2, 2, tk, tq), F32),
            pltpu.VMEM((2, 2, 1, tq), F32),
            pltpu.VMEM((2, 1, tq), F32),
            pltpu.VMEM((2, 2 * V_HEAD_DIM, tq), F32),
        ],
        compiler_params=pltpu.CompilerParams(
            dimension_semantics=("parallel", "parallel", "parallel"), vmem_limit_bytes=VMEM_LIMIT),
        name="attention",
    )(q_all, k_all, v_t)


def _s5_kernel(u_ref, are_ref, aim_ref, wb_ref, wcr_ref, wci_ref, d_ref, wglu_ref, bglu_ref,
               o_ref, state_ref, sre_ref, sim_ref, tb_ref, *, steps, batch):
    @pl.when(pl.program_id(0) == 0)
    def _():
        state_ref[...] = jnp.zeros_like(state_ref)

    lane_groups = SSM_WIDTH // LANES
    for b in range(batch):
        for c in range(lane_groups):
            c0 = b * SSM_WIDTH + c * LANES
            tb_ref[c, pl.ds(b, steps, stride=batch), :] = u_ref[:, c0:c0 + LANES].astype(F32)
    u32 = jnp.concatenate([tb_ref[c] for c in range(lane_groups)], axis=1)
    u = u32.astype(BF16)
    for k in range(SSM_HALVES):
        bu = _dot(u[:, k * SSM_HALF_IN:(k + 1) * SSM_HALF_IN], wb_ref[k])
        sre_ref[k] = bu[:, :SSM_HALF_STATE]
        sim_ref[k] = bu[:, SSM_HALF_STATE:]

    nt = (((1,), (1,)), ((), ()))
    ys = []
    for k in range(SSM_HALVES):
        ar = are_ref[:, k * SSM_HALF_STATE:(k + 1) * SSM_HALF_STATE]
        ai = aim_ref[:, k * SSM_HALF_STATE:(k + 1) * SSM_HALF_STATE]
        sr = state_ref[k, 0]
        si = state_ref[k, 1]
        for t in range(steps):
            rs = slice(t * batch, (t + 1) * batch)
            nr = ar * sr - ai * si + sre_ref[k, rs, :]
            ni = ar * si + ai * sr + sim_ref[k, rs, :]
            sre_ref[k, rs, :] = nr
            sim_ref[k, rs, :] = ni
            sr, si = nr, ni
        state_ref[k, 0] = sr
        state_ref[k, 1] = si
        ys.append(lax.dot_general(sre_ref[k].astype(BF16), wcr_ref[k], nt, preferred_element_type=F32)
                  - lax.dot_general(sim_ref[k].astype(BF16), wci_ref[k], nt, preferred_element_type=F32))

    y = jnp.concatenate(ys, axis=1) + d_ref[...] * u32
    g = jax.nn.gelu(y, approximate=True)
    z = _dot(g.astype(BF16), wglu_ref[...]) + bglu_ref[...]
    out = g * jax.nn.sigmoid(z)
    for c in range(lane_groups):
        tb_ref[c] = out[:, c * LANES:(c + 1) * LANES]
    for b in range(batch):
        for c in range(lane_groups):
            c0 = b * SSM_WIDTH + c * LANES
            o_ref[:, c0:c0 + LANES] = tb_ref[c, pl.ds(b, steps, stride=batch), :].astype(BF16)


def _s5(u_tm, a_re_t, a_im_t, wb, wcr, wci, d_skip, w_glu, b_glu, batch):
    length = u_tm.shape[0]
    width = SSM_WIDTH
    steps = SSM_CHUNK
    rows = steps * batch
    const2 = lambda i: (0, 0)
    const3 = lambda i: (0, 0, 0)
    return pl.pallas_call(
        functools.partial(_s5_kernel, steps=steps, batch=batch),
        grid=(length // steps,),
        in_specs=[
            pl.BlockSpec((steps, batch * width), lambda i: (i, 0)),
            pl.BlockSpec(a_re_t.shape, const2),
            pl.BlockSpec(a_im_t.shape, const2),
            pl.BlockSpec(wb.shape, const3),
            pl.BlockSpec(wcr.shape, const3),
            pl.BlockSpec(wci.shape, const3),
            pl.BlockSpec((1, width), const2),
            pl.BlockSpec((width, width), const2),
            pl.BlockSpec((1, width), const2),
        ],
        out_specs=pl.BlockSpec((steps, batch * width), lambda i: (i, 0)),
        out_shape=jax.ShapeDtypeStruct((length, batch * width), BF16),
        scratch_shapes=[
            pltpu.VMEM((SSM_HALVES, 2, batch, SSM_HALF_STATE), F32),
            pltpu.VMEM((SSM_HALVES, rows, SSM_HALF_STATE), F32),
            pltpu.VMEM((SSM_HALVES, rows, SSM_HALF_STATE), F32),
            pltpu.VMEM((width // LANES, rows, LANES), F32),
        ],
        compiler_params=pltpu.CompilerParams(
            dimension_semantics=("arbitrary",), vmem_limit_bytes=VMEM_LIMIT),
        name="s5",
    )(u_tm, a_re_t, a_im_t, wb, wcr, wci, d_skip, w_glu, b_glu)


def _merge_kernel(x_ref, attn_ref, ssm_ref, gate_ref, wba_ref, wbs_ref, wout_ref, gpost_ref, o_ref):
    ga = gate_ref[:, :D_MODEL].astype(F32)
    gs = gate_ref[:, D_MODEL:].astype(F32)
    merged = ga * _dot(attn_ref[...], wba_ref[...]) + gs * _dot(ssm_ref[...], wbs_ref[...])
    m2 = _dot(merged.astype(BF16), wout_ref[...])
    o_ref[...] = x_ref[...] + _rms(m2, gpost_ref[...])


def _merge(x, attn, ssm_tm, gates, w_ba, w_bs, w_out, g_post):
    b, l, d = x.shape
    tl = ROW_TILE
    const = lambda *_: (0, 0)
    return pl.pallas_call(
        _merge_kernel,
        grid=(b, l // tl),
        in_specs=[
            pl.BlockSpec((None, tl, d), lambda bi, i: (bi, i, 0)),
            pl.BlockSpec((None, tl, N_HEADS * V_HEAD_DIM), lambda bi, i: (bi, i, 0)),
            pl.BlockSpec((tl, SSM_WIDTH), lambda bi, i: (i, bi)),
            pl.BlockSpec((None, tl, 2 * d), lambda bi, i: (bi, i, 0)),
            pl.BlockSpec(w_ba.shape, const),
            pl.BlockSpec(w_bs.shape, const),
            pl.BlockSpec(w_out.shape, const),
            pl.BlockSpec((1, d), const),
        ],
        out_specs=pl.BlockSpec((None, tl, d), lambda bi, i: (bi, i, 0)),
        out_shape=jax.ShapeDtypeStruct((b, l, d), F32),
        compiler_params=pltpu.CompilerParams(
            dimension_semantics=("parallel", "parallel"), vmem_limit_bytes=VMEM_LIMIT),
        name="merge",
    )(x, attn, ssm_tm, gates, w_ba, w_bs, w_out, g_post)


def _ffn_kernel(x_ref, halo_ref, gpre_ref, wup_ref, cw_ref, cb_ref, wdown_ref, gpost_ref, o_ref,
                act_ref, *, tl):
    i = pl.program_id(1)
    x = x_ref[...]
    g = gpre_ref[...]
    hn_t = _rms(x, g).astype(BF16)
    keep = (i > 0).astype(F32)
    hn_h = (_rms(halo_ref[...], g) * keep).astype(BF16)
    hn = jnp.concatenate([hn_h, hn_t], axis=0)

    def conv(h, c0):
        cols = slice(c0, c0 + FFN_COL_CHUNK)
        out = cb_ref[:, cols] + cw_ref[2:3, cols] * h[HALO:]
        out = out + cw_ref[1:2, cols] * pltpu.roll(h, 1, 0)[HALO:]
        return out + cw_ref[0:1, cols] * pltpu.roll(h, 2, 0)[HALO:]

    for j in range(D_FF // FFN_COL_CHUNK):
        c0 = j * FFN_COL_CHUNK
        hg = _dot(hn, wup_ref[:, c0:c0 + FFN_COL_CHUNK])
        hv = _dot(hn, wup_ref[:, D_FF + c0:D_FF + c0 + FFN_COL_CHUNK])
        act = jax.nn.gelu(conv(hg, c0), approximate=True) * conv(hv, D_FF + c0)
        act_ref[:, c0:c0 + FFN_COL_CHUNK] = act.astype(BF16)

    ff = _dot(act_ref[...], wdown_ref[...])
    o_ref[...] = x + _rms(ff, gpost_ref[...])


def _conv_ffn(x1, g_pre, w_up, conv_w, conv_b, w_down, g_post):
    b, l, d = x1.shape
    tl = ROW_TILE
    const = lambda *_: (0, 0)
    halo_blocks = tl // HALO
    return pl.pallas_call(
        functools.partial(_ffn_kernel, tl=tl),
        grid=(b, l // tl),
        in_specs=[
            pl.BlockSpec((None, tl, d), lambda bi, i: (bi, i, 0)),
            pl.BlockSpec((None, HALO, d), lambda bi, i: (bi, jnp.maximum(i * halo_blocks - 1, 0), 0)),
            pl.BlockSpec((1, d), const),
            pl.BlockSpec(w_up.shape, const, pipeline_mode=pl.Buffered(1)),
            pl.BlockSpec(conv_w.shape, const),
            pl.BlockSpec(conv_b.shape, const),
            pl.BlockSpec(w_down.shape, const, pipeline_mode=pl.Buffered(1)),
            pl.BlockSpec((1, d), const),
        ],
        out_specs=pl.BlockSpec((None, tl, d), lambda bi, i: (bi, i, 0)),
        out_shape=jax.ShapeDtypeStruct((b, l, d), F32),
        scratch_shapes=[pltpu.VMEM((tl, D_FF), BF16)],
        compiler_params=pltpu.CompilerParams(
            dimension_semantics=("parallel", "parallel"), vmem_limit_bytes=VMEM_LIMIT),
        name="conv_ffn",
    )(x1, x1, g_pre, w_up, conv_w, conv_b, w_down, g_post)


def _head_lane_groups(nope, rope):
    k = (nope if nope is not None else rope).shape[0]
    nope = jnp.zeros((k, N_HEADS, QK_NOPE_DIM), BF16) if nope is None else nope
    rope = jnp.zeros((k, N_HEADS, QK_ROPE_DIM), BF16) if rope is None else rope
    return jnp.concatenate([nope, rope, rope], axis=2).astype(BF16).reshape(k, -1)


def _layer(x, pos_f, p):
    b, l, d = x.shape
    row = lambda v: v.reshape(1, -1)

    w_in = p["w_in"]
    off_u = P_KR + QK_ROPE_DIM
    w_kr = w_in[:, P_KR:off_u]
    kr_cols = jnp.concatenate([jnp.zeros((d, QK_NOPE_DIM), w_in.dtype), w_kr, w_kr], axis=1)
    w_in_p = jnp.concatenate([w_in[:, :P_KR], kr_cols, w_in[:, off_u:]], axis=1).astype(BF16)
    w_uq = p["w_uq"].reshape(Q_LORA_RANK, N_HEADS, QK_HEAD_DIM)
    w_uq_p = _head_lane_groups(w_uq[:, :, :QK_NOPE_DIM], w_uq[:, :, QK_NOPE_DIM:])
    w_uk_p = _head_lane_groups(p["w_uk"].reshape(KV_LORA_RANK, N_HEADS, QK_NOPE_DIM), None)

    q_all, k_all, v_t, u_tm, gates = _in_proj(
        x, pos_f, row(p["mix_norm_pre"]), w_in_p, row(p["q_norm"]), w_uq_p, row(p["kv_norm"]),
        w_uk_p, p["w_uv"].T.astype(BF16), row(p["b_gate"]))

    attn = _attention(q_all, k_all, v_t)

    a_re_t, a_im_t, wb, wcr, wci = _s5_prep(
        p["ssm_lambda_re"], p["ssm_lambda_im"], p["ssm_log_dt"], p["ssm_b_re"], p["ssm_b_im"],
        p["ssm_c_re"], p["ssm_c_im"], b)
    ssm_tm = _s5(u_tm, a_re_t, a_im_t, wb, wcr, wci,
                 row(p["ssm_d"]), p["w_glu"].astype(BF16), row(p["b_glu"]), b)

    x1 = _merge(x, attn, ssm_tm, gates, p["w_branch_attn"].astype(BF16), p["w_branch_ssm"].astype(BF16),
                p["w_out"].astype(BF16), row(p["mix_norm_post"]))
    return _conv_ffn(x1, row(p["ffn_norm_pre"]), p["w_up"].astype(BF16), p["conv_w"], row(p["conv_b"]),
                     p["w_down"].astype(BF16), row(p["ffn_norm_post"]))


def kernel(x, positions, mix_norm_pre, w_in, q_norm, w_uq, kv_norm, w_uk, w_uv, ssm_lambda_re, ssm_lambda_im, ssm_log_dt, ssm_b_re, ssm_b_im, ssm_c_re, ssm_c_im, ssm_d, w_glu, b_glu, w_branch_attn, w_branch_ssm, b_gate, w_out, mix_norm_post, ffn_norm_pre, w_up, conv_w, conv_b, w_down, ffn_norm_post):
    b, l, _ = x.shape
    params = dict(mix_norm_pre=mix_norm_pre, w_in=w_in, q_norm=q_norm, w_uq=w_uq, kv_norm=kv_norm,
                  w_uk=w_uk, w_uv=w_uv, ssm_lambda_re=ssm_lambda_re, ssm_lambda_im=ssm_lambda_im,
                  ssm_log_dt=ssm_log_dt, ssm_b_re=ssm_b_re, ssm_b_im=ssm_b_im, ssm_c_re=ssm_c_re,
                  ssm_c_im=ssm_c_im, ssm_d=ssm_d, w_glu=w_glu, b_glu=b_glu, w_branch_attn=w_branch_attn,
                  w_branch_ssm=w_branch_ssm, b_gate=b_gate, w_out=w_out, mix_norm_post=mix_norm_post,
                  ffn_norm_pre=ffn_norm_pre, w_up=w_up, conv_w=conv_w, conv_b=conv_b, w_down=w_down,
                  ffn_norm_post=ffn_norm_post)
    pos_f = positions.astype(F32).reshape(b, 1, l)
    for layer in range(mix_norm_pre.shape[0]):
        x = _layer(x, pos_f, {k: v[layer] for k, v in params.items()})
    return x
```

```python
import functools
import math

import jax
import jax.numpy as jnp
from jax import lax
from jax.experimental import pallas as pl
from jax.experimental.pallas import tpu as pltpu

D_MODEL = 1024
N_HEADS = 8
QK_NOPE_DIM = 64
QK_ROPE_DIM = 32
QK_HEAD_DIM = QK_NOPE_DIM + QK_ROPE_DIM
V_HEAD_DIM = 64
Q_LORA_RANK = 384
KV_LORA_RANK = 256
ROPE_THETA = 10000.0
SSM_WIDTH = 512
SSM_GROUP = 16
SSM_GROUPS = SSM_WIDTH // SSM_GROUP
SSM_STATE = 64
D_FF = 2816
CONV_WIDTH = 3
EPS = 1e-6

LANES = 128
HEAD_PAD = LANES
ROPE_LANE0 = QK_NOPE_DIM
ROPE_HALF = QK_ROPE_DIM // 2
N_PAIRS = N_HEADS // 2
SUM_ROW = (V_HEAD_DIM, 0)
SSM_HALVES = 2
SSM_HALF_IN = SSM_WIDTH // SSM_HALVES
SSM_HALF_STATE = SSM_GROUPS * SSM_STATE // SSM_HALVES

P_CQ = 0
P_CKV = P_CQ + Q_LORA_RANK
P_KR = P_CKV + KV_LORA_RANK
P_U = P_KR + LANES
P_GATE = P_U + SSM_WIDTH
P_END = P_GATE + 2 * D_MODEL

Q_SCALE = (1.0 / math.sqrt(QK_HEAD_DIM)) * math.log2(math.e)
NEG_BIG = -1e30

ROW_TILE = 512
ATTN_TILE = ROW_TILE
ATTN_KEY_BLOCK = ATTN_TILE // 2
SSM_CHUNK = 64
FFN_COL_CHUNK = 256
FFN_SUBTILES = 2
FFN_ROW_TILE = FFN_SUBTILES * ROW_TILE
HALO = 16
VMEM_LIMIT = 56 * 1024 * 1024

BF16 = jnp.bfloat16
F32 = jnp.float32


def _rms(x, g):
    return x * lax.rsqrt(jnp.mean(x * x, axis=-1, keepdims=True) + EPS) * g


def _dot(a, b):
    return jnp.dot(a, b, preferred_element_type=F32)


GELU_C = math.sqrt(2.0 / math.pi)


def _gelu_tanh(x):
    k = -2.0 * GELU_C * math.log2(math.e)
    e = jnp.exp2(x * (x * x * (k * 0.044715) + k))
    return x / (1.0 + e)


def _s5_prep_kernel(lr_ref, li_ref, logdt_ref, br_ref, bi_ref, cr_ref, ci_ref,
                    are_ref, aim_ref, wb_ref, wcr_ref, wci_ref, *, batch):
    lr = lr_ref[...]
    li = li_ref[...]
    dt = jnp.exp(logdt_ref[...])
    mag = jnp.exp(lr * dt)
    ang = li * dt
    a_re = mag * jnp.cos(ang)
    a_im = mag * jnp.sin(ang)
    den = lr * lr + li * li
    n_re = a_re - 1.0
    n_im = a_im
    z_re = (n_re * lr + n_im * li) / den
    z_im = (n_im * lr - n_re * li) / den
    are_ref[...] = jnp.broadcast_to(a_re, (batch, a_re.shape[1]))
    aim_ref[...] = jnp.broadcast_to(a_im, (batch, a_im.shape[1]))
    br = br_ref[...]
    bi = bi_ref[...]
    bb_re = z_re * br - z_im * bi
    bb_im = z_re * bi + z_im * br

    groups_per_half = SSM_GROUPS // SSM_HALVES
    row_group = jnp.right_shift(lax.broadcasted_iota(jnp.int32, (SSM_HALF_IN, SSM_HALF_STATE), 0),
                                SSM_GROUP.bit_length() - 1)
    col_group = jnp.right_shift(lax.broadcasted_iota(jnp.int32, (SSM_HALF_IN, SSM_HALF_STATE), 1),
                                SSM_STATE.bit_length() - 1)
    on_diag = row_group == col_group

    def diag(v):
        return jnp.where(on_diag, jnp.concatenate([v] * groups_per_half, axis=0), 0.0)

    for k in range(SSM_HALVES):
        cols = slice(k * SSM_HALF_STATE, (k + 1) * SSM_HALF_STATE)
        wb_ref[k, :, :SSM_HALF_STATE] = diag(bb_re[:, cols]).astype(BF16)
        wb_ref[k, :, SSM_HALF_STATE:] = diag(bb_im[:, cols]).astype(BF16)
        wcr_ref[k] = diag(cr_ref[:, cols]).astype(BF16)
        wci_ref[k] = diag(ci_ref[:, cols]).astype(BF16)


def _s5_prep(lam_re, lam_im, log_dt, b_re, b_im, c_re, c_im, batch):
    n = SSM_GROUPS * SSM_STATE
    lanes = lambda v: v.reshape(1, n)
    chan_major = lambda v, perm: v.transpose(perm).reshape(SSM_GROUP, n)
    return pl.pallas_call(
        functools.partial(_s5_prep_kernel, batch=batch),
        out_shape=(jax.ShapeDtypeStruct((batch, n), F32), jax.ShapeDtypeStruct((batch, n), F32),
                   jax.ShapeDtypeStruct((SSM_HALVES, SSM_HALF_IN, 2 * SSM_HALF_STATE), BF16),
                   jax.ShapeDtypeStruct((SSM_HALVES, SSM_HALF_IN, SSM_HALF_STATE), BF16),
                   jax.ShapeDtypeStruct((SSM_HALVES, SSM_HALF_IN, SSM_HALF_STATE), BF16)),
        name="s5_prep",
    )(lanes(lam_re), lanes(lam_im), lanes(jnp.repeat(log_dt, SSM_STATE)),
      chan_major(b_re, (2, 0, 1)), chan_major(b_im, (2, 0, 1)),
      chan_major(c_re, (1, 0, 2)), chan_major(c_im, (1, 0, 2)))


def _in_proj_kernel(x_ref, pos_ref, gpre_ref, win_ref, qn_ref, wuq_ref, kvn_ref, wuk_ref,
                    wuv_ref, bg_ref, q_ref, k_ref, vt_ref, u_ref, gate_ref):
    x = x_ref[...]
    tl = x.shape[0]
    hn = _rms(x, gpre_ref[...]).astype(BF16)

    freq_idx = lax.broadcasted_iota(jnp.int32, (ROPE_HALF, 1), 0).astype(F32)
    inv_freq = jnp.exp(freq_idx * (-2.0 * math.log(ROPE_THETA) / QK_ROPE_DIM))
    ang = inv_freq * pos_ref[...]
    c16 = jnp.cos(ang)
    s16 = jnp.sin(ang)
    ones = jnp.ones((ROPE_LANE0, tl), F32)
    zeros = lambda n: jnp.zeros((n, tl), F32)
    cos = jnp.concatenate([ones, zeros(ROPE_HALF), c16, c16, zeros(ROPE_HALF)], axis=0).T
    sin = jnp.concatenate([zeros(ROPE_LANE0 + ROPE_HALF), s16, -s16, zeros(ROPE_HALF)], axis=0).T
    cos_q = cos * Q_SCALE
    sin_q = sin * Q_SCALE

    low = _dot(hn, win_ref[:, P_CQ:P_U])
    cq = low[:, P_CQ:P_CKV]
    ckv = low[:, P_CKV:P_KR]
    kr = low[:, P_KR:P_U]
    cqn = _rms(cq, qn_ref[...]).astype(BF16)
    q = _dot(cqn, wuq_ref[...])
    for h in range(N_HEADS):
        sl = slice(h * HEAD_PAD, (h + 1) * HEAD_PAD)
        t = q[:, sl]
        q_ref[:, sl] = (t * cos_q + pltpu.roll(t, ROPE_HALF, 1) * sin_q).astype(BF16)

    ckvn = _rms(ckv, kvn_ref[...]).astype(BF16)
    v_t = lax.dot_general(wuv_ref[...], ckvn, (((1,), (1,)), ((), ())),
                          preferred_element_type=F32).astype(BF16)
    for c in range(vt_ref.shape[0]):
        vt_ref[c] = v_t[:, c * ATTN_KEY_BLOCK:(c + 1) * ATTN_KEY_BLOCK]
    kn = _dot(ckvn, wuk_ref[...])
    kr = kr * cos + pltpu.roll(kr, ROPE_HALF, 1) * sin
    for h in range(N_HEADS):
        sl = slice(h * HEAD_PAD, (h + 1) * HEAD_PAD)
        k_ref[:, sl] = (kn[:, sl] + kr).astype(BF16)

    u_ref[...] = _dot(hn, win_ref[:, P_U:P_GATE]).astype(BF16)
    logits = _dot(hn, win_ref[:, P_GATE:P_END]) + bg_ref[...]
    gate_ref[...] = jax.nn.sigmoid(logits).astype(BF16)


def _in_proj(x, pos_f, g_pre, w_in_p, q_norm, w_uq_p, kv_norm, w_uk_p, w_uv, b_gate):
    b, l, d = x.shape
    tl = ROW_TILE
    const = lambda *_: (0, 0)
    return pl.pallas_call(
        _in_proj_kernel,
        grid=(b, l // tl),
        in_specs=[
            pl.BlockSpec((None, tl, d), lambda bi, i: (bi, i, 0)),
            pl.BlockSpec((None, 1, tl), lambda bi, i: (bi, 0, i)),
            pl.BlockSpec((1, d), const),
            pl.BlockSpec((d, P_END), const),
            pl.BlockSpec((1, Q_LORA_RANK), const),
            pl.BlockSpec((Q_LORA_RANK, N_HEADS * HEAD_PAD), const),
            pl.BlockSpec((1, KV_LORA_RANK), const),
            pl.BlockSpec((KV_LORA_RANK, N_HEADS * HEAD_PAD), const),
            pl.BlockSpec((N_HEADS * V_HEAD_DIM, KV_LORA_RANK), const),
            pl.BlockSpec((1, 2 * D_MODEL), const),
        ],
        out_specs=[
            pl.BlockSpec((None, tl, N_HEADS * HEAD_PAD), lambda bi, i: (bi, i, 0)),
            pl.BlockSpec((None, tl, N_HEADS * HEAD_PAD), lambda bi, i: (bi, i, 0)),
            pl.BlockSpec((None, tl // ATTN_KEY_BLOCK, N_HEADS * V_HEAD_DIM, ATTN_KEY_BLOCK),
                         lambda bi, i: (bi, i, 0, 0)),
            pl.BlockSpec((tl, SSM_WIDTH), lambda bi, i: (i, bi)),
            pl.BlockSpec((None, tl, 2 * D_MODEL), lambda bi, i: (bi, i, 0)),
        ],
        out_shape=[
            jax.ShapeDtypeStruct((b, l, N_HEADS * HEAD_PAD), BF16),
            jax.ShapeDtypeStruct((b, l, N_HEADS * HEAD_PAD), BF16),
            jax.ShapeDtypeStruct((b, l // ATTN_KEY_BLOCK, N_HEADS * V_HEAD_DIM, ATTN_KEY_BLOCK), BF16),
            jax.ShapeDtypeStruct((l, b * SSM_WIDTH), BF16),
            jax.ShapeDtypeStruct((b, l, 2 * D_MODEL), BF16),
        ],
        compiler_params=pltpu.CompilerParams(
            dimension_semantics=("parallel", "parallel"), vmem_limit_bytes=VMEM_LIMIT),
        name="in_proj",
    )(x, pos_f, g_pre, w_in_p, q_norm, w_uq_p, kv_norm, w_uk_p, w_uv, b_gate)


def _attn_kernel(q_ref, k_ref, vt_ref, o_ref, s_ref, mx_ref, m_ref, acc_ref, *, tq, tk):
    i = pl.program_id(2)
    n_q = pl.num_programs(2)
    key = lax.broadcasted_iota(jnp.int32, (tk, tq), 0)
    qry = lax.broadcasted_iota(jnp.int32, (tk, tq), 1)
    diag_masks = (key <= qry, key + tk <= qry)

    def scores(j, slot, mask=None, tile=None):
        r0 = pl.multiple_of(j * tk, tk)
        q0 = pl.multiple_of((i if tile is None else tile) * tq, tq)
        for hh in range(2):
            lanes = slice(hh * HEAD_PAD, (hh + 1) * HEAD_PAD)
            kj = k_ref[pl.ds(r0, tk), lanes]
            qt = q_ref[pl.ds(q0, tq), lanes]
            s = lax.dot_general(kj, qt, (((1,), (1,)), ((), ())), preferred_element_type=F32)
            if mask is not None:
                s = jnp.where(mask, s, NEG_BIG)
            s_ref[slot, hh] = s
            mx_ref[slot, hh] = jnp.max(s, axis=0, keepdims=True)

    vrow = lax.broadcasted_iota(jnp.int32, (2 * V_HEAD_DIM, tk), 0)
    one = jnp.ones((), BF16)

    def softmax_pv(j, slot):
        vt = vt_ref[j]
        for hh in range(2):
            m = m_ref[hh]
            m_new = jnp.maximum(m, mx_ref[slot, hh])
            alpha = jnp.exp2(m - m_new)
            p = jnp.exp2(s_ref[slot, hh] - m_new)
            vt_h = jnp.where(vrow == SUM_ROW[hh], one, vt)
            acc_ref[hh] = alpha * acc_ref[hh] + _dot(vt_h, p.astype(BF16))
            m_ref[hh] = m_new

    m_ref[...] = jnp.full(m_ref.shape, NEG_BIG, F32)
    acc_ref[...] = jnp.zeros(acc_ref.shape, F32)

    def two_blocks(t, last=False):
        scores(2 * t + 1, 1)
        softmax_pv(2 * t, 0)
        scores(2 * t + 2, 0, diag_masks[0] if last else None)
        softmax_pv(2 * t + 1, 1)

    def four_blocks(t, carry):
        two_blocks(2 * t)
        two_blocks(2 * t + 1)
        return carry

    n_pairs = jnp.maximum(i - 1, 0)
    lax.fori_loop(0, n_pairs // 2, four_blocks, 0)

    @pl.when(n_pairs % 2 == 1)
    def _():
        two_blocks(n_pairs - 1)

    def diagonal_tile():
        scores(2 * i + 1, 1, diag_masks[1])
        softmax_pv(2 * i, 0)
        scores(0, 0, tile=jnp.minimum(i + 1, n_q - 1))
        softmax_pv(2 * i + 1, 1)

    @pl.when(i > 0)
    def _():
        two_blocks(i - 1, last=True)
        diagonal_tile()

    @pl.when(i == 0)
    def _():
        scores(0, 0, diag_masks[0])
        diagonal_tile()

    a0 = acc_ref[0]
    a1 = acc_ref[1]
    o0 = a0[:V_HEAD_DIM] / a0[SUM_ROW[0]:SUM_ROW[0] + 1]
    o1 = a1[V_HEAD_DIM:] / a1[SUM_ROW[1]:SUM_ROW[1] + 1]
    out_t = jnp.concatenate([o0, o1], axis=0)
    o_ref[...] = out_t.T.astype(BF16)


def _attention(q_all, k_all, v_t):
    b, l, _ = q_all.shape
    tq, tk = ATTN_TILE, ATTN_KEY_BLOCK
    return pl.pallas_call(
        functools.partial(_attn_kernel, tq=tq, tk=tk),
        grid=(b, N_PAIRS, l // tq),
        in_specs=[
            pl.BlockSpec((None, l, 2 * HEAD_PAD), lambda bi, p, i: (bi, 0, p)),
            pl.BlockSpec((None, l, 2 * HEAD_PAD), lambda bi, p, i: (bi, 0, p)),
            pl.BlockSpec((None, l // tk, 2 * V_HEAD_DIM, tk), lambda bi, p, i: (bi, 0, p, 0)),
        ],
        out_specs=pl.BlockSpec((None, tq, 2 * V_HEAD_DIM), lambda bi, p, i: (bi, i, p)),
        out_shape=jax.ShapeDtypeStruct((b, l, N_HEADS * V_HEAD_DIM), BF16),
        scratch_shapes=[
            pltpu.VMEM((2, 2, tk, tq), F32),
            pltpu.VMEM((2, 2, 1, tq), F32),
            pltpu.VMEM((2, 1, tq), F32),
            pltpu.VMEM((2, 2 * V_HEAD_DIM, tq), F32),
        ],
        compiler_params=pltpu.CompilerParams(
            dimension_semantics=("parallel", "parallel", "arbitrary"), vmem_limit_bytes=VMEM_LIMIT),
        name="attention",
    )(q_all, k_all, v_t)


def _s5_kernel(u_ref, are_ref, aim_ref, wb_ref, wcr_ref, wci_ref, d_ref, wglu_ref, bglu_ref,
               o_ref, state_ref, sre_ref, sim_ref, tb_ref, *, steps, batch):
    @pl.when(pl.program_id(0) == 0)
    def _():
        state_ref[...] = jnp.zeros_like(state_ref)

    lane_groups = SSM_WIDTH // LANES
    for b in range(batch):
        for c in range(lane_groups):
            c0 = b * SSM_WIDTH + c * LANES
            tb_ref[c, pl.ds(b, steps, stride=batch), :] = u_ref[:, c0:c0 + LANES].astype(F32)
    u32 = jnp.concatenate([tb_ref[c] for c in range(lane_groups)], axis=1)
    u = u32.astype(BF16)
    for k in range(SSM_HALVES):
        bu = _dot(u[:, k * SSM_HALF_IN:(k + 1) * SSM_HALF_IN], wb_ref[k])
        sre_ref[k] = bu[:, :SSM_HALF_STATE]
        sim_ref[k] = bu[:, SSM_HALF_STATE:]

    nt = (((1,), (1,)), ((), ()))
    ys = []
    for k in range(SSM_HALVES):
        ar = are_ref[:, k * SSM_HALF_STATE:(k + 1) * SSM_HALF_STATE]
        ai = aim_ref[:, k * SSM_HALF_STATE:(k + 1) * SSM_HALF_STATE]
        sr = state_ref[k, 0]
        si = state_ref[k, 1]
        for t in range(steps):
            rs = slice(t * batch, (t + 1) * batch)
            nr = ar * sr - ai * si + sre_ref[k, rs, :]
            ni = ar * si + ai * sr + sim_ref[k, rs, :]
            sre_ref[k, rs, :] = nr
            sim_ref[k, rs, :] = ni
            sr, si = nr, ni
        state_ref[k, 0] = sr
        state_ref[k, 1] = si
        ys.append(lax.dot_general(sre_ref[k].astype(BF16), wcr_ref[k], nt, preferred_element_type=F32)
                  - lax.dot_general(sim_ref[k].astype(BF16), wci_ref[k], nt, preferred_element_type=F32))

    y = jnp.concatenate(ys, axis=1) + d_ref[...] * u32
    g = _gelu_tanh(y)
    z = _dot(g.astype(BF16), wglu_ref[...]) + bglu_ref[...]
    out = g * jax.nn.sigmoid(z)
    for c in range(lane_groups):
        tb_ref[c] = out[:, c * LANES:(c + 1) * LANES]
    for b in range(batch):
        for c in range(lane_groups):
            c0 = b * SSM_WIDTH + c * LANES
            o_ref[:, c0:c0 + LANES] = tb_ref[c, pl.ds(b, steps, stride=batch), :].astype(BF16)


def _s5(u_tm, a_re_t, a_im_t, wb, wcr, wci, d_skip, w_glu, b_glu, batch):
    length = u_tm.shape[0]
    width = SSM_WIDTH
    steps = SSM_CHUNK
    rows = steps * batch
    const2 = lambda i: (0, 0)
    const3 = lambda i: (0, 0, 0)
    return pl.pallas_call(
        functools.partial(_s5_kernel, steps=steps, batch=batch),
        grid=(length // steps,),
        in_specs=[
            pl.BlockSpec((steps, batch * width), lambda i: (i, 0)),
            pl.BlockSpec(a_re_t.shape, const2),
            pl.BlockSpec(a_im_t.shape, const2),
            pl.BlockSpec(wb.shape, const3),
            pl.BlockSpec(wcr.shape, const3),
            pl.BlockSpec(wci.shape, const3),
            pl.BlockSpec((1, width), const2),
            pl.BlockSpec((width, width), const2),
            pl.BlockSpec((1, width), const2),
        ],
        out_specs=pl.BlockSpec((steps, batch * width), lambda i: (i, 0)),
        out_shape=jax.ShapeDtypeStruct((length, batch * width), BF16),
        scratch_shapes=[
            pltpu.VMEM((SSM_HALVES, 2, batch, SSM_HALF_STATE), F32),
            pltpu.VMEM((SSM_HALVES, rows, SSM_HALF_STATE), F32),
            pltpu.VMEM((SSM_HALVES, rows, SSM_HALF_STATE), F32),
            pltpu.VMEM((width // LANES, rows, LANES), F32),
        ],
        compiler_params=pltpu.CompilerParams(
            dimension_semantics=("arbitrary",), vmem_limit_bytes=VMEM_LIMIT),
        name="s5",
    )(u_tm, a_re_t, a_im_t, wb, wcr, wci, d_skip, w_glu, b_glu)


def _merge_kernel(x_ref, attn_ref, ssm_ref, gate_ref, wba_ref, wbs_ref, wout_ref, gpost_ref, o_ref):
    ga = gate_ref[:, :D_MODEL].astype(F32)
    gs = gate_ref[:, D_MODEL:].astype(F32)
    merged = ga * _dot(attn_ref[...], wba_ref[...]) + gs * _dot(ssm_ref[...], wbs_ref[...])
    m2 = _dot(merged.astype(BF16), wout_ref[...])
    o_ref[...] = x_ref[...] + _rms(m2, gpost_ref[...])


def _merge(x, attn, ssm_tm, gates, w_ba, w_bs, w_out, g_post):
    b, l, d = x.shape
    tl = ROW_TILE
    const = lambda *_: (0, 0)
    return pl.pallas_call(
        _merge_kernel,
        grid=(b, l // tl),
        in_specs=[
            pl.BlockSpec((None, tl, d), lambda bi, i: (bi, i, 0)),
            pl.BlockSpec((None, tl, N_HEADS * V_HEAD_DIM), lambda bi, i: (bi, i, 0)),
            pl.BlockSpec((tl, SSM_WIDTH), lambda bi, i: (i, bi)),
            pl.BlockSpec((None, tl, 2 * d), lambda bi, i: (bi, i, 0)),
            pl.BlockSpec(w_ba.shape, const),
            pl.BlockSpec(w_bs.shape, const),
            pl.BlockSpec(w_out.shape, const),
            pl.BlockSpec((1, d), const),
        ],
        out_specs=pl.BlockSpec((None, tl, d), lambda bi, i: (bi, i, 0)),
        out_shape=jax.ShapeDtypeStruct((b, l, d), F32),
        compiler_params=pltpu.CompilerParams(
            dimension_semantics=("parallel", "parallel"), vmem_limit_bytes=VMEM_LIMIT),
        name="merge",
    )(x, attn, ssm_tm, gates, w_ba, w_bs, w_out, g_post)


def _ffn_kernel(x_ref, halo_ref, gpre_ref, wup_ref, cw_ref, cb_ref, wdown_ref, gpost_ref, o_ref,
                act_ref, *, tl):
    i = pl.program_id(1)
    g = gpre_ref[...]
    sub = tl // FFN_SUBTILES

    def conv(h, c0):
        cols = slice(c0, c0 + FFN_COL_CHUNK)
        out = cb_ref[:, cols] + cw_ref[2:3, cols] * h[HALO:]
        out = out + cw_ref[1:2, cols] * pltpu.roll(h, 1, 0)[HALO:]
        return out + cw_ref[0:1, cols] * pltpu.roll(h, 2, 0)[HALO:]

    for st in range(FFN_SUBTILES):
        x = x_ref[st * sub:(st + 1) * sub, :]
        hn_t = _rms(x, g).astype(BF16)
        if st == 0:
            keep = (i > 0).astype(F32)
            hn_h = (_rms(halo_ref[...], g) * keep).astype(BF16)
        else:
            hn_h = _rms(x_ref[st * sub - HALO:st * sub, :], g).astype(BF16)
        hn = jnp.concatenate([hn_h, hn_t], axis=0)

        for j in range(D_FF // FFN_COL_CHUNK):
            c0 = j * FFN_COL_CHUNK
            hg = _dot(hn, wup_ref[:, c0:c0 + FFN_COL_CHUNK])
            hv = _dot(hn, wup_ref[:, D_FF + c0:D_FF + c0 + FFN_COL_CHUNK])
            act = _gelu_tanh(conv(hg, c0)) * conv(hv, D_FF + c0)
            act_ref[st, :, c0:c0 + FFN_COL_CHUNK] = act.astype(BF16)

        ff = _dot(act_ref[st], wdown_ref[...])
        o_ref[st * sub:(st + 1) * sub, :] = x + _rms(ff, gpost_ref[...])


def _conv_ffn(x1, g_pre, w_up, conv_w, conv_b, w_down, g_post):
    b, l, d = x1.shape
    tl = FFN_ROW_TILE
    const = lambda *_: (0, 0)
    halo_blocks = tl // HALO
    return pl.pallas_call(
        functools.partial(_ffn_kernel, tl=tl),
        grid=(b, l // tl),
        in_specs=[
            pl.BlockSpec((None, tl, d), lambda bi, i: (bi, i, 0)),
            pl.BlockSpec((None, HALO, d), lambda bi, i: (bi, jnp.maximum(i * halo_blocks - 1, 0), 0)),
            pl.BlockSpec((1, d), const),
            pl.BlockSpec(w_up.shape, const, pipeline_mode=pl.Buffered(1)),
            pl.BlockSpec(conv_w.shape, const),
            pl.BlockSpec(conv_b.shape, const),
            pl.BlockSpec(w_down.shape, const, pipeline_mode=pl.Buffered(1)),
            pl.BlockSpec((1, d), const),
        ],
        out_specs=pl.BlockSpec((None, tl, d), lambda bi, i: (bi, i, 0)),
        out_shape=jax.ShapeDtypeStruct((b, l, d), F32),
        scratch_shapes=[pltpu.VMEM((FFN_SUBTILES, tl // FFN_SUBTILES, D_FF), BF16)],
        compiler_params=pltpu.CompilerParams(
            dimension_semantics=("parallel", "parallel"), vmem_limit_bytes=VMEM_LIMIT),
        name="conv_ffn",
    )(x1, x1, g_pre, w_up, conv_w, conv_b, w_down, g_post)


def _head_lane_groups(nope, rope):
    k = (nope if nope is not None else rope).shape[0]
    nope = jnp.zeros((k, N_HEADS, QK_NOPE_DIM), BF16) if nope is None else nope
    rope = jnp.zeros((k, N_HEADS, QK_ROPE_DIM), BF16) if rope is None else rope
    return jnp.concatenate([nope, rope, rope], axis=2).astype(BF16).reshape(k, -1)


def _layer(x, pos_f, p):
    b, l, d = x.shape
    row = lambda v: v.reshape(1, -1)

    w_in = p["w_in"]
    off_u = P_KR + QK_ROPE_DIM
    w_kr = w_in[:, P_KR:off_u]
    kr_cols = jnp.concatenate([jnp.zeros((d, QK_NOPE_DIM), w_in.dtype), w_kr, w_kr], axis=1)
    w_in_p = jnp.concatenate([w_in[:, :P_KR], kr_cols, w_in[:, off_u:]], axis=1).astype(BF16)
    w_uq = p["w_uq"].reshape(Q_LORA_RANK, N_HEADS, QK_HEAD_DIM)
    w_uq_p = _head_lane_groups(w_uq[:, :, :QK_NOPE_DIM], w_uq[:, :, QK_NOPE_DIM:])
    w_uk_p = _head_lane_groups(p["w_uk"].reshape(KV_LORA_RANK, N_HEADS, QK_NOPE_DIM), None)

    q_all, k_all, v_t, u_tm, gates = _in_proj(
        x, pos_f, row(p["mix_norm_pre"]), w_in_p, row(p["q_norm"]), w_uq_p, row(p["kv_norm"]),
        w_uk_p, p["w_uv"].T.astype(BF16), row(p["b_gate"]))

    attn = _attention(q_all, k_all, v_t)

    a_re_t, a_im_t, wb, wcr, wci = _s5_prep(
        p["ssm_lambda_re"], p["ssm_lambda_im"], p["ssm_log_dt"], p["ssm_b_re"], p["ssm_b_im"],
        p["ssm_c_re"], p["ssm_c_im"], b)
    ssm_tm = _s5(u_tm, a_re_t, a_im_t, wb, wcr, wci,
                 row(p["ssm_d"]), p["w_glu"].astype(BF16), row(p["b_glu"]), b)

    x1 = _merge(x, attn, ssm_tm, gates, p["w_branch_attn"].astype(BF16), p["w_branch_ssm"].astype(BF16),
                p["w_out"].astype(BF16), row(p["mix_norm_post"]))
    return _conv_ffn(x1, row(p["ffn_norm_pre"]), p["w_up"].astype(BF16), p["conv_w"], row(p["conv_b"]),
                     p["w_down"].astype(BF16), row(p["ffn_norm_post"]))


def kernel(x, positions, mix_norm_pre, w_in, q_norm, w_uq, kv_norm, w_uk, w_uv, ssm_lambda_re, ssm_lambda_im, ssm_log_dt, ssm_b_re, ssm_b_im, ssm_c_re, ssm_c_im, ssm_d, w_glu, b_glu, w_branch_attn, w_branch_ssm, b_gate, w_out, mix_norm_post, ffn_norm_pre, w_up, conv_w, conv_b, w_down, ffn_norm_post):
    b, l, _ = x.shape
    params = dict(mix_norm_pre=mix_norm_pre, w_in=w_in, q_norm=q_norm, w_uq=w_uq, kv_norm=kv_norm,
                  w_uk=w_uk, w_uv=w_uv, ssm_lambda_re=ssm_lambda_re, ssm_lambda_im=ssm_lambda_im,
                  ssm_log_dt=ssm_log_dt, ssm_b_re=ssm_b_re, ssm_b_im=ssm_b_im, ssm_c_re=ssm_c_re,
                  ssm_c_im=ssm_c_im, ssm_d=ssm_d, w_glu=w_glu, b_glu=b_glu, w_branch_attn=w_branch_attn,
                  w_branch_ssm=w_branch_ssm, b_gate=b_gate, w_out=w_out, mix_norm_post=mix_norm_post,
                  ffn_norm_pre=ffn_norm_pre, w_up=w_up, conv_w=conv_w, conv_b=conv_b, w_down=w_down,
                  ffn_norm_post=ffn_norm_post)
    pos_f = positions.astype(F32).reshape(b, 1, l)
    for layer in range(mix_norm_pre.shape[0]):
        x = _layer(x, pos_f, {k: v[layer] for k, v in params.items()})
    return x
```

```python
import functools
import math

import jax
import jax.numpy as jnp
from jax import lax
from jax.experimental import pallas as pl
from jax.experimental.pallas import tpu as pltpu

D_MODEL = 1024
N_HEADS = 8
QK_NOPE_DIM = 64
QK_ROPE_DIM = 32
QK_HEAD_DIM = QK_NOPE_DIM + QK_ROPE_DIM
V_HEAD_DIM = 64
Q_LORA_RANK = 384
KV_LORA_RANK = 256
ROPE_THETA = 10000.0
SSM_WIDTH = 512
SSM_GROUP = 16
SSM_GROUPS = SSM_WIDTH // SSM_GROUP
SSM_STATE = 64
D_FF = 2816
CONV_WIDTH = 3
EPS = 1e-6

LANES = 128
HEAD_PAD = LANES
ROPE_LANE0 = QK_NOPE_DIM
ROPE_HALF = QK_ROPE_DIM // 2
N_PAIRS = N_HEADS // 2
SUM_ROW = (V_HEAD_DIM, 0)
SSM_HALVES = 2
SSM_HALF_IN = SSM_WIDTH // SSM_HALVES
SSM_HALF_STATE = SSM_GROUPS * SSM_STATE // SSM_HALVES

P_CQ = 0
P_CKV = P_CQ + Q_LORA_RANK
P_KR = P_CKV + KV_LORA_RANK
P_U = P_KR + LANES
P_GATE = P_U + SSM_WIDTH
P_END = P_GATE + 2 * D_MODEL

Q_SCALE = (1.0 / math.sqrt(QK_HEAD_DIM)) * math.log2(math.e)
NEG_BIG = -1e30

ROW_TILE = 512
ATTN_TILE = ROW_TILE
ATTN_KEY_BLOCK = ATTN_TILE // 2
SSM_CHUNK = 64
FFN_COL_CHUNK = 256
FFN_PHASES = 8
HALO = 16
VMEM_LIMIT = 56 * 1024 * 1024

BF16 = jnp.bfloat16
F32 = jnp.float32


def _rms(x, g):
    return x * lax.rsqrt(jnp.mean(x * x, axis=-1, keepdims=True) + EPS) * g


def _dot(a, b):
    return jnp.dot(a, b, preferred_element_type=F32)


GELU_C = math.sqrt(2.0 / math.pi)


def _gelu_tanh(x):
    k = -2.0 * GELU_C * math.log2(math.e)
    e = jnp.exp2(x * (x * x * (k * 0.044715) + k))
    return x / (1.0 + e)


def _s5_prep_kernel(lr_ref, li_ref, logdt_ref, br_ref, bi_ref, cr_ref, ci_ref,
                    are_ref, aim_ref, wb_ref, wcr_ref, wci_ref, *, batch):
    lr = lr_ref[...]
    li = li_ref[...]
    dt = jnp.exp(logdt_ref[...])
    mag = jnp.exp(lr * dt)
    ang = li * dt
    a_re = mag * jnp.cos(ang)
    a_im = mag * jnp.sin(ang)
    den = lr * lr + li * li
    n_re = a_re - 1.0
    n_im = a_im
    z_re = (n_re * lr + n_im * li) / den
    z_im = (n_im * lr - n_re * li) / den
    are_ref[...] = jnp.broadcast_to(a_re, (batch, a_re.shape[1]))
    aim_ref[...] = jnp.broadcast_to(a_im, (batch, a_im.shape[1]))
    br = br_ref[...]
    bi = bi_ref[...]
    bb_re = z_re * br - z_im * bi
    bb_im = z_re * bi + z_im * br

    groups_per_half = SSM_GROUPS // SSM_HALVES
    row_group = jnp.right_shift(lax.broadcasted_iota(jnp.int32, (SSM_HALF_IN, SSM_HALF_STATE), 0),
                                SSM_GROUP.bit_length() - 1)
    col_group = jnp.right_shift(lax.broadcasted_iota(jnp.int32, (SSM_HALF_IN, SSM_HALF_STATE), 1),
                                SSM_STATE.bit_length() - 1)
    on_diag = row_group == col_group

    def diag(v):
        return jnp.where(on_diag, jnp.concatenate([v] * groups_per_half, axis=0), 0.0)

    for k in range(SSM_HALVES):
        cols = slice(k * SSM_HALF_STATE, (k + 1) * SSM_HALF_STATE)
        wb_ref[k, :, :SSM_HALF_STATE] = diag(bb_re[:, cols]).astype(BF16)
        wb_ref[k, :, SSM_HALF_STATE:] = diag(bb_im[:, cols]).astype(BF16)
        wcr_ref[k] = diag(cr_ref[:, cols]).astype(BF16)
        wci_ref[k] = diag(ci_ref[:, cols]).astype(BF16)


def _s5_prep(lam_re, lam_im, log_dt, b_re, b_im, c_re, c_im, batch):
    n = SSM_GROUPS * SSM_STATE
    lanes = lambda v: v.reshape(1, n)
    chan_major = lambda v, perm: v.transpose(perm).reshape(SSM_GROUP, n)
    return pl.pallas_call(
        functools.partial(_s5_prep_kernel, batch=batch),
        out_shape=(jax.ShapeDtypeStruct((batch, n), F32), jax.ShapeDtypeStruct((batch, n), F32),
                   jax.ShapeDtypeStruct((SSM_HALVES, SSM_HALF_IN, 2 * SSM_HALF_STATE), BF16),
                   jax.ShapeDtypeStruct((SSM_HALVES, SSM_HALF_IN, SSM_HALF_STATE), BF16),
                   jax.ShapeDtypeStruct((SSM_HALVES, SSM_HALF_IN, SSM_HALF_STATE), BF16)),
        name="s5_prep",
    )(lanes(lam_re), lanes(lam_im), lanes(jnp.repeat(log_dt, SSM_STATE)),
      chan_major(b_re, (2, 0, 1)), chan_major(b_im, (2, 0, 1)),
      chan_major(c_re, (1, 0, 2)), chan_major(c_im, (1, 0, 2)))


def _in_proj_kernel(x_ref, pos_ref, gpre_ref, win_ref, qn_ref, wuq_ref, kvn_ref, wuk_ref,
                    wuv_ref, bg_ref, qt_ref, k_ref, vt_ref, u_ref, gate_ref):
    x = x_ref[...]
    tl = x.shape[0]
    hn = _rms(x, gpre_ref[...]).astype(BF16)

    freq_idx = lax.broadcasted_iota(jnp.int32, (ROPE_HALF, 1), 0).astype(F32)
    inv_freq = jnp.exp(freq_idx * (-2.0 * math.log(ROPE_THETA) / QK_ROPE_DIM))
    ang = inv_freq * pos_ref[...]
    c16 = jnp.cos(ang)
    s16 = jnp.sin(ang)
    ones = jnp.ones((ROPE_LANE0, tl), F32)
    zeros = lambda n: jnp.zeros((n, tl), F32)
    cos_t = jnp.concatenate([ones, zeros(ROPE_HALF), c16, c16, zeros(ROPE_HALF)], axis=0)
    sin_t = jnp.concatenate([zeros(ROPE_LANE0 + ROPE_HALF), s16, -s16, zeros(ROPE_HALF)], axis=0)

    low = _dot(hn, win_ref[:, P_CQ:P_U])
    cq = low[:, P_CQ:P_CKV]
    ckv = low[:, P_CKV:P_KR]
    kr = low[:, P_KR:P_U]
    cqn = _rms(cq, qn_ref[...]).astype(BF16)

    nt = (((1,), (1,)), ((), ()))
    q_t = lax.dot_general(wuq_ref[...], cqn, nt, preferred_element_type=F32)
    cos_q = cos_t * Q_SCALE
    sin_q = sin_t * Q_SCALE
    for h in range(N_HEADS):
        rows = slice(h * HEAD_PAD, (h + 1) * HEAD_PAD)
        t = q_t[rows, :]
        qt_ref[rows, :] = (t * cos_q + pltpu.roll(t, ROPE_HALF, 0) * sin_q).astype(BF16)

    cos = cos_t.T
    sin = sin_t.T

    ckvn = _rms(ckv, kvn_ref[...]).astype(BF16)
    v_t = lax.dot_general(wuv_ref[...], ckvn, nt, preferred_element_type=F32).astype(BF16)
    for c in range(vt_ref.shape[0]):
        vt_ref[c] = v_t[:, c * ATTN_KEY_BLOCK:(c + 1) * ATTN_KEY_BLOCK]
    kn = _dot(ckvn, wuk_ref[...])
    kr = kr * cos + pltpu.roll(kr, ROPE_HALF, 1) * sin
    for h in range(N_HEADS):
        sl = slice(h * HEAD_PAD, (h + 1) * HEAD_PAD)
        k_ref[:, sl] = (kn[:, sl] + kr).astype(BF16)

    u_ref[...] = _dot(hn, win_ref[:, P_U:P_GATE]).astype(BF16)
    logits = _dot(hn, win_ref[:, P_GATE:P_END]) + bg_ref[...]
    gate_ref[...] = jax.nn.sigmoid(logits).astype(BF16)


def _in_proj(x, pos_f, g_pre, w_in_p, q_norm, w_uq_p, kv_norm, w_uk_p, w_uv, b_gate):
    b, l, d = x.shape
    tl = ROW_TILE
    const = lambda *_: (0, 0)
    return pl.pallas_call(
        _in_proj_kernel,
        grid=(b, l // tl),
        in_specs=[
            pl.BlockSpec((None, tl, d), lambda bi, i: (bi, i, 0)),
            pl.BlockSpec((None, 1, tl), lambda bi, i: (bi, 0, i)),
            pl.BlockSpec((1, d), const),
            pl.BlockSpec((d, P_END), const),
            pl.BlockSpec((1, Q_LORA_RANK), const),
            pl.BlockSpec((N_HEADS * HEAD_PAD, Q_LORA_RANK), const),
            pl.BlockSpec((1, KV_LORA_RANK), const),
            pl.BlockSpec((KV_LORA_RANK, N_HEADS * HEAD_PAD), const),
            pl.BlockSpec((N_HEADS * V_HEAD_DIM, KV_LORA_RANK), const),
            pl.BlockSpec((1, 2 * D_MODEL), const),
        ],
        out_specs=[
            pl.BlockSpec((None, None, N_HEADS * HEAD_PAD, tl), lambda bi, i: (bi, i, 0, 0)),
            pl.BlockSpec((None, tl, N_HEADS * HEAD_PAD), lambda bi, i: (bi, i, 0)),
            pl.BlockSpec((None, tl // ATTN_KEY_BLOCK, N_HEADS * V_HEAD_DIM, ATTN_KEY_BLOCK),
                         lambda bi, i: (bi, i, 0, 0)),
            pl.BlockSpec((tl, SSM_WIDTH), lambda bi, i: (i, bi)),
            pl.BlockSpec((None, tl, 2 * D_MODEL), lambda bi, i: (bi, i, 0)),
        ],
        out_shape=[
            jax.ShapeDtypeStruct((b, l // tl, N_HEADS * HEAD_PAD, tl), BF16),
            jax.ShapeDtypeStruct((b, l, N_HEADS * HEAD_PAD), BF16),
            jax.ShapeDtypeStruct((b, l // ATTN_KEY_BLOCK, N_HEADS * V_HEAD_DIM, ATTN_KEY_BLOCK), BF16),
            jax.ShapeDtypeStruct((l, b * SSM_WIDTH), BF16),
            jax.ShapeDtypeStruct((b, l, 2 * D_MODEL), BF16),
        ],
        compiler_params=pltpu.CompilerParams(
            dimension_semantics=("parallel", "parallel"), vmem_limit_bytes=VMEM_LIMIT),
        name="in_proj",
    )(x, pos_f, g_pre, w_in_p, q_norm, w_uq_p, kv_norm, w_uk_p, w_uv, b_gate)


def _attn_kernel(qt_ref, k_ref, vt_ref, o_ref, s_ref, mx_ref, m_ref, acc_ref, *, tq, tk):
    i = pl.program_id(2)
    n_q = pl.num_programs(2)
    key = lax.broadcasted_iota(jnp.int32, (tk, tq), 0)
    qry = lax.broadcasted_iota(jnp.int32, (tk, tq), 1)
    diag_masks = (key <= qry, key + tk <= qry)

    def scores(j, slot, mask=None, tile=None):
        r0 = pl.multiple_of(j * tk, tk)
        q_tile = i if tile is None else tile
        for hh in range(2):
            feats = slice(hh * HEAD_PAD, (hh + 1) * HEAD_PAD)
            kj = k_ref[pl.ds(r0, tk), feats]
            s = _dot(kj, qt_ref[q_tile, feats, :])
            if mask is not None:
                s = jnp.where(mask, s, NEG_BIG)
            s_ref[slot, hh] = s
            mx_ref[slot, hh] = jnp.max(s, axis=0, keepdims=True)

    vrow = lax.broadcasted_iota(jnp.int32, (2 * V_HEAD_DIM, tk), 0)
    one = jnp.ones((), BF16)

    def softmax_pv(j, slot):
        vt = vt_ref[j]
        for hh in range(2):
            m = m_ref[hh]
            m_new = jnp.maximum(m, mx_ref[slot, hh])
            alpha = jnp.exp2(m - m_new)
            p = jnp.exp2(s_ref[slot, hh] - m_new)
            vt_h = jnp.where(vrow == SUM_ROW[hh], one, vt)
            acc_ref[hh] = alpha * acc_ref[hh] + _dot(vt_h, p.astype(BF16))
            m_ref[hh] = m_new

    m_ref[...] = jnp.full(m_ref.shape, NEG_BIG, F32)
    acc_ref[...] = jnp.zeros(acc_ref.shape, F32)

    def two_blocks(t, last=False):
        scores(2 * t + 1, 1)
        softmax_pv(2 * t, 0)
        scores(2 * t + 2, 0, diag_masks[0] if last else None)
        softmax_pv(2 * t + 1, 1)

    def four_blocks(t, carry):
        two_blocks(2 * t)
        two_blocks(2 * t + 1)
        return carry

    n_pairs = jnp.maximum(i - 1, 0)
    lax.fori_loop(0, n_pairs // 2, four_blocks, 0)

    @pl.when(n_pairs % 2 == 1)
    def _():
        two_blocks(n_pairs - 1)

    def diagonal_tile():
        scores(2 * i + 1, 1, diag_masks[1])
        softmax_pv(2 * i, 0)
        scores(0, 0, tile=jnp.minimum(i + 1, n_q - 1))
        softmax_pv(2 * i + 1, 1)

    @pl.when(i > 0)
    def _():
        two_blocks(i - 1, last=True)
        diagonal_tile()

    @pl.when(i == 0)
    def _():
        scores(0, 0, diag_masks[0])
        diagonal_tile()

    a0 = acc_ref[0]
    a1 = acc_ref[1]
    o0 = a0[:V_HEAD_DIM] / a0[SUM_ROW[0]:SUM_ROW[0] + 1]
    o1 = a1[V_HEAD_DIM:] / a1[SUM_ROW[1]:SUM_ROW[1] + 1]
    out_t = jnp.concatenate([o0, o1], axis=0)
    o_ref[...] = out_t.T.astype(BF16)


def _attention(q_t, k_all, v_t):
    b, l, _ = k_all.shape
    tq, tk = ATTN_TILE, ATTN_KEY_BLOCK
    return pl.pallas_call(
        functools.partial(_attn_kernel, tq=tq, tk=tk),
        grid=(b, N_PAIRS, l // tq),
        in_specs=[
            pl.BlockSpec((None, l // tq, 2 * HEAD_PAD, tq), lambda bi, p, i: (bi, 0, p, 0)),
            pl.BlockSpec((None, l, 2 * HEAD_PAD), lambda bi, p, i: (bi, 0, p)),
            pl.BlockSpec((None, l // tk, 2 * V_HEAD_DIM, tk), lambda bi, p, i: (bi, 0, p, 0)),
        ],
        out_specs=pl.BlockSpec((None, tq, 2 * V_HEAD_DIM), lambda bi, p, i: (bi, i, p)),
        out_shape=jax.ShapeDtypeStruct((b, l, N_HEADS * V_HEAD_DIM), BF16),
        scratch_shapes=[
            pltpu.VMEM((2, 2, tk, tq), F32),
            pltpu.VMEM((2, 2, 1, tq), F32),
            pltpu.VMEM((2, 1, tq), F32),
            pltpu.VMEM((2, 2 * V_HEAD_DIM, tq), F32),
        ],
        compiler_params=pltpu.CompilerParams(
            dimension_semantics=("parallel", "parallel", "arbitrary"), vmem_limit_bytes=VMEM_LIMIT),
        name="attention",
    )(q_t, k_all, v_t)


def _s5_kernel(u_ref, are_ref, aim_ref, wb_ref, wcr_ref, wci_ref, d_ref, wglu_ref, bglu_ref,
               o_ref, state_ref, sre_ref, sim_ref, tb_ref, *, steps, batch):
    @pl.when(pl.program_id(0) == 0)
    def _():
        state_ref[...] = jnp.zeros_like(state_ref)

    lane_groups = SSM_WIDTH // LANES
    for b in range(batch):
        for c in range(lane_groups):
            c0 = b * SSM_WIDTH + c * LANES
            tb_ref[c, pl.ds(b, steps, stride=batch), :] = u_ref[:, c0:c0 + LANES].astype(F32)
    u32 = jnp.concatenate([tb_ref[c] for c in range(lane_groups)], axis=1)
    u = u32.astype(BF16)
    for k in range(SSM_HALVES):
        bu = _dot(u[:, k * SSM_HALF_IN:(k + 1) * SSM_HALF_IN], wb_ref[k])
        sre_ref[k] = bu[:, :SSM_HALF_STATE]
        sim_ref[k] = bu[:, SSM_HALF_STATE:]

    nt = (((1,), (1,)), ((), ()))
    ys = []
    for k in range(SSM_HALVES):
        ar = are_ref[:, k * SSM_HALF_STATE:(k + 1) * SSM_HALF_STATE]
        ai = aim_ref[:, k * SSM_HALF_STATE:(k + 1) * SSM_HALF_STATE]
        sr = state_ref[k, 0]
        si = state_ref[k, 1]
        for t in range(steps):
            rs = slice(t * batch, (t + 1) * batch)
            nr = ar * sr - ai * si + sre_ref[k, rs, :]
            ni = ar * si + ai * sr + sim_ref[k, rs, :]
            sre_ref[k, rs, :] = nr
            sim_ref[k, rs, :] = ni
            sr, si = nr, ni
        state_ref[k, 0] = sr
        state_ref[k, 1] = si
        ys.append(lax.dot_general(sre_ref[k].astype(BF16), wcr_ref[k], nt, preferred_element_type=F32)
                  - lax.dot_general(sim_ref[k].astype(BF16), wci_ref[k], nt, preferred_element_type=F32))

    y = jnp.concatenate(ys, axis=1) + d_ref[...] * u32
    g = _gelu_tanh(y)
    z = _dot(g.astype(BF16), wglu_ref[...]) + bglu_ref[...]
    out = g * jax.nn.sigmoid(z)
    for c in range(lane_groups):
        tb_ref[c] = out[:, c * LANES:(c + 1) * LANES]
    for b in range(batch):
        for c in range(lane_groups):
            c0 = b * SSM_WIDTH + c * LANES
            o_ref[:, c0:c0 + LANES] = tb_ref[c, pl.ds(b, steps, stride=batch), :].astype(BF16)


def _s5(u_tm, a_re_t, a_im_t, wb, wcr, wci, d_skip, w_glu, b_glu, batch):
    length = u_tm.shape[0]
    width = SSM_WIDTH
    steps = SSM_CHUNK
    rows = steps * batch
    const2 = lambda i: (0, 0)
    const3 = lambda i: (0, 0, 0)
    return pl.pallas_call(
        functools.partial(_s5_kernel, steps=steps, batch=batch),
        grid=(length // steps,),
        in_specs=[
            pl.BlockSpec((steps, batch * width), lambda i: (i, 0)),
            pl.BlockSpec(a_re_t.shape, const2),
            pl.BlockSpec(a_im_t.shape, const2),
            pl.BlockSpec(wb.shape, const3),
            pl.BlockSpec(wcr.shape, const3),
            pl.BlockSpec(wci.shape, const3),
            pl.BlockSpec((1, width), const2),
            pl.BlockSpec((width, width), const2),
            pl.BlockSpec((1, width), const2),
        ],
        out_specs=pl.BlockSpec((steps, batch * width), lambda i: (i, 0)),
        out_shape=jax.ShapeDtypeStruct((length, batch * width), BF16),
        scratch_shapes=[
            pltpu.VMEM((SSM_HALVES, 2, batch, SSM_HALF_STATE), F32),
            pltpu.VMEM((SSM_HALVES, rows, SSM_HALF_STATE), F32),
            pltpu.VMEM((SSM_HALVES, rows, SSM_HALF_STATE), F32),
            pltpu.VMEM((width // LANES, rows, LANES), F32),
        ],
        compiler_params=pltpu.CompilerParams(
            dimension_semantics=("arbitrary",), vmem_limit_bytes=VMEM_LIMIT),
        name="s5",
    )(u_tm, a_re_t, a_im_t, wb, wcr, wci, d_skip, w_glu, b_glu)


def _merge_kernel(x_ref, attn_ref, ssm_ref, gate_ref, wba_ref, wbs_ref, wout_ref, gpost_ref, o_ref):
    ga = gate_ref[:, :D_MODEL].astype(F32)
    gs = gate_ref[:, D_MODEL:].astype(F32)
    merged = ga * _dot(attn_ref[...], wba_ref[...]) + gs * _dot(ssm_ref[...], wbs_ref[...])
    m2 = _dot(merged.astype(BF16), wout_ref[...])
    o_ref[...] = x_ref[...] + _rms(m2, gpost_ref[...])


def _merge(x, attn, ssm_tm, gates, w_ba, w_bs, w_out, g_post):
    b, l, d = x.shape
    tl = ROW_TILE
    const = lambda *_: (0, 0)
    return pl.pallas_call(
        _merge_kernel,
        grid=(b, l // tl),
        in_specs=[
            pl.BlockSpec((None, tl, d), lambda bi, i: (bi, i, 0)),
            pl.BlockSpec((None, tl, N_HEADS * V_HEAD_DIM), lambda bi, i: (bi, i, 0)),
            pl.BlockSpec((tl, SSM_WIDTH), lambda bi, i: (i, bi)),
            pl.BlockSpec((None, tl, 2 * d), lambda bi, i: (bi, i, 0)),
            pl.BlockSpec(w_ba.shape, const),
            pl.BlockSpec(w_bs.shape, const),
            pl.BlockSpec(w_out.shape, const),
            pl.BlockSpec((1, d), const),
        ],
        out_specs=pl.BlockSpec((None, tl, d), lambda bi, i: (bi, i, 0)),
        out_shape=jax.ShapeDtypeStruct((b, l, d), F32),
        compiler_params=pltpu.CompilerParams(
            dimension_semantics=("parallel", "parallel"), vmem_limit_bytes=VMEM_LIMIT),
        name="merge",
    )(x, attn, ssm_tm, gates, w_ba, w_bs, w_out, g_post)


def _ffn_kernel(x_ref, halo_ref, gpre_ref, wup_ref, cw_ref, cb_ref, wdown_ref, gpost_ref, o_ref,
                act_ref, perm_ref):
    i = pl.program_id(1)
    tl, d = x_ref.shape
    n = tl // FFN_PHASES
    lane_groups = d // LANES

    for c in range(lane_groups):
        perm_ref[c] = x_ref[:, c * LANES:(c + 1) * LANES]
    x1 = jnp.concatenate(
        [jnp.concatenate([perm_ref[c, pl.ds(b, n, stride=FFN_PHASES), :] for b in range(FFN_PHASES)], axis=0)
         for c in range(lane_groups)], axis=1)

    xa = jnp.concatenate([halo_ref[...], x1], axis=0)
    hn = _rms(xa, gpre_ref[...])
    row_id = lax.broadcasted_iota(jnp.int32, (hn.shape[0], 1), 0)
    hn = jnp.where(row_id >= jnp.where(i > 0, 0, HALO), hn, 0.0).astype(BF16)

    first_row = lax.broadcasted_iota(jnp.int32, (n, 1), 0) == 0

    def conv(h, c0):
        cols = slice(c0, c0 + FFN_COL_CHUNK)
        blocks = [h[HALO + b * n:HALO + (b + 1) * n] for b in range(FFN_PHASES)]

        def moved_down(block, prev_token):
            return jnp.where(first_row, prev_token, pltpu.roll(block, 1, 0))

        s1 = moved_down(blocks[FFN_PHASES - 1], h[HALO - 1:HALO])
        s2 = moved_down(blocks[FFN_PHASES - 2], h[HALO - 2:HALO - 1])
        tap1 = [s1] + blocks[:-1]
        tap2 = [s2, s1] + blocks[:-2]
        w0, w1, w2, bias = cw_ref[0:1, cols], cw_ref[1:2, cols], cw_ref[2:3, cols], cb_ref[:, cols]
        return jnp.concatenate(
            [bias + w2 * blocks[b] + w1 * tap1[b] + w0 * tap2[b] for b in range(FFN_PHASES)], axis=0)

    for j in range(D_FF // FFN_COL_CHUNK):
        c0 = j * FFN_COL_CHUNK
        hg = _dot(hn, wup_ref[:, c0:c0 + FFN_COL_CHUNK])
        hv = _dot(hn, wup_ref[:, D_FF + c0:D_FF + c0 + FFN_COL_CHUNK])
        act = _gelu_tanh(conv(hg, c0)) * conv(hv, D_FF + c0)
        act_ref[:, c0:c0 + FFN_COL_CHUNK] = act.astype(BF16)

    ff = _dot(act_ref[...], wdown_ref[...])
    out = x1 + _rms(ff, gpost_ref[...])

    for c in range(lane_groups):
        for b in range(FFN_PHASES):
            perm_ref[c, pl.ds(b, n, stride=FFN_PHASES), :] = out[b * n:(b + 1) * n, c * LANES:(c + 1) * LANES]
    for c in range(lane_groups):
        o_ref[:, c * LANES:(c + 1) * LANES] = perm_ref[c]


def _conv_ffn(x1, g_pre, w_up, conv_w, conv_b, w_down, g_post):
    b, l, d = x1.shape
    tl = ROW_TILE
    const = lambda *_: (0, 0)
    halo_blocks = tl // HALO
    return pl.pallas_call(
        _ffn_kernel,
        grid=(b, l // tl),
        in_specs=[
            pl.BlockSpec((None, tl, d), lambda bi, i: (bi, i, 0)),
            pl.BlockSpec((None, HALO, d), lambda bi, i: (bi, jnp.maximum(i * halo_blocks - 1, 0), 0)),
            pl.BlockSpec((1, d), const),
            pl.BlockSpec(w_up.shape, const, pipeline_mode=pl.Buffered(1)),
            pl.BlockSpec(conv_w.shape, const),
            pl.BlockSpec(conv_b.shape, const),
            pl.BlockSpec(w_down.shape, const, pipeline_mode=pl.Buffered(1)),
            pl.BlockSpec((1, d), const),
        ],
        out_specs=pl.BlockSpec((None, tl, d), lambda bi, i: (bi, i, 0)),
        out_shape=jax.ShapeDtypeStruct((b, l, d), F32),
        scratch_shapes=[pltpu.VMEM((tl, D_FF), BF16),
                        pltpu.VMEM((d // LANES, tl, LANES), F32)],
        compiler_params=pltpu.CompilerParams(
            dimension_semantics=("parallel", "parallel"), vmem_limit_bytes=VMEM_LIMIT),
        name="conv_ffn",
    )(x1, x1, g_pre, w_up, conv_w, conv_b, w_down, g_post)


def _head_lane_groups(nope, rope):
    k = (nope if nope is not None else rope).shape[0]
    nope = jnp.zeros((k, N_HEADS, QK_NOPE_DIM), BF16) if nope is None else nope
    rope = jnp.zeros((k, N_HEADS, QK_ROPE_DIM), BF16) if rope is None else rope
    return jnp.concatenate([nope, rope, rope], axis=2).astype(BF16).reshape(k, -1)


def _layer(x, pos_f, p):
    b, l, d = x.shape
    row = lambda v: v.reshape(1, -1)

    w_in = p["w_in"]
    off_u = P_KR + QK_ROPE_DIM
    w_kr = w_in[:, P_KR:off_u]
    kr_cols = jnp.concatenate([jnp.zeros((d, QK_NOPE_DIM), w_in.dtype), w_kr, w_kr], axis=1)
    w_in_p = jnp.concatenate([w_in[:, :P_KR], kr_cols, w_in[:, off_u:]], axis=1).astype(BF16)
    w_uq = p["w_uq"].reshape(Q_LORA_RANK, N_HEADS, QK_HEAD_DIM)
    w_uq_p = _head_lane_groups(w_uq[:, :, :QK_NOPE_DIM], w_uq[:, :, QK_NOPE_DIM:])
    w_uk_p = _head_lane_groups(p["w_uk"].reshape(KV_LORA_RANK, N_HEADS, QK_NOPE_DIM), None)

    q_t, k_all, v_t, u_tm, gates = _in_proj(
        x, pos_f, row(p["mix_norm_pre"]), w_in_p, row(p["q_norm"]), w_uq_p.T, row(p["kv_norm"]),
        w_uk_p, p["w_uv"].T.astype(BF16), row(p["b_gate"]))

    attn = _attention(q_t, k_all, v_t)

    a_re_t, a_im_t, wb, wcr, wci = _s5_prep(
        p["ssm_lambda_re"], p["ssm_lambda_im"], p["ssm_log_dt"], p["ssm_b_re"], p["ssm_b_im"],
        p["ssm_c_re"], p["ssm_c_im"], b)
    ssm_tm = _s5(u_tm, a_re_t, a_im_t, wb, wcr, wci,
                 row(p["ssm_d"]), p["w_glu"].astype(BF16), row(p["b_glu"]), b)

    x1 = _merge(x, attn, ssm_tm, gates, p["w_branch_attn"].astype(BF16), p["w_branch_ssm"].astype(BF16),
                p["w_out"].astype(BF16), row(p["mix_norm_post"]))
    return _conv_ffn(x1, row(p["ffn_norm_pre"]), p["w_up"].astype(BF16), p["conv_w"], row(p["conv_b"]),
                     p["w_down"].astype(BF16), row(p["ffn_norm_post"]))


def kernel(x, positions, mix_norm_pre, w_in, q_norm, w_uq, kv_norm, w_uk, w_uv, ssm_lambda_re, ssm_lambda_im, ssm_log_dt, ssm_b_re, ssm_b_im, ssm_c_re, ssm_c_im, ssm_d, w_glu, b_glu, w_branch_attn, w_branch_ssm, b_gate, w_out, mix_norm_post, ffn_norm_pre, w_up, conv_w, conv_b, w_down, ffn_norm_post):
    b, l, _ = x.shape
    params = dict(mix_norm_pre=mix_norm_pre, w_in=w_in, q_norm=q_norm, w_uq=w_uq, kv_norm=kv_norm,
                  w_uk=w_uk, w_uv=w_uv, ssm_lambda_re=ssm_lambda_re, ssm_lambda_im=ssm_lambda_im,
                  ssm_log_dt=ssm_log_dt, ssm_b_re=ssm_b_re, ssm_b_im=ssm_b_im, ssm_c_re=ssm_c_re,
                  ssm_c_im=ssm_c_im, ssm_d=ssm_d, w_glu=w_glu, b_glu=b_glu, w_branch_attn=w_branch_attn,
                  w_branch_ssm=w_branch_ssm, b_gate=b_gate, w_out=w_out, mix_norm_post=mix_norm_post,
                  ffn_norm_pre=ffn_norm_pre, w_up=w_up, conv_w=conv_w, conv_b=conv_b, w_down=w_down,
                  ffn_norm_post=ffn_norm_post)
    pos_f = positions.astype(F32).reshape(b, 1, l)
    for layer in range(mix_norm_pre.shape[0]):
        x = _layer(x, pos_f, {k: v[layer] for k, v in params.items()})
    return x
```

```python
import functools
import math

import jax
import jax.numpy as jnp
from jax import lax
from jax.experimental import pallas as pl
from jax.experimental.pallas import tpu as pltpu

D_MODEL = 1024
N_HEADS = 8
QK_NOPE_DIM = 64
QK_ROPE_DIM = 32
QK_HEAD_DIM = QK_NOPE_DIM + QK_ROPE_DIM
V_HEAD_DIM = 64
Q_LORA_RANK = 384
KV_LORA_RANK = 256
ROPE_THETA = 10000.0
SSM_WIDTH = 512
SSM_GROUP = 16
SSM_GROUPS = SSM_WIDTH // SSM_GROUP
SSM_STATE = 64
D_FF = 2816
CONV_WIDTH = 3
EPS = 1e-6

LANES = 128
HEAD_PAD = LANES
ROPE_LANE0 = QK_NOPE_DIM
ROPE_HALF = QK_ROPE_DIM // 2
N_PAIRS = N_HEADS // 2
SUM_ROW = (V_HEAD_DIM, 0)
SSM_HALVES = 2
SSM_HALF_IN = SSM_WIDTH // SSM_HALVES
SSM_HALF_STATE = SSM_GROUPS * SSM_STATE // SSM_HALVES

P_CQ = 0
P_CKV = P_CQ + Q_LORA_RANK
P_KR = P_CKV + KV_LORA_RANK
P_U = P_KR + LANES
P_GATE = P_U + SSM_WIDTH
P_END = P_GATE + 2 * D_MODEL

Q_SCALE = (1.0 / math.sqrt(QK_HEAD_DIM)) * math.log2(math.e)
NEG_BIG = -1e30

ROW_TILE = 512
ATTN_TILE = ROW_TILE
ATTN_KEY_BLOCK = ATTN_TILE // 2
SSM_CHUNK = 64
FFN_COL_CHUNK = 256
FFN_PHASES = 8
HALO = 16
VMEM_LIMIT = 56 * 1024 * 1024

BF16 = jnp.bfloat16
F32 = jnp.float32


def _rms(x, g):
    return x * lax.rsqrt(jnp.mean(x * x, axis=-1, keepdims=True) + EPS) * g


def _dot(a, b):
    return jnp.dot(a, b, preferred_element_type=F32)


GELU_C = math.sqrt(2.0 / math.pi)


def _gelu_tanh(x):
    k = -2.0 * GELU_C * math.log2(math.e)
    e = jnp.exp2(x * (x * x * (k * 0.044715) + k))
    return x / (1.0 + e)


def _s5_prep_kernel(lr_ref, li_ref, logdt_ref, br_ref, bi_ref, cr_ref, ci_ref,
                    are_ref, aim_ref, wb_ref, wcr_ref, wci_ref, *, batch):
    lr = lr_ref[...]
    li = li_ref[...]
    dt = jnp.exp(logdt_ref[...])
    mag = jnp.exp(lr * dt)
    ang = li * dt
    a_re = mag * jnp.cos(ang)
    a_im = mag * jnp.sin(ang)
    den = lr * lr + li * li
    n_re = a_re - 1.0
    n_im = a_im
    z_re = (n_re * lr + n_im * li) / den
    z_im = (n_im * lr - n_re * li) / den
    are_ref[...] = jnp.broadcast_to(a_re, (batch, a_re.shape[1]))
    aim_ref[...] = jnp.broadcast_to(a_im, (batch, a_im.shape[1]))
    br = br_ref[...]
    bi = bi_ref[...]
    bb_re = z_re * br - z_im * bi
    bb_im = z_re * bi + z_im * br

    groups_per_half = SSM_GROUPS // SSM_HALVES
    row_group = jnp.right_shift(lax.broadcasted_iota(jnp.int32, (SSM_HALF_IN, SSM_HALF_STATE), 0),
                                SSM_GROUP.bit_length() - 1)
    col_group = jnp.right_shift(lax.broadcasted_iota(jnp.int32, (SSM_HALF_IN, SSM_HALF_STATE), 1),
                                SSM_STATE.bit_length() - 1)
    on_diag = row_group == col_group

    def diag(v):
        return jnp.where(on_diag, jnp.concatenate([v] * groups_per_half, axis=0), 0.0)

    for k in range(SSM_HALVES):
        cols = slice(k * SSM_HALF_STATE, (k + 1) * SSM_HALF_STATE)
        wb_ref[k, :, :SSM_HALF_STATE] = diag(bb_re[:, cols]).astype(BF16)
        wb_ref[k, :, SSM_HALF_STATE:] = diag(bb_im[:, cols]).astype(BF16)
        wcr_ref[k] = diag(cr_ref[:, cols]).astype(BF16)
        wci_ref[k] = diag(ci_ref[:, cols]).astype(BF16)


def _s5_prep(lam_re, lam_im, log_dt, b_re, b_im, c_re, c_im, batch):
    n = SSM_GROUPS * SSM_STATE
    lanes = lambda v: v.reshape(1, n)
    chan_major = lambda v, perm: v.transpose(perm).reshape(SSM_GROUP, n)
    return pl.pallas_call(
        functools.partial(_s5_prep_kernel, batch=batch),
        out_shape=(jax.ShapeDtypeStruct((batch, n), F32), jax.ShapeDtypeStruct((batch, n), F32),
                   jax.ShapeDtypeStruct((SSM_HALVES, SSM_HALF_IN, 2 * SSM_HALF_STATE), BF16),
                   jax.ShapeDtypeStruct((SSM_HALVES, SSM_HALF_IN, SSM_HALF_STATE), BF16),
                   jax.ShapeDtypeStruct((SSM_HALVES, SSM_HALF_IN, SSM_HALF_STATE), BF16)),
        name="s5_prep",
    )(lanes(lam_re), lanes(lam_im), lanes(jnp.repeat(log_dt, SSM_STATE)),
      chan_major(b_re, (2, 0, 1)), chan_major(b_im, (2, 0, 1)),
      chan_major(c_re, (1, 0, 2)), chan_major(c_im, (1, 0, 2)))


def _in_proj_kernel(x_ref, pos_ref, gpre_ref, win_ref, qn_ref, wuq_ref, kvn_ref, wuk_ref,
                    wuv_ref, bg_ref, q_ref, k_ref, vt_ref, u_ref, gate_ref):
    x = x_ref[...]
    tl = x.shape[0]
    hn = _rms(x, gpre_ref[...]).astype(BF16)

    freq_idx = lax.broadcasted_iota(jnp.int32, (ROPE_HALF, 1), 0).astype(F32)
    inv_freq = jnp.exp(freq_idx * (-2.0 * math.log(ROPE_THETA) / QK_ROPE_DIM))
    ang = inv_freq * pos_ref[...]
    c16 = jnp.cos(ang)
    s16 = jnp.sin(ang)
    ones = jnp.ones((ROPE_LANE0, tl), F32)
    zeros = lambda n: jnp.zeros((n, tl), F32)
    cos = jnp.concatenate([ones, zeros(ROPE_HALF), c16, c16, zeros(ROPE_HALF)], axis=0).T
    sin = jnp.concatenate([zeros(ROPE_LANE0 + ROPE_HALF), s16, -s16, zeros(ROPE_HALF)], axis=0).T
    cos_q = cos * Q_SCALE
    sin_q = sin * Q_SCALE

    low = _dot(hn, win_ref[:, P_CQ:P_U])
    cq = low[:, P_CQ:P_CKV]
    ckv = low[:, P_CKV:P_KR]
    kr = low[:, P_KR:P_U]
    cqn = _rms(cq, qn_ref[...]).astype(BF16)
    q = _dot(cqn, wuq_ref[...])
    for h in range(N_HEADS):
        sl = slice(h * HEAD_PAD, (h + 1) * HEAD_PAD)
        t = q[:, sl]
        q_ref[:, sl] = (t * cos_q + pltpu.roll(t, ROPE_HALF, 1) * sin_q).astype(BF16)

    ckvn = _rms(ckv, kvn_ref[...]).astype(BF16)
    v_t = lax.dot_general(wuv_ref[...], ckvn, (((1,), (1,)), ((), ())),
                          preferred_element_type=F32).astype(BF16)
    for c in range(vt_ref.shape[0]):
        vt_ref[c] = v_t[:, c * ATTN_KEY_BLOCK:(c + 1) * ATTN_KEY_BLOCK]
    kn = _dot(ckvn, wuk_ref[...])
    kr = kr * cos + pltpu.roll(kr, ROPE_HALF, 1) * sin
    for h in range(N_HEADS):
        sl = slice(h * HEAD_PAD, (h + 1) * HEAD_PAD)
        k_ref[:, sl] = (kn[:, sl] + kr).astype(BF16)

    u_ref[...] = _dot(hn, win_ref[:, P_U:P_GATE]).astype(BF16)
    logits = _dot(hn, win_ref[:, P_GATE:P_END]) + bg_ref[...]
    gate_ref[...] = jax.nn.sigmoid(logits).astype(BF16)


def _in_proj(x, pos_f, g_pre, w_in_p, q_norm, w_uq_p, kv_norm, w_uk_p, w_uv, b_gate):
    b, l, d = x.shape
    tl = ROW_TILE
    const = lambda *_: (0, 0)
    return pl.pallas_call(
        _in_proj_kernel,
        grid=(b, l // tl),
        in_specs=[
            pl.BlockSpec((None, tl, d), lambda bi, i: (bi, i, 0)),
            pl.BlockSpec((None, 1, tl), lambda bi, i: (bi, 0, i)),
            pl.BlockSpec((1, d), const),
            pl.BlockSpec((d, P_END), const),
            pl.BlockSpec((1, Q_LORA_RANK), const),
            pl.BlockSpec((Q_LORA_RANK, N_HEADS * HEAD_PAD), const),
            pl.BlockSpec((1, KV_LORA_RANK), const),
            pl.BlockSpec((KV_LORA_RANK, N_HEADS * HEAD_PAD), const),
            pl.BlockSpec((N_HEADS * V_HEAD_DIM, KV_LORA_RANK), const),
            pl.BlockSpec((1, 2 * D_MODEL), const),
        ],
        out_specs=[
            pl.BlockSpec((None, tl, N_HEADS * HEAD_PAD), lambda bi, i: (bi, i, 0)),
            pl.BlockSpec((None, tl, N_HEADS * HEAD_PAD), lambda bi, i: (bi, i, 0)),
            pl.BlockSpec((None, tl // ATTN_KEY_BLOCK, N_HEADS * V_HEAD_DIM, ATTN_KEY_BLOCK),
                         lambda bi, i: (bi, i, 0, 0)),
            pl.BlockSpec((tl, SSM_WIDTH), lambda bi, i: (i, bi)),
            pl.BlockSpec((None, tl, 2 * D_MODEL), lambda bi, i: (bi, i, 0)),
        ],
        out_shape=[
            jax.ShapeDtypeStruct((b, l, N_HEADS * HEAD_PAD), BF16),
            jax.ShapeDtypeStruct((b, l, N_HEADS * HEAD_PAD), BF16),
            jax.ShapeDtypeStruct((b, l // ATTN_KEY_BLOCK, N_HEADS * V_HEAD_DIM, ATTN_KEY_BLOCK), BF16),
            jax.ShapeDtypeStruct((l, b * SSM_WIDTH), BF16),
            jax.ShapeDtypeStruct((b, l, 2 * D_MODEL), BF16),
        ],
        compiler_params=pltpu.CompilerParams(
            dimension_semantics=("parallel", "parallel"), vmem_limit_bytes=VMEM_LIMIT),
        name="in_proj",
    )(x, pos_f, g_pre, w_in_p, q_norm, w_uq_p, kv_norm, w_uk_p, w_uv, b_gate)


def _attn_kernel(q_ref, k_ref, vt_ref, o_ref, s_ref, mx_ref, m_ref, acc_ref, *, tq, tk):
    i = pl.program_id(2)
    n_q = pl.num_programs(2)
    key = lax.broadcasted_iota(jnp.int32, (tk, tq), 0)
    qry = lax.broadcasted_iota(jnp.int32, (tk, tq), 1)
    diag_mask = key <= qry

    def scores(j, slot, mask=None, tile=None, cols=slice(None)):
        r0 = pl.multiple_of(j * tk, tk)
        nq = tq if cols.start is None else cols.stop - cols.start
        q0 = pl.multiple_of((i if tile is None else tile) * tq + (cols.start or 0), nq)
        for hh in range(2):
            lanes = slice(hh * HEAD_PAD, (hh + 1) * HEAD_PAD)
            kj = k_ref[pl.ds(r0, tk), lanes]
            qt = q_ref[pl.ds(q0, nq), lanes]
            s = lax.dot_general(kj, qt, (((1,), (1,)), ((), ())), preferred_element_type=F32)
            if mask is not None:
                s = jnp.where(mask, s, NEG_BIG)
            s_ref[slot, hh, :, cols] = s
            mx_ref[slot, hh, :, cols] = jnp.max(s, axis=0, keepdims=True)

    vrow = lax.broadcasted_iota(jnp.int32, (2 * V_HEAD_DIM, tk), 0)
    one = jnp.ones((), BF16)

    def softmax_pv(j, slot, cols=slice(None)):
        vt = vt_ref[j]
        for hh in range(2):
            m = m_ref[hh, :, cols]
            m_new = jnp.maximum(m, mx_ref[slot, hh, :, cols])
            alpha = jnp.exp2(m - m_new)
            p = jnp.exp2(s_ref[slot, hh, :, cols] - m_new)
            vt_h = jnp.where(vrow == SUM_ROW[hh], one, vt)
            acc_ref[hh, :, cols] = alpha * acc_ref[hh, :, cols] + _dot(vt_h, p.astype(BF16))
            m_ref[hh, :, cols] = m_new

    m_ref[...] = jnp.full(m_ref.shape, NEG_BIG, F32)
    acc_ref[...] = jnp.zeros(acc_ref.shape, F32)

    def two_blocks(t, last=False):
        scores(2 * t + 1, 1)
        softmax_pv(2 * t, 0)
        scores(2 * t + 2, 0, diag_mask if last else None)
        softmax_pv(2 * t + 1, 1)

    def four_blocks(t, carry):
        two_blocks(2 * t)
        two_blocks(2 * t + 1)
        return carry

    n_pairs = jnp.maximum(i - 1, 0)
    lax.fori_loop(0, n_pairs // 2, four_blocks, 0)

    @pl.when(n_pairs % 2 == 1)
    def _():
        two_blocks(n_pairs - 1)

    def diagonal_tile():
        late = slice(tk, tq)
        scores(2 * i + 1, 1, diag_mask[:, :tq - tk], cols=late)
        softmax_pv(2 * i, 0)
        scores(0, 0, tile=jnp.minimum(i + 1, n_q - 1))
        softmax_pv(2 * i + 1, 1, cols=late)

    @pl.when(i > 0)
    def _():
        two_blocks(i - 1, last=True)
        diagonal_tile()

    @pl.when(i == 0)
    def _():
        scores(0, 0, diag_mask)
        diagonal_tile()

    a0 = acc_ref[0]
    a1 = acc_ref[1]
    o0 = a0[:V_HEAD_DIM] / a0[SUM_ROW[0]:SUM_ROW[0] + 1]
    o1 = a1[V_HEAD_DIM:] / a1[SUM_ROW[1]:SUM_ROW[1] + 1]
    out_t = jnp.concatenate([o0, o1], axis=0)
    o_ref[...] = out_t.T.astype(BF16)


def _attention(q_all, k_all, v_t):
    b, l, _ = q_all.shape
    tq, tk = ATTN_TILE, ATTN_KEY_BLOCK
    return pl.pallas_call(
        functools.partial(_attn_kernel, tq=tq, tk=tk),
        grid=(b, N_PAIRS, l // tq),
        in_specs=[
            pl.BlockSpec((None, l, 2 * HEAD_PAD), lambda bi, p, i: (bi, 0, p)),
            pl.BlockSpec((None, l, 2 * HEAD_PAD), lambda bi, p, i: (bi, 0, p)),
            pl.BlockSpec((None, l // tk, 2 * V_HEAD_DIM, tk), lambda bi, p, i: (bi, 0, p, 0)),
        ],
        out_specs=pl.BlockSpec((None, tq, 2 * V_HEAD_DIM), lambda bi, p, i: (bi, i, p)),
        out_shape=jax.ShapeDtypeStruct((b, l, N_HEADS * V_HEAD_DIM), BF16),
        scratch_shapes=[
            pltpu.VMEM((2, 2, tk, tq), F32),
            pltpu.VMEM((2, 2, 1, tq), F32),
            pltpu.VMEM((2, 1, tq), F32),
            pltpu.VMEM((2, 2 * V_HEAD_DIM, tq), F32),
        ],
        compiler_params=pltpu.CompilerParams(
            dimension_semantics=("parallel", "parallel", "arbitrary"), vmem_limit_bytes=VMEM_LIMIT),
        name="attention",
    )(q_all, k_all, v_t)


def _s5_kernel(u_ref, are_ref, aim_ref, wb_ref, wcr_ref, wci_ref, d_ref, wglu_ref, bglu_ref,
               o_ref, state_ref, sre_ref, sim_ref, tb_ref, *, steps, batch):
    @pl.when(pl.program_id(0) == 0)
    def _():
        state_ref[...] = jnp.zeros_like(state_ref)

    lane_groups = SSM_WIDTH // LANES
    for b in range(batch):
        for c in range(lane_groups):
            c0 = b * SSM_WIDTH + c * LANES
            tb_ref[c, pl.ds(b, steps, stride=batch), :] = u_ref[:, c0:c0 + LANES].astype(F32)
    u32 = jnp.concatenate([tb_ref[c] for c in range(lane_groups)], axis=1)
    u = u32.astype(BF16)
    for k in range(SSM_HALVES):
        bu = _dot(u[:, k * SSM_HALF_IN:(k + 1) * SSM_HALF_IN], wb_ref[k])
        sre_ref[k] = bu[:, :SSM_HALF_STATE]
        sim_ref[k] = bu[:, SSM_HALF_STATE:]

    nt = (((1,), (1,)), ((), ()))
    ys = []
    for k in range(SSM_HALVES):
        ar = are_ref[:, k * SSM_HALF_STATE:(k + 1) * SSM_HALF_STATE]
        ai = aim_ref[:, k * SSM_HALF_STATE:(k + 1) * SSM_HALF_STATE]
        sr = state_ref[k, 0]
        si = state_ref[k, 1]
        for t in range(steps):
            rs = slice(t * batch, (t + 1) * batch)
            nr = ar * sr - ai * si + sre_ref[k, rs, :]
            ni = ar * si + ai * sr + sim_ref[k, rs, :]
            sre_ref[k, rs, :] = nr
            sim_ref[k, rs, :] = ni
            sr, si = nr, ni
        state_ref[k, 0] = sr
        state_ref[k, 1] = si
        ys.append(lax.dot_general(sre_ref[k].astype(BF16), wcr_ref[k], nt, preferred_element_type=F32)
                  - lax.dot_general(sim_ref[k].astype(BF16), wci_ref[k], nt, preferred_element_type=F32))

    y = jnp.concatenate(ys, axis=1) + d_ref[...] * u32
    g = _gelu_tanh(y)
    z = _dot(g.astype(BF16), wglu_ref[...]) + bglu_ref[...]
    out = g * jax.nn.sigmoid(z)
    for c in range(lane_groups):
        tb_ref[c] = out[:, c * LANES:(c + 1) * LANES]
    for b in range(batch):
        for c in range(lane_groups):
            c0 = b * SSM_WIDTH + c * LANES
            o_ref[:, c0:c0 + LANES] = tb_ref[c, pl.ds(b, steps, stride=batch), :].astype(BF16)


def _s5(u_tm, a_re_t, a_im_t, wb, wcr, wci, d_skip, w_glu, b_glu, batch):
    length = u_tm.shape[0]
    width = SSM_WIDTH
    steps = SSM_CHUNK
    rows = steps * batch
    const2 = lambda i: (0, 0)
    const3 = lambda i: (0, 0, 0)
    return pl.pallas_call(
        functools.partial(_s5_kernel, steps=steps, batch=batch),
        grid=(length // steps,),
        in_specs=[
            pl.BlockSpec((steps, batch * width), lambda i: (i, 0)),
            pl.BlockSpec(a_re_t.shape, const2),
            pl.BlockSpec(a_im_t.shape, const2),
            pl.BlockSpec(wb.shape, const3),
            pl.BlockSpec(wcr.shape, const3),
            pl.BlockSpec(wci.shape, const3),
            pl.BlockSpec((1, width), const2),
            pl.BlockSpec((width, width), const2),
            pl.BlockSpec((1, width), const2),
        ],
        out_specs=pl.BlockSpec((steps, batch * width), lambda i: (i, 0)),
        out_shape=jax.ShapeDtypeStruct((length, batch * width), BF16),
        scratch_shapes=[
            pltpu.VMEM((SSM_HALVES, 2, batch, SSM_HALF_STATE), F32),
            pltpu.VMEM((SSM_HALVES, rows, SSM_HALF_STATE), F32),
            pltpu.VMEM((SSM_HALVES, rows, SSM_HALF_STATE), F32),
            pltpu.VMEM((width // LANES, rows, LANES), F32),
        ],
        compiler_params=pltpu.CompilerParams(
            dimension_semantics=("arbitrary",), vmem_limit_bytes=VMEM_LIMIT),
        name="s5",
    )(u_tm, a_re_t, a_im_t, wb, wcr, wci, d_skip, w_glu, b_glu)


def _merge_kernel(x_ref, attn_ref, ssm_ref, gate_ref, wba_ref, wbs_ref, wout_ref, gpost_ref, o_ref):
    ga = gate_ref[:, :D_MODEL].astype(F32)
    gs = gate_ref[:, D_MODEL:].astype(F32)
    merged = ga * _dot(attn_ref[...], wba_ref[...]) + gs * _dot(ssm_ref[...], wbs_ref[...])
    m2 = _dot(merged.astype(BF16), wout_ref[...])
    o_ref[...] = x_ref[...] + _rms(m2, gpost_ref[...])


def _merge(x, attn, ssm_tm, gates, w_ba, w_bs, w_out, g_post):
    b, l, d = x.shape
    tl = ROW_TILE
    const = lambda *_: (0, 0)
    return pl.pallas_call(
        _merge_kernel,
        grid=(b, l // tl),
        in_specs=[
            pl.BlockSpec((None, tl, d), lambda bi, i: (bi, i, 0)),
            pl.BlockSpec((None, tl, N_HEADS * V_HEAD_DIM), lambda bi, i: (bi, i, 0)),
            pl.BlockSpec((tl, SSM_WIDTH), lambda bi, i: (i, bi)),
            pl.BlockSpec((None, tl, 2 * d), lambda bi, i: (bi, i, 0)),
            pl.BlockSpec(w_ba.shape, const),
            pl.BlockSpec(w_bs.shape, const),
            pl.BlockSpec(w_out.shape, const),
            pl.BlockSpec((1, d), const),
        ],
        out_specs=pl.BlockSpec((None, tl, d), lambda bi, i: (bi, i, 0)),
        out_shape=jax.ShapeDtypeStruct((b, l, d), F32),
        compiler_params=pltpu.CompilerParams(
            dimension_semantics=("parallel", "parallel"), vmem_limit_bytes=VMEM_LIMIT),
        name="merge",
    )(x, attn, ssm_tm, gates, w_ba, w_bs, w_out, g_post)


def _ffn_kernel(x_ref, halo_ref, gpre_ref, wup_ref, cw_ref, cb_ref, wdown_ref, gpost_ref, o_ref,
                act_ref, perm_ref):
    i = pl.program_id(1)
    tl, d = x_ref.shape
    n = tl // FFN_PHASES
    lane_groups = d // LANES

    for c in range(lane_groups):
        perm_ref[c] = x_ref[:, c * LANES:(c + 1) * LANES]
    x1 = jnp.concatenate(
        [jnp.concatenate([perm_ref[c, pl.ds(b, n, stride=FFN_PHASES), :] for b in range(FFN_PHASES)], axis=0)
         for c in range(lane_groups)], axis=1)

    xa = jnp.concatenate([halo_ref[...], x1], axis=0)
    hn = _rms(xa, gpre_ref[...])
    row_id = lax.broadcasted_iota(jnp.int32, (hn.shape[0], 1), 0)
    hn = jnp.where(row_id >= jnp.where(i > 0, 0, HALO), hn, 0.0).astype(BF16)

    first_row = lax.broadcasted_iota(jnp.int32, (n, 1), 0) == 0

    def conv(h, c0):
        cols = slice(c0, c0 + FFN_COL_CHUNK)
        blocks = [h[HALO + b * n:HALO + (b + 1) * n] for b in range(FFN_PHASES)]

        def moved_down(block, prev_token):
            return jnp.where(first_row, prev_token, pltpu.roll(block, 1, 0))

        s1 = moved_down(blocks[FFN_PHASES - 1], h[HALO - 1:HALO])
        s2 = moved_down(blocks[FFN_PHASES - 2], h[HALO - 2:HALO - 1])
        tap1 = [s1] + blocks[:-1]
        tap2 = [s2, s1] + blocks[:-2]
        w0, w1, w2, bias = cw_ref[0:1, cols], cw_ref[1:2, cols], cw_ref[2:3, cols], cb_ref[:, cols]
        return jnp.concatenate(
            [bias + w2 * blocks[b] + w1 * tap1[b] + w0 * tap2[b] for b in range(FFN_PHASES)], axis=0)

    for j in range(D_FF // FFN_COL_CHUNK):
        c0 = j * FFN_COL_CHUNK
        hg = _dot(hn, wup_ref[:, c0:c0 + FFN_COL_CHUNK])
        hv = _dot(hn, wup_ref[:, D_FF + c0:D_FF + c0 + FFN_COL_CHUNK])
        act = _gelu_tanh(conv(hg, c0)) * conv(hv, D_FF + c0)
        act_ref[:, c0:c0 + FFN_COL_CHUNK] = act.astype(BF16)

    ff = _dot(act_ref[...], wdown_ref[...])
    out = x1 + _rms(ff, gpost_ref[...])

    for c in range(lane_groups):
        for b in range(FFN_PHASES):
            perm_ref[c, pl.ds(b, n, stride=FFN_PHASES), :] = out[b * n:(b + 1) * n, c * LANES:(c + 1) * LANES]
    for c in range(lane_groups):
        o_ref[:, c * LANES:(c + 1) * LANES] = perm_ref[c]


def _conv_ffn(x1, g_pre, w_up, conv_w, conv_b, w_down, g_post):
    b, l, d = x1.shape
    tl = ROW_TILE
    const = lambda *_: (0, 0)
    halo_blocks = tl // HALO
    return pl.pallas_call(
        _ffn_kernel,
        grid=(b, l // tl),
        in_specs=[
            pl.BlockSpec((None, tl, d), lambda bi, i: (bi, i, 0)),
            pl.BlockSpec((None, HALO, d), lambda bi, i: (bi, jnp.maximum(i * halo_blocks - 1, 0), 0)),
            pl.BlockSpec((1, d), const),
            pl.BlockSpec(w_up.shape, const, pipeline_mode=pl.Buffered(1)),
            pl.BlockSpec(conv_w.shape, const),
            pl.BlockSpec(conv_b.shape, const),
            pl.BlockSpec(w_down.shape, const, pipeline_mode=pl.Buffered(1)),
            pl.BlockSpec((1, d), const),
        ],
        out_specs=pl.BlockSpec((None, tl, d), lambda bi, i: (bi, i, 0)),
        out_shape=jax.ShapeDtypeStruct((b, l, d), F32),
        scratch_shapes=[pltpu.VMEM((tl, D_FF), BF16),
                        pltpu.VMEM((d // LANES, tl, LANES), F32)],
        compiler_params=pltpu.CompilerParams(
            dimension_semantics=("parallel", "parallel"), vmem_limit_bytes=VMEM_LIMIT),
        name="conv_ffn",
    )(x1, x1, g_pre, w_up, conv_w, conv_b, w_down, g_post)


def _head_lane_groups(nope, rope):
    k = (nope if nope is not None else rope).shape[0]
    nope = jnp.zeros((k, N_HEADS, QK_NOPE_DIM), BF16) if nope is None else nope
    rope = jnp.zeros((k, N_HEADS, QK_ROPE_DIM), BF16) if rope is None else rope
    return jnp.concatenate([nope, rope, rope], axis=2).astype(BF16).reshape(k, -1)


def _layer(x, pos_f, p):
    b, l, d = x.shape
    row = lambda v: v.reshape(1, -1)

    w_in = p["w_in"]
    off_u = P_KR + QK_ROPE_DIM
    w_kr = w_in[:, P_KR:off_u]
    kr_cols = jnp.concatenate([jnp.zeros((d, QK_NOPE_DIM), w_in.dtype), w_kr, w_kr], axis=1)
    w_in_p = jnp.concatenate([w_in[:, :P_KR], kr_cols, w_in[:, off_u:]], axis=1).astype(BF16)
    w_uq = p["w_uq"].reshape(Q_LORA_RANK, N_HEADS, QK_HEAD_DIM)
    w_uq_p = _head_lane_groups(w_uq[:, :, :QK_NOPE_DIM], w_uq[:, :, QK_NOPE_DIM:])
    w_uk_p = _head_lane_groups(p["w_uk"].reshape(KV_LORA_RANK, N_HEADS, QK_NOPE_DIM), None)

    q_all, k_all, v_t, u_tm, gates = _in_proj(
        x, pos_f, row(p["mix_norm_pre"]), w_in_p, row(p["q_norm"]), w_uq_p, row(p["kv_norm"]),
        w_uk_p, p["w_uv"].T.astype(BF16), row(p["b_gate"]))

    attn = _attention(q_all, k_all, v_t)

    a_re_t, a_im_t, wb, wcr, wci = _s5_prep(
        p["ssm_lambda_re"], p["ssm_lambda_im"], p["ssm_log_dt"], p["ssm_b_re"], p["ssm_b_im"],
        p["ssm_c_re"], p["ssm_c_im"], b)
    ssm_tm = _s5(u_tm, a_re_t, a_im_t, wb, wcr, wci,
                 row(p["ssm_d"]), p["w_glu"].astype(BF16), row(p["b_glu"]), b)

    x1 = _merge(x, attn, ssm_tm, gates, p["w_branch_attn"].astype(BF16), p["w_branch_ssm"].astype(BF16),
                p["w_out"].astype(BF16), row(p["mix_norm_post"]))
    return _conv_ffn(x1, row(p["ffn_norm_pre"]), p["w_up"].astype(BF16), p["conv_w"], row(p["conv_b"]),
                     p["w_down"].astype(BF16), row(p["ffn_norm_post"]))


def kernel(x, positions, mix_norm_pre, w_in, q_norm, w_uq, kv_norm, w_uk, w_uv, ssm_lambda_re, ssm_lambda_im, ssm_log_dt, ssm_b_re, ssm_b_im, ssm_c_re, ssm_c_im, ssm_d, w_glu, b_glu, w_branch_attn, w_branch_ssm, b_gate, w_out, mix_norm_post, ffn_norm_pre, w_up, conv_w, conv_b, w_down, ffn_norm_post):
    b, l, _ = x.shape
    params = dict(mix_norm_pre=mix_norm_pre, w_in=w_in, q_norm=q_norm, w_uq=w_uq, kv_norm=kv_norm,
                  w_uk=w_uk, w_uv=w_uv, ssm_lambda_re=ssm_lambda_re, ssm_lambda_im=ssm_lambda_im,
                  ssm_log_dt=ssm_log_dt, ssm_b_re=ssm_b_re, ssm_b_im=ssm_b_im, ssm_c_re=ssm_c_re,
                  ssm_c_im=ssm_c_im, ssm_d=ssm_d, w_glu=w_glu, b_glu=b_glu, w_branch_attn=w_branch_attn,
                  w_branch_ssm=w_branch_ssm, b_gate=b_gate, w_out=w_out, mix_norm_post=mix_norm_post,
                  ffn_norm_pre=ffn_norm_pre, w_up=w_up, conv_w=conv_w, conv_b=conv_b, w_down=w_down,
                  ffn_norm_post=ffn_norm_post)
    pos_f = positions.astype(F32).reshape(b, 1, l)
    for layer in range(mix_norm_pre.shape[0]):
        x = _layer(x, pos_f, {k: v[layer] for k, v in params.items()})
    return x
```

```python
import functools
import math

import jax
import jax.numpy as jnp
from jax import lax
from jax.experimental import pallas as pl
from jax.experimental.pallas import tpu as pltpu

D_MODEL = 1024
N_HEADS = 8
QK_NOPE_DIM = 64
QK_ROPE_DIM = 32
QK_HEAD_DIM = QK_NOPE_DIM + QK_ROPE_DIM
V_HEAD_DIM = 64
Q_LORA_RANK = 384
KV_LORA_RANK = 256
ROPE_THETA = 10000.0
SSM_WIDTH = 512
SSM_GROUP = 16
SSM_GROUPS = SSM_WIDTH // SSM_GROUP
SSM_STATE = 64
D_FF = 2816
CONV_WIDTH = 3
EPS = 1e-6

LANES = 128
HEAD_PAD = LANES
ROPE_LANE0 = QK_NOPE_DIM
ROPE_HALF = QK_ROPE_DIM // 2
N_PAIRS = N_HEADS // 2
SUM_ROW = (V_HEAD_DIM, 0)
SSM_HALVES = 2
SSM_HALF_IN = SSM_WIDTH // SSM_HALVES
SSM_HALF_STATE = SSM_GROUPS * SSM_STATE // SSM_HALVES

P_CQ = 0
P_CKV = P_CQ + Q_LORA_RANK
P_KR = P_CKV + KV_LORA_RANK
P_U = P_KR + LANES

Q_SCALE = (1.0 / math.sqrt(QK_HEAD_DIM)) * math.log2(math.e)
NEG_BIG = -1e30

ROW_TILE = 512
ATTN_TILE = ROW_TILE
ATTN_KEY_BLOCK = ATTN_TILE // 2
SSM_CHUNK = 64
FFN_COL_CHUNK = 256
FFN_PHASES = 8
VMEM_LIMIT = 56 * 1024 * 1024

BF16 = jnp.bfloat16
F32 = jnp.float32


def _rms(x, g):
    return x * lax.rsqrt(jnp.mean(x * x, axis=-1, keepdims=True) + EPS) * g


def _dot(a, b):
    return jnp.dot(a, b, preferred_element_type=F32)


GELU_C = math.sqrt(2.0 / math.pi)


def _gelu_tanh(x):
    k = -2.0 * GELU_C * math.log2(math.e)
    e = jnp.exp2(x * (x * x * (k * 0.044715) + k))
    return x / (1.0 + e)


def _s5_prep_kernel(lr_ref, li_ref, logdt_ref, br_ref, bi_ref, cr_ref, ci_ref,
                    are_ref, aim_ref, wb_ref, wcr_ref, wci_ref, *, batch):
    lr = lr_ref[...]
    li = li_ref[...]
    dt = jnp.exp(logdt_ref[...])
    mag = jnp.exp(lr * dt)
    ang = li * dt
    a_re = mag * jnp.cos(ang)
    a_im = mag * jnp.sin(ang)
    den = lr * lr + li * li
    n_re = a_re - 1.0
    n_im = a_im
    z_re = (n_re * lr + n_im * li) / den
    z_im = (n_im * lr - n_re * li) / den
    are_ref[...] = jnp.broadcast_to(a_re, (batch, a_re.shape[1]))
    aim_ref[...] = jnp.broadcast_to(a_im, (batch, a_im.shape[1]))
    br = br_ref[...]
    bi = bi_ref[...]
    bb_re = z_re * br - z_im * bi
    bb_im = z_re * bi + z_im * br

    groups_per_half = SSM_GROUPS // SSM_HALVES
    row_group = jnp.right_shift(lax.broadcasted_iota(jnp.int32, (SSM_HALF_IN, SSM_HALF_STATE), 0),
                                SSM_GROUP.bit_length() - 1)
    col_group = jnp.right_shift(lax.broadcasted_iota(jnp.int32, (SSM_HALF_IN, SSM_HALF_STATE), 1),
                                SSM_STATE.bit_length() - 1)
    on_diag = row_group == col_group

    def diag(v):
        return jnp.where(on_diag, jnp.concatenate([v] * groups_per_half, axis=0), 0.0)

    for k in range(SSM_HALVES):
        cols = slice(k * SSM_HALF_STATE, (k + 1) * SSM_HALF_STATE)
        wb_ref[k, :, :SSM_HALF_STATE] = diag(bb_re[:, cols]).astype(BF16)
        wb_ref[k, :, SSM_HALF_STATE:] = diag(bb_im[:, cols]).astype(BF16)
        wcr_ref[k] = diag(cr_ref[:, cols]).astype(BF16)
        wci_ref[k] = diag(ci_ref[:, cols]).astype(BF16)


def _s5_prep(lam_re, lam_im, log_dt, b_re, b_im, c_re, c_im, batch):
    n = SSM_GROUPS * SSM_STATE
    lanes = lambda v: v.reshape(1, n)
    chan_major = lambda v, perm: v.transpose(perm).reshape(SSM_GROUP, n)
    return pl.pallas_call(
        functools.partial(_s5_prep_kernel, batch=batch),
        out_shape=(jax.ShapeDtypeStruct((batch, n), F32), jax.ShapeDtypeStruct((batch, n), F32),
                   jax.ShapeDtypeStruct((SSM_HALVES, SSM_HALF_IN, 2 * SSM_HALF_STATE), BF16),
                   jax.ShapeDtypeStruct((SSM_HALVES, SSM_HALF_IN, SSM_HALF_STATE), BF16),
                   jax.ShapeDtypeStruct((SSM_HALVES, SSM_HALF_IN, SSM_HALF_STATE), BF16)),
        name="s5_prep",
    )(lanes(lam_re), lanes(lam_im), lanes(jnp.repeat(log_dt, SSM_STATE)),
      chan_major(b_re, (2, 0, 1)), chan_major(b_im, (2, 0, 1)),
      chan_major(c_re, (1, 0, 2)), chan_major(c_im, (1, 0, 2)))


def _in_proj_kernel(x_ref, pos_ref, gpre_ref, wlow_ref, wu_ref, wgate_ref, qn_ref, wuq_ref, kvn_ref, wuk_ref,
                    wuv_ref, bg_ref, q_ref, k_ref, vt_ref, u_ref, gate_ref):
    x = x_ref[...]
    tl = x.shape[0]
    hn = _rms(x, gpre_ref[...]).astype(BF16)

    freq_idx = lax.broadcasted_iota(jnp.int32, (ROPE_HALF, 1), 0).astype(F32)
    inv_freq = jnp.exp(freq_idx * (-2.0 * math.log(ROPE_THETA) / QK_ROPE_DIM))
    ang = inv_freq * pos_ref[...]
    c16 = jnp.cos(ang)
    s16 = jnp.sin(ang)
    ones = jnp.ones((ROPE_LANE0, tl), F32)
    zeros = lambda n: jnp.zeros((n, tl), F32)
    cos = jnp.concatenate([ones, zeros(ROPE_HALF), c16, c16, zeros(ROPE_HALF)], axis=0).T
    sin = jnp.concatenate([zeros(ROPE_LANE0 + ROPE_HALF), s16, -s16, zeros(ROPE_HALF)], axis=0).T
    cos_q = cos * Q_SCALE
    sin_q = sin * Q_SCALE

    low = _dot(hn, wlow_ref[...])
    cq = low[:, P_CQ:P_CKV]
    ckv = low[:, P_CKV:P_KR]
    kr = low[:, P_KR:P_U]
    cqn = _rms(cq, qn_ref[...]).astype(BF16)
    q = _dot(cqn, wuq_ref[...])
    for h in range(N_HEADS):
        sl = slice(h * HEAD_PAD, (h + 1) * HEAD_PAD)
        t = q[:, sl]
        q_ref[:, sl] = (t * cos_q + pltpu.roll(t, ROPE_HALF, 1) * sin_q).astype(BF16)

    ckvn = _rms(ckv, kvn_ref[...]).astype(BF16)
    v_t = lax.dot_general(wuv_ref[...], ckvn, (((1,), (1,)), ((), ())),
                          preferred_element_type=F32).astype(BF16)
    for c in range(vt_ref.shape[0]):
        vt_ref[c] = v_t[:, c * ATTN_KEY_BLOCK:(c + 1) * ATTN_KEY_BLOCK]
    kn = _dot(ckvn, wuk_ref[...])
    kr = kr * cos + pltpu.roll(kr, ROPE_HALF, 1) * sin
    for h in range(N_HEADS):
        sl = slice(h * HEAD_PAD, (h + 1) * HEAD_PAD)
        k_ref[:, sl] = (kn[:, sl] + kr).astype(BF16)

    u_ref[...] = _dot(hn, wu_ref[...]).astype(BF16)
    logits = _dot(hn, wgate_ref[...]) + bg_ref[...]
    gate_ref[...] = jax.nn.sigmoid(logits).astype(BF16)


def _in_proj(x, pos_f, g_pre, w_low, w_u, w_gate, q_norm, w_uq_p, kv_norm, w_uk_p, w_uv, b_gate):
    b, l, d = x.shape
    tl = ROW_TILE
    const = lambda *_: (0, 0)
    return pl.pallas_call(
        _in_proj_kernel,
        grid=(b, l // tl),
        in_specs=[
            pl.BlockSpec((None, tl, d), lambda bi, i: (bi, i, 0)),
            pl.BlockSpec((None, 1, tl), lambda bi, i: (bi, 0, i)),
            pl.BlockSpec((1, d), const),
            pl.BlockSpec(w_low.shape, const),
            pl.BlockSpec(w_u.shape, const),
            pl.BlockSpec(w_gate.shape, const),
            pl.BlockSpec((1, Q_LORA_RANK), const),
            pl.BlockSpec((Q_LORA_RANK, N_HEADS * HEAD_PAD), const),
            pl.BlockSpec((1, KV_LORA_RANK), const),
            pl.BlockSpec((KV_LORA_RANK, N_HEADS * HEAD_PAD), const),
            pl.BlockSpec((N_HEADS * V_HEAD_DIM, KV_LORA_RANK), const),
            pl.BlockSpec((1, 2 * D_MODEL), const),
        ],
        out_specs=[
            pl.BlockSpec((None, tl, N_HEADS * HEAD_PAD), lambda bi, i: (bi, i, 0)),
            pl.BlockSpec((None, tl, N_HEADS * HEAD_PAD), lambda bi, i: (bi, i, 0)),
            pl.BlockSpec((None, tl // ATTN_KEY_BLOCK, N_HEADS * V_HEAD_DIM, ATTN_KEY_BLOCK),
                         lambda bi, i: (bi, i, 0, 0)),
            pl.BlockSpec((tl, SSM_WIDTH), lambda bi, i: (i, bi)),
            pl.BlockSpec((None, tl, 2 * D_MODEL), lambda bi, i: (bi, i, 0)),
        ],
        out_shape=[
            jax.ShapeDtypeStruct((b, l, N_HEADS * HEAD_PAD), BF16),
            jax.ShapeDtypeStruct((b, l, N_HEADS * HEAD_PAD), BF16),
            jax.ShapeDtypeStruct((b, l // ATTN_KEY_BLOCK, N_HEADS * V_HEAD_DIM, ATTN_KEY_BLOCK), BF16),
            jax.ShapeDtypeStruct((l, b * SSM_WIDTH), BF16),
            jax.ShapeDtypeStruct((b, l, 2 * D_MODEL), BF16),
        ],
        compiler_params=pltpu.CompilerParams(
            dimension_semantics=("parallel", "parallel"), vmem_limit_bytes=VMEM_LIMIT),
        name="in_proj",
    )(x, pos_f, g_pre, w_low, w_u, w_gate, q_norm, w_uq_p, kv_norm, w_uk_p, w_uv, b_gate)


def _attn_kernel(q_ref, k_ref, vt_ref, o_ref, s_ref, mx_ref, m_ref, acc_ref, *, tq, tk):
    i = pl.program_id(2)
    n_q = pl.num_programs(2)
    key = lax.broadcasted_iota(jnp.int32, (tk, tq), 0)
    qry = lax.broadcasted_iota(jnp.int32, (tk, tq), 1)
    diag_mask = key <= qry

    def scores(j, slot, mask=None, tile=None, cols=slice(None)):
        r0 = pl.multiple_of(j * tk, tk)
        nq = tq if cols.start is None else cols.stop - cols.start
        q0 = pl.multiple_of((i if tile is None else tile) * tq + (cols.start or 0), nq)
        for hh in range(2):
            lanes = slice(hh * HEAD_PAD, (hh + 1) * HEAD_PAD)
            kj = k_ref[pl.ds(r0, tk), lanes]
            qt = q_ref[pl.ds(q0, nq), lanes]
            s = lax.dot_general(kj, qt, (((1,), (1,)), ((), ())), preferred_element_type=F32)
            if mask is not None:
                s = jnp.where(mask, s, NEG_BIG)
            s_ref[slot, hh, :, cols] = s
            mx_ref[slot, hh, :, cols] = jnp.max(s, axis=0, keepdims=True)

    vrow = lax.broadcasted_iota(jnp.int32, (2 * V_HEAD_DIM, tk), 0)
    one = jnp.ones((), BF16)

    def softmax_pv(j, slot, cols=slice(None)):
        vt = vt_ref[j]
        for hh in range(2):
            m = m_ref[hh, :, cols]
            m_new = jnp.maximum(m, mx_ref[slot, hh, :, cols])
            alpha = jnp.exp2(m - m_new)
            p = jnp.exp2(s_ref[slot, hh, :, cols] - m_new)
            vt_h = jnp.where(vrow == SUM_ROW[hh], one, vt)
            acc_ref[hh, :, cols] = alpha * acc_ref[hh, :, cols] + _dot(vt_h, p.astype(BF16))
            m_ref[hh, :, cols] = m_new

    m_ref[...] = jnp.full(m_ref.shape, NEG_BIG, F32)
    acc_ref[...] = jnp.zeros(acc_ref.shape, F32)

    def two_blocks(t, last=False):
        scores(2 * t + 1, 1)
        softmax_pv(2 * t, 0)
        scores(2 * t + 2, 0, diag_mask if last else None)
        softmax_pv(2 * t + 1, 1)

    def four_blocks(t, carry):
        two_blocks(2 * t)
        two_blocks(2 * t + 1)
        return carry

    n_pairs = jnp.maximum(i - 1, 0)
    lax.fori_loop(0, n_pairs // 2, four_blocks, 0)

    @pl.when(n_pairs % 2 == 1)
    def _():
        two_blocks(n_pairs - 1)

    def diagonal_tile():
        late = slice(tk, tq)
        scores(2 * i + 1, 1, diag_mask[:, :tq - tk], cols=late)
        softmax_pv(2 * i, 0)
        scores(0, 0, tile=jnp.minimum(i + 1, n_q - 1))
        softmax_pv(2 * i + 1, 1, cols=late)

    @pl.when(i > 0)
    def _():
        two_blocks(i - 1, last=True)
        diagonal_tile()

    @pl.when(i == 0)
    def _():
        scores(0, 0, diag_mask)
        diagonal_tile()

    a0 = acc_ref[0]
    a1 = acc_ref[1]
    o0 = a0[:V_HEAD_DIM] / a0[SUM_ROW[0]:SUM_ROW[0] + 1]
    o1 = a1[V_HEAD_DIM:] / a1[SUM_ROW[1]:SUM_ROW[1] + 1]
    out_t = jnp.concatenate([o0, o1], axis=0)
    o_ref[...] = out_t.T.astype(BF16)


def _attention(q_all, k_all, v_t):
    b, l, _ = q_all.shape
    tq, tk = ATTN_TILE, ATTN_KEY_BLOCK
    return pl.pallas_call(
        functools.partial(_attn_kernel, tq=tq, tk=tk),
        grid=(b, N_PAIRS, l // tq),
        in_specs=[
            pl.BlockSpec((None, l, 2 * HEAD_PAD), lambda bi, p, i: (bi, 0, p)),
            pl.BlockSpec((None, l, 2 * HEAD_PAD), lambda bi, p, i: (bi, 0, p)),
            pl.BlockSpec((None, l // tk, 2 * V_HEAD_DIM, tk), lambda bi, p, i: (bi, 0, p, 0)),
        ],
        out_specs=pl.BlockSpec((None, tq, 2 * V_HEAD_DIM), lambda bi, p, i: (bi, i, p)),
        out_shape=jax.ShapeDtypeStruct((b, l, N_HEADS * V_HEAD_DIM), BF16),
        scratch_shapes=[
            pltpu.VMEM((2, 2, tk, tq), F32),
            pltpu.VMEM((2, 2, 1, tq), F32),
            pltpu.VMEM((2, 1, tq), F32),
            pltpu.VMEM((2, 2 * V_HEAD_DIM, tq), F32),
        ],
        compiler_params=pltpu.CompilerParams(
            dimension_semantics=("parallel", "parallel", "arbitrary"), vmem_limit_bytes=VMEM_LIMIT),
        name="attention",
    )(q_all, k_all, v_t)


def _s5_kernel(u_ref, are_ref, aim_ref, wb_ref, wcr_ref, wci_ref, d_ref, wglu_ref, bglu_ref,
               o_ref, state_ref, sre_ref, sim_ref, tb_ref, *, steps, batch):
    @pl.when(pl.program_id(0) == 0)
    def _():
        state_ref[...] = jnp.zeros_like(state_ref)

    lane_groups = SSM_WIDTH // LANES
    for b in range(batch):
        for c in range(lane_groups):
            c0 = b * SSM_WIDTH + c * LANES
            tb_ref[c, pl.ds(b, steps, stride=batch), :] = u_ref[:, c0:c0 + LANES].astype(F32)
    u32 = jnp.concatenate([tb_ref[c] for c in range(lane_groups)], axis=1)
    u = u32.astype(BF16)
    for k in range(SSM_HALVES):
        bu = _dot(u[:, k * SSM_HALF_IN:(k + 1) * SSM_HALF_IN], wb_ref[k])
        sre_ref[k] = bu[:, :SSM_HALF_STATE]
        sim_ref[k] = bu[:, SSM_HALF_STATE:]

    nt = (((1,), (1,)), ((), ()))
    ys = []
    for k in range(SSM_HALVES):
        ar = are_ref[:, k * SSM_HALF_STATE:(k + 1) * SSM_HALF_STATE]
        ai = aim_ref[:, k * SSM_HALF_STATE:(k + 1) * SSM_HALF_STATE]
        sr = state_ref[k, 0]
        si = state_ref[k, 1]
        for t in range(steps):
            rs = slice(t * batch, (t + 1) * batch)
            nr = ar * sr - ai * si + sre_ref[k, rs, :]
            ni = ar * si + ai * sr + sim_ref[k, rs, :]
            sre_ref[k, rs, :] = nr
            sim_ref[k, rs, :] = ni
            sr, si = nr, ni
        state_ref[k, 0] = sr
        state_ref[k, 1] = si
        ys.append(lax.dot_general(sre_ref[k].astype(BF16), wcr_ref[k], nt, preferred_element_type=F32)
                  - lax.dot_general(sim_ref[k].astype(BF16), wci_ref[k], nt, preferred_element_type=F32))

    y = jnp.concatenate(ys, axis=1) + d_ref[...] * u32
    g = _gelu_tanh(y)
    z = _dot(g.astype(BF16), wglu_ref[...]) + bglu_ref[...]
    out = g * jax.nn.sigmoid(z)
    for c in range(lane_groups):
        tb_ref[c] = out[:, c * LANES:(c + 1) * LANES]
    for b in range(batch):
        for c in range(lane_groups):
            c0 = b * SSM_WIDTH + c * LANES
            o_ref[:, c0:c0 + LANES] = tb_ref[c, pl.ds(b, steps, stride=batch), :].astype(BF16)


def _s5(u_tm, a_re_t, a_im_t, wb, wcr, wci, d_skip, w_glu, b_glu, batch):
    length = u_tm.shape[0]
    width = SSM_WIDTH
    steps = SSM_CHUNK
    rows = steps * batch
    const2 = lambda i: (0, 0)
    const3 = lambda i: (0, 0, 0)
    return pl.pallas_call(
        functools.partial(_s5_kernel, steps=steps, batch=batch),
        grid=(length // steps,),
        in_specs=[
            pl.BlockSpec((steps, batch * width), lambda i: (i, 0)),
            pl.BlockSpec(a_re_t.shape, const2),
            pl.BlockSpec(a_im_t.shape, const2),
            pl.BlockSpec(wb.shape, const3),
            pl.BlockSpec(wcr.shape, const3),
            pl.BlockSpec(wci.shape, const3),
            pl.BlockSpec((1, width), const2),
            pl.BlockSpec((width, width), const2),
            pl.BlockSpec((1, width), const2),
        ],
        out_specs=pl.BlockSpec((steps, batch * width), lambda i: (i, 0)),
        out_shape=jax.ShapeDtypeStruct((length, batch * width), BF16),
        scratch_shapes=[
            pltpu.VMEM((SSM_HALVES, 2, batch, SSM_HALF_STATE), F32),
            pltpu.VMEM((SSM_HALVES, rows, SSM_HALF_STATE), F32),
            pltpu.VMEM((SSM_HALVES, rows, SSM_HALF_STATE), F32),
            pltpu.VMEM((width // LANES, rows, LANES), F32),
        ],
        compiler_params=pltpu.CompilerParams(
            dimension_semantics=("arbitrary",), vmem_limit_bytes=VMEM_LIMIT),
        name="s5",
    )(u_tm, a_re_t, a_im_t, wb, wcr, wci, d_skip, w_glu, b_glu)


def _merge_kernel(x_ref, attn_ref, ssm_ref, gate_ref, wba_ref, wbs_ref, wout_ref, gpost_ref, o_ref):
    ga = gate_ref[:, :D_MODEL].astype(F32)
    gs = gate_ref[:, D_MODEL:].astype(F32)
    merged = ga * _dot(attn_ref[...], wba_ref[...]) + gs * _dot(ssm_ref[...], wbs_ref[...])
    m2 = _dot(merged.astype(BF16), wout_ref[...])
    o_ref[...] = x_ref[...] + _rms(m2, gpost_ref[...])


def _merge(x, attn, ssm_tm, gates, w_ba, w_bs, w_out, g_post):
    b, l, d = x.shape
    tl = ROW_TILE
    const = lambda *_: (0, 0)
    return pl.pallas_call(
        _merge_kernel,
        grid=(b, l // tl),
        in_specs=[
            pl.BlockSpec((None, tl, d), lambda bi, i: (bi, i, 0)),
            pl.BlockSpec((None, tl, N_HEADS * V_HEAD_DIM), lambda bi, i: (bi, i, 0)),
            pl.BlockSpec((tl, SSM_WIDTH), lambda bi, i: (i, bi)),
            pl.BlockSpec((None, tl, 2 * d), lambda bi, i: (bi, i, 0)),
            pl.BlockSpec(w_ba.shape, const),
            pl.BlockSpec(w_bs.shape, const),
            pl.BlockSpec(w_out.shape, const),
            pl.BlockSpec((1, d), const),
        ],
        out_specs=pl.BlockSpec((None, tl, d), lambda bi, i: (bi, i, 0)),
        out_shape=jax.ShapeDtypeStruct((b, l, d), F32),
        compiler_params=pltpu.CompilerParams(
            dimension_semantics=("parallel", "parallel"), vmem_limit_bytes=VMEM_LIMIT),
        name="merge",
    )(x, attn, ssm_tm, gates, w_ba, w_bs, w_out, g_post)


def _ffn_kernel(x_ref, gpre_ref, wup_ref, cw_ref, cb_ref, wdown_ref, gpost_ref, o_ref,
                act_ref, perm_ref, tail_ref):
    i = pl.program_id(1)
    tl, d = x_ref.shape
    n = tl // FFN_PHASES
    lane_groups = d // LANES

    @pl.when(i == 0)
    def _():
        tail_ref[...] = jnp.zeros(tail_ref.shape, F32)

    for c in range(lane_groups):
        perm_ref[c] = x_ref[:, c * LANES:(c + 1) * LANES]
    x1 = jnp.concatenate(
        [jnp.concatenate([perm_ref[c, pl.ds(b, n, stride=FFN_PHASES), :] for b in range(FFN_PHASES)], axis=0)
         for c in range(lane_groups)], axis=1)
    hn = _rms(x1, gpre_ref[...]).astype(BF16)

    first_row = lax.broadcasted_iota(jnp.int32, (n, 1), 0) == 0

    def conv(h, c0):
        cols = slice(c0, c0 + FFN_COL_CHUNK)
        blocks = [h[b * n:(b + 1) * n] for b in range(FFN_PHASES)]

        def moved_down(block, prev_token):
            return jnp.where(first_row, prev_token, pltpu.roll(block, 1, 0))

        s1 = moved_down(blocks[FFN_PHASES - 1], tail_ref[1:2, cols])
        s2 = moved_down(blocks[FFN_PHASES - 2], tail_ref[0:1, cols])
        tail_ref[0:1, cols] = blocks[FFN_PHASES - 2][n - 1:n]
        tail_ref[1:2, cols] = blocks[FFN_PHASES - 1][n - 1:n]
        tap1 = [s1] + blocks[:-1]
        tap2 = [s2, s1] + blocks[:-2]
        w0, w1, w2, bias = cw_ref[0:1, cols], cw_ref[1:2, cols], cw_ref[2:3, cols], cb_ref[:, cols]
        return jnp.concatenate(
            [bias + w2 * blocks[b] + w1 * tap1[b] + w0 * tap2[b] for b in range(FFN_PHASES)], axis=0)

    for j in range(D_FF // FFN_COL_CHUNK):
        c0 = j * FFN_COL_CHUNK
        hg = _dot(hn, wup_ref[:, c0:c0 + FFN_COL_CHUNK])
        hv = _dot(hn, wup_ref[:, D_FF + c0:D_FF + c0 + FFN_COL_CHUNK])
        act = _gelu_tanh(conv(hg, c0)) * conv(hv, D_FF + c0)
        act_ref[:, c0:c0 + FFN_COL_CHUNK] = act.astype(BF16)

    ff = _dot(act_ref[...], wdown_ref[...])
    out = x1 + _rms(ff, gpost_ref[...])

    for c in range(lane_groups):
        for b in range(FFN_PHASES):
            perm_ref[c, pl.ds(b, n, stride=FFN_PHASES), :] = out[b * n:(b + 1) * n, c * LANES:(c + 1) * LANES]
    for c in range(lane_groups):
        o_ref[:, c * LANES:(c + 1) * LANES] = perm_ref[c]


def _conv_ffn(x1, g_pre, w_up, conv_w, conv_b, w_down, g_post):
    b, l, d = x1.shape
    tl = ROW_TILE
    const = lambda *_: (0, 0)
    return pl.pallas_call(
        _ffn_kernel,
        grid=(b, l // tl),
        in_specs=[
            pl.BlockSpec((None, tl, d), lambda bi, i: (bi, i, 0)),
            pl.BlockSpec((1, d), const),
            pl.BlockSpec(w_up.shape, const, pipeline_mode=pl.Buffered(1)),
            pl.BlockSpec(conv_w.shape, const),
            pl.BlockSpec(conv_b.shape, const),
            pl.BlockSpec(w_down.shape, const, pipeline_mode=pl.Buffered(1)),
            pl.BlockSpec((1, d), const),
        ],
        out_specs=pl.BlockSpec((None, tl, d), lambda bi, i: (bi, i, 0)),
        out_shape=jax.ShapeDtypeStruct((b, l, d), F32),
        scratch_shapes=[pltpu.VMEM((tl, D_FF), BF16),
                        pltpu.VMEM((d // LANES, tl, LANES), F32),
                        pltpu.VMEM((CONV_WIDTH - 1, 2 * D_FF), F32)],
        compiler_params=pltpu.CompilerParams(
            dimension_semantics=("parallel", "arbitrary"), vmem_limit_bytes=VMEM_LIMIT),
        name="conv_ffn",
    )(x1, g_pre, w_up, conv_w, conv_b, w_down, g_post)


def _head_lane_groups(nope, rope):
    k = (nope if nope is not None else rope).shape[0]
    nope = jnp.zeros((k, N_HEADS, QK_NOPE_DIM), BF16) if nope is None else nope
    rope = jnp.zeros((k, N_HEADS, QK_ROPE_DIM), BF16) if rope is None else rope
    return jnp.concatenate([nope, rope, rope], axis=2).astype(BF16).reshape(k, -1)


def _layer(x, pos_f, p):
    b, l, d = x.shape
    row = lambda v: v.reshape(1, -1)

    w_in = p["w_in"]
    off_u = P_KR + QK_ROPE_DIM
    off_gate = off_u + SSM_WIDTH
    w_kr = w_in[:, P_KR:off_u]
    w_low = jnp.concatenate([w_in[:, :P_KR], jnp.zeros((d, QK_NOPE_DIM), w_in.dtype), w_kr, w_kr],
                            axis=1).astype(BF16)
    w_u = w_in[:, off_u:off_gate].astype(BF16)
    w_gate = w_in[:, off_gate:].astype(BF16)
    w_uq = p["w_uq"].reshape(Q_LORA_RANK, N_HEADS, QK_HEAD_DIM)
    w_uq_p = _head_lane_groups(w_uq[:, :, :QK_NOPE_DIM], w_uq[:, :, QK_NOPE_DIM:])
    w_uk_p = _head_lane_groups(p["w_uk"].reshape(KV_LORA_RANK, N_HEADS, QK_NOPE_DIM), None)

    q_all, k_all, v_t, u_tm, gates = _in_proj(
        x, pos_f, row(p["mix_norm_pre"]), w_low, w_u, w_gate, row(p["q_norm"]), w_uq_p, row(p["kv_norm"]),
        w_uk_p, p["w_uv"].T.astype(BF16), row(p["b_gate"]))

    attn = _attention(q_all, k_all, v_t)

    a_re_t, a_im_t, wb, wcr, wci = _s5_prep(
        p["ssm_lambda_re"], p["ssm_lambda_im"], p["ssm_log_dt"], p["ssm_b_re"], p["ssm_b_im"],
        p["ssm_c_re"], p["ssm_c_im"], b)
    ssm_tm = _s5(u_tm, a_re_t, a_im_t, wb, wcr, wci,
                 row(p["ssm_d"]), p["w_glu"].astype(BF16), row(p["b_glu"]), b)

    x1 = _merge(x, attn, ssm_tm, gates, p["w_branch_attn"].astype(BF16), p["w_branch_ssm"].astype(BF16),
                p["w_out"].astype(BF16), row(p["mix_norm_post"]))
    return _conv_ffn(x1, row(p["ffn_norm_pre"]), p["w_up"].astype(BF16), p["conv_w"], row(p["conv_b"]),
                     p["w_down"].astype(BF16), row(p["ffn_norm_post"]))


def kernel(x, positions, mix_norm_pre, w_in, q_norm, w_uq, kv_norm, w_uk, w_uv, ssm_lambda_re, ssm_lambda_im, ssm_log_dt, ssm_b_re, ssm_b_im, ssm_c_re, ssm_c_im, ssm_d, w_glu, b_glu, w_branch_attn, w_branch_ssm, b_gate, w_out, mix_norm_post, ffn_norm_pre, w_up, conv_w, conv_b, w_down, ffn_norm_post):
    b, l, _ = x.shape
    params = dict(mix_norm_pre=mix_norm_pre, w_in=w_in, q_norm=q_norm, w_uq=w_uq, kv_norm=kv_norm,
                  w_uk=w_uk, w_uv=w_uv, ssm_lambda_re=ssm_lambda_re, ssm_lambda_im=ssm_lambda_im,
                  ssm_log_dt=ssm_log_dt, ssm_b_re=ssm_b_re, ssm_b_im=ssm_b_im, ssm_c_re=ssm_c_re,
                  ssm_c_im=ssm_c_im, ssm_d=ssm_d, w_glu=w_glu, b_glu=b_glu, w_branch_attn=w_branch_attn,
                  w_branch_ssm=w_branch_ssm, b_gate=b_gate, w_out=w_out, mix_norm_post=mix_norm_post,
                  ffn_norm_pre=ffn_norm_pre, w_up=w_up, conv_w=conv_w, conv_b=conv_b, w_down=w_down,
                  ffn_norm_post=ffn_norm_post)
    pos_f = positions.astype(F32).reshape(b, 1, l)
    for layer in range(mix_norm_pre.shape[0]):
        x = _layer(x, pos_f, {k: v[layer] for k, v in params.items()})
    return x
```

```python
import functools
import math

import jax
import jax.numpy as jnp
from jax import lax
from jax.experimental import pallas as pl
from jax.experimental.pallas import tpu as pltpu

D_MODEL = 1024
N_HEADS = 8
QK_NOPE_DIM = 64
QK_ROPE_DIM = 32
QK_HEAD_DIM = QK_NOPE_DIM + QK_ROPE_DIM
V_HEAD_DIM = 64
Q_LORA_RANK = 384
KV_LORA_RANK = 256
ROPE_THETA = 10000.0
SSM_WIDTH = 512
SSM_GROUP = 16
SSM_GROUPS = SSM_WIDTH // SSM_GROUP
SSM_STATE = 64
D_FF = 2816
CONV_WIDTH = 3
EPS = 1e-6

LANES = 128
HEAD_PAD = LANES
ROPE_LANE0 = QK_NOPE_DIM
ROPE_HALF = QK_ROPE_DIM // 2
N_PAIRS = N_HEADS // 2
SUM_ROW = (V_HEAD_DIM, 0)
SSM_HALVES = 2
SSM_HALF_IN = SSM_WIDTH // SSM_HALVES
SSM_HALF_STATE = SSM_GROUPS * SSM_STATE // SSM_HALVES

P_CQ = 0
P_CKV = P_CQ + Q_LORA_RANK
P_KR = P_CKV + KV_LORA_RANK
P_U = P_KR + LANES

Q_SCALE = (1.0 / math.sqrt(QK_HEAD_DIM)) * math.log2(math.e)
NEG_BIG = -1e30

ROW_TILE = 512
ATTN_TILE = ROW_TILE
ATTN_KEY_BLOCK = ATTN_TILE // 2
SSM_CHUNK = 128
FFN_COL_CHUNK = 256
FFN_PHASES = 8
VMEM_LIMIT = 56 * 1024 * 1024

BF16 = jnp.bfloat16
F32 = jnp.float32


def _rms(x, g):
    return x * lax.rsqrt(jnp.mean(x * x, axis=-1, keepdims=True) + EPS) * g


def _dot(a, b):
    return jnp.dot(a, b, preferred_element_type=F32)


GELU_C = math.sqrt(2.0 / math.pi)


def _gelu_tanh(x):
    k = -2.0 * GELU_C * math.log2(math.e)
    e = jnp.exp2(x * (x * x * (k * 0.044715) + k))
    return x / (1.0 + e)


def _s5_prep_kernel(lr_ref, li_ref, logdt_ref, br_ref, bi_ref, cr_ref, ci_ref,
                    are_ref, aim_ref, wb_ref, wcr_ref, wci_ref, *, batch):
    lr = lr_ref[...]
    li = li_ref[...]
    dt = jnp.exp(logdt_ref[...])
    mag = jnp.exp(lr * dt)
    ang = li * dt
    a_re = mag * jnp.cos(ang)
    a_im = mag * jnp.sin(ang)
    den = lr * lr + li * li
    n_re = a_re - 1.0
    n_im = a_im
    z_re = (n_re * lr + n_im * li) / den
    z_im = (n_im * lr - n_re * li) / den
    are_ref[...] = jnp.broadcast_to(a_re, (batch, a_re.shape[1]))
    aim_ref[...] = jnp.broadcast_to(a_im, (batch, a_im.shape[1]))
    br = br_ref[...]
    bi = bi_ref[...]
    bb_re = z_re * br - z_im * bi
    bb_im = z_re * bi + z_im * br

    groups_per_half = SSM_GROUPS // SSM_HALVES
    row_group = jnp.right_shift(lax.broadcasted_iota(jnp.int32, (SSM_HALF_IN, SSM_HALF_STATE), 0),
                                SSM_GROUP.bit_length() - 1)
    col_group = jnp.right_shift(lax.broadcasted_iota(jnp.int32, (SSM_HALF_IN, SSM_HALF_STATE), 1),
                                SSM_STATE.bit_length() - 1)
    on_diag = row_group == col_group

    def diag(v):
        return jnp.where(on_diag, jnp.concatenate([v] * groups_per_half, axis=0), 0.0)

    for k in range(SSM_HALVES):
        cols = slice(k * SSM_HALF_STATE, (k + 1) * SSM_HALF_STATE)
        wb_ref[k, :, :SSM_HALF_STATE] = diag(bb_re[:, cols]).astype(BF16)
        wb_ref[k, :, SSM_HALF_STATE:] = diag(bb_im[:, cols]).astype(BF16)
        wcr_ref[k] = diag(cr_ref[:, cols]).astype(BF16)
        wci_ref[k] = diag(ci_ref[:, cols]).astype(BF16)


def _s5_prep(lam_re, lam_im, log_dt, b_re, b_im, c_re, c_im, batch):
    n = SSM_GROUPS * SSM_STATE
    lanes = lambda v: v.reshape(1, n)
    chan_major = lambda v, perm: v.transpose(perm).reshape(SSM_GROUP, n)
    return pl.pallas_call(
        functools.partial(_s5_prep_kernel, batch=batch),
        out_shape=(jax.ShapeDtypeStruct((batch, n), F32), jax.ShapeDtypeStruct((batch, n), F32),
                   jax.ShapeDtypeStruct((SSM_HALVES, SSM_HALF_IN, 2 * SSM_HALF_STATE), BF16),
                   jax.ShapeDtypeStruct((SSM_HALVES, SSM_HALF_IN, SSM_HALF_STATE), BF16),
                   jax.ShapeDtypeStruct((SSM_HALVES, SSM_HALF_IN, SSM_HALF_STATE), BF16)),
        name="s5_prep",
    )(lanes(lam_re), lanes(lam_im), lanes(jnp.repeat(log_dt, SSM_STATE)),
      chan_major(b_re, (2, 0, 1)), chan_major(b_im, (2, 0, 1)),
      chan_major(c_re, (1, 0, 2)), chan_major(c_im, (1, 0, 2)))


def _in_proj_kernel(x_ref, pos_ref, gpre_ref, wlow_ref, wu_ref, wgate_ref, qn_ref, wuq_ref, kvn_ref, wuk_ref,
                    wuv_ref, bg_ref, q_ref, k_ref, vt_ref, u_ref, gate_ref):
    x = x_ref[...]
    tl = x.shape[0]
    hn = _rms(x, gpre_ref[...]).astype(BF16)

    freq_idx = lax.broadcasted_iota(jnp.int32, (ROPE_HALF, 1), 0).astype(F32)
    inv_freq = jnp.exp(freq_idx * (-2.0 * math.log(ROPE_THETA) / QK_ROPE_DIM))
    ang = inv_freq * pos_ref[...]
    c16 = jnp.cos(ang)
    s16 = jnp.sin(ang)
    ones = jnp.ones((ROPE_LANE0, tl), F32)
    zeros = lambda n: jnp.zeros((n, tl), F32)
    cos = jnp.concatenate([ones, zeros(ROPE_HALF), c16, c16, zeros(ROPE_HALF)], axis=0).T
    sin = jnp.concatenate([zeros(ROPE_LANE0 + ROPE_HALF), s16, -s16, zeros(ROPE_HALF)], axis=0).T
    cos_q = cos * Q_SCALE
    sin_q = sin * Q_SCALE

    low = _dot(hn, wlow_ref[...])
    cq = low[:, P_CQ:P_CKV]
    ckv = low[:, P_CKV:P_KR]
    kr = low[:, P_KR:P_U]
    cqn = _rms(cq, qn_ref[...]).astype(BF16)
    q = _dot(cqn, wuq_ref[...])
    for h in range(N_HEADS):
        sl = slice(h * HEAD_PAD, (h + 1) * HEAD_PAD)
        t = q[:, sl]
        q_ref[:, sl] = (t * cos_q + pltpu.roll(t, ROPE_HALF, 1) * sin_q).astype(BF16)

    ckvn = _rms(ckv, kvn_ref[...]).astype(BF16)
    v_t = lax.dot_general(wuv_ref[...], ckvn, (((1,), (1,)), ((), ())),
                          preferred_element_type=F32).astype(BF16)
    for c in range(vt_ref.shape[0]):
        vt_ref[c] = v_t[:, c * ATTN_KEY_BLOCK:(c + 1) * ATTN_KEY_BLOCK]
    kn = _dot(ckvn, wuk_ref[...])
    kr = kr * cos + pltpu.roll(kr, ROPE_HALF, 1) * sin
    for h in range(N_HEADS):
        sl = slice(h * HEAD_PAD, (h + 1) * HEAD_PAD)
        k_ref[:, sl] = (kn[:, sl] + kr).astype(BF16)

    u_ref[...] = _dot(hn, wu_ref[...]).astype(BF16)
    logits = _dot(hn, wgate_ref[...]) + bg_ref[...]
    gate_ref[...] = jax.nn.sigmoid(logits).astype(BF16)


def _in_proj(x, pos_f, g_pre, w_low, w_u, w_gate, q_norm, w_uq_p, kv_norm, w_uk_p, w_uv, b_gate):
    b, l, d = x.shape
    tl = ROW_TILE
    const = lambda *_: (0, 0)
    return pl.pallas_call(
        _in_proj_kernel,
        grid=(b, l // tl),
        in_specs=[
            pl.BlockSpec((None, tl, d), lambda bi, i: (bi, i, 0)),
            pl.BlockSpec((None, 1, tl), lambda bi, i: (bi, 0, i)),
            pl.BlockSpec((1, d), const),
            pl.BlockSpec(w_low.shape, const),
            pl.BlockSpec(w_u.shape, const),
            pl.BlockSpec(w_gate.shape, const),
            pl.BlockSpec((1, Q_LORA_RANK), const),
            pl.BlockSpec((Q_LORA_RANK, N_HEADS * HEAD_PAD), const),
            pl.BlockSpec((1, KV_LORA_RANK), const),
            pl.BlockSpec((KV_LORA_RANK, N_HEADS * HEAD_PAD), const),
            pl.BlockSpec((N_HEADS * V_HEAD_DIM, KV_LORA_RANK), const),
            pl.BlockSpec((1, 2 * D_MODEL), const),
        ],
        out_specs=[
            pl.BlockSpec((None, tl, N_HEADS * HEAD_PAD), lambda bi, i: (bi, i, 0)),
            pl.BlockSpec((None, tl, N_HEADS * HEAD_PAD), lambda bi, i: (bi, i, 0)),
            pl.BlockSpec((None, tl // ATTN_KEY_BLOCK, N_HEADS * V_HEAD_DIM, ATTN_KEY_BLOCK),
                         lambda bi, i: (bi, i, 0, 0)),
            pl.BlockSpec((tl, SSM_WIDTH), lambda bi, i: (i, bi)),
            pl.BlockSpec((None, tl, 2 * D_MODEL), lambda bi, i: (bi, i, 0)),
        ],
        out_shape=[
            jax.ShapeDtypeStruct((b, l, N_HEADS * HEAD_PAD), BF16),
            jax.ShapeDtypeStruct((b, l, N_HEADS * HEAD_PAD), BF16),
            jax.ShapeDtypeStruct((b, l // ATTN_KEY_BLOCK, N_HEADS * V_HEAD_DIM, ATTN_KEY_BLOCK), BF16),
            jax.ShapeDtypeStruct((l, b * SSM_WIDTH), BF16),
            jax.ShapeDtypeStruct((b, l, 2 * D_MODEL), BF16),
        ],
        compiler_params=pltpu.CompilerParams(
            dimension_semantics=("parallel", "parallel"), vmem_limit_bytes=VMEM_LIMIT),
        name="in_proj",
    )(x, pos_f, g_pre, w_low, w_u, w_gate, q_norm, w_uq_p, kv_norm, w_uk_p, w_uv, b_gate)


def _attn_kernel(q_ref, k_ref, vt_ref, o_ref, s_ref, mx_ref, m_ref, acc_ref, *, tq, tk):
    n_q = q_ref.shape[0] // tq
    key = lax.broadcasted_iota(jnp.int32, (tk, tq), 0)
    qry = lax.broadcasted_iota(jnp.int32, (tk, tq), 1)
    diag_mask = key <= qry
    vrow = lax.broadcasted_iota(jnp.int32, (2 * V_HEAD_DIM, tk), 0)
    one = jnp.ones((), BF16)

    def query_tile(i, carry):
        def scores(j, slot, mask=None, tile=None, cols=slice(None)):
            r0 = pl.multiple_of(j * tk, tk)
            nq = tq if cols.start is None else cols.stop - cols.start
            q0 = pl.multiple_of((i if tile is None else tile) * tq + (cols.start or 0), nq)
            for hh in range(2):
                lanes = slice(hh * HEAD_PAD, (hh + 1) * HEAD_PAD)
                kj = k_ref[pl.ds(r0, tk), lanes]
                qt = q_ref[pl.ds(q0, nq), lanes]
                s = lax.dot_general(kj, qt, (((1,), (1,)), ((), ())), preferred_element_type=F32)
                if mask is not None:
                    s = jnp.where(mask, s, NEG_BIG)
                s_ref[slot, hh, :, cols] = s
                mx_ref[slot, hh, :, cols] = jnp.max(s, axis=0, keepdims=True)

        def softmax_pv(j, slot, cols=slice(None)):
            vt = vt_ref[j]
            for hh in range(2):
                m = m_ref[hh, :, cols]
                m_new = jnp.maximum(m, mx_ref[slot, hh, :, cols])
                alpha = jnp.exp2(m - m_new)
                p = jnp.exp2(s_ref[slot, hh, :, cols] - m_new)
                vt_h = jnp.where(vrow == SUM_ROW[hh], one, vt)
                acc_ref[hh, :, cols] = alpha * acc_ref[hh, :, cols] + _dot(vt_h, p.astype(BF16))
                m_ref[hh, :, cols] = m_new

        m_ref[...] = jnp.full(m_ref.shape, NEG_BIG, F32)
        acc_ref[...] = jnp.zeros(acc_ref.shape, F32)

        def two_blocks(t, last=False):
            scores(2 * t + 1, 1)
            softmax_pv(2 * t, 0)
            scores(2 * t + 2, 0, diag_mask if last else None)
            softmax_pv(2 * t + 1, 1)

        def four_blocks(t, c):
            two_blocks(2 * t)
            two_blocks(2 * t + 1)
            return c

        n_pairs = jnp.maximum(i - 1, 0)
        lax.fori_loop(0, n_pairs // 2, four_blocks, 0)

        @pl.when(n_pairs % 2 == 1)
        def _():
            two_blocks(n_pairs - 1)

        def diagonal_tile():
            late = slice(tk, tq)
            scores(2 * i + 1, 1, diag_mask[:, :tq - tk], cols=late)
            softmax_pv(2 * i, 0)
            scores(0, 0, tile=jnp.minimum(i + 1, n_q - 1))
            softmax_pv(2 * i + 1, 1, cols=late)

        @pl.when(i > 0)
        def _():
            two_blocks(i - 1, last=True)
            diagonal_tile()

        @pl.when(i == 0)
        def _():
            scores(0, 0, diag_mask)
            diagonal_tile()

        a0 = acc_ref[0]
        a1 = acc_ref[1]
        o0 = a0[:V_HEAD_DIM] / a0[SUM_ROW[0]:SUM_ROW[0] + 1]
        o1 = a1[V_HEAD_DIM:] / a1[SUM_ROW[1]:SUM_ROW[1] + 1]
        out_t = jnp.concatenate([o0, o1], axis=0)
        o_ref[pl.ds(pl.multiple_of(i * tq, tq), tq), :] = out_t.T.astype(BF16)
        return carry

    lax.fori_loop(0, n_q, query_tile, 0)


def _attention(q_all, k_all, v_t):
    b, l, _ = q_all.shape
    tq, tk = ATTN_TILE, ATTN_KEY_BLOCK
    return pl.pallas_call(
        functools.partial(_attn_kernel, tq=tq, tk=tk),
        grid=(b, N_PAIRS),
        in_specs=[
            pl.BlockSpec((None, l, 2 * HEAD_PAD), lambda bi, p: (bi, 0, p)),
            pl.BlockSpec((None, l, 2 * HEAD_PAD), lambda bi, p: (bi, 0, p)),
            pl.BlockSpec((None, l // tk, 2 * V_HEAD_DIM, tk), lambda bi, p: (bi, 0, p, 0)),
        ],
        out_specs=pl.BlockSpec((None, l, 2 * V_HEAD_DIM), lambda bi, p: (bi, 0, p)),
        out_shape=jax.ShapeDtypeStruct((b, l, N_HEADS * V_HEAD_DIM), BF16),
        scratch_shapes=[
            pltpu.VMEM((2, 2, tk, tq), F32),
            pltpu.VMEM((2, 2, 1, tq), F32),
            pltpu.VMEM((2, 1, tq), F32),
            pltpu.VMEM((2, 2 * V_HEAD_DIM, tq), F32),
        ],
        compiler_params=pltpu.CompilerParams(
            dimension_semantics=("parallel", "parallel"), vmem_limit_bytes=VMEM_LIMIT),
        name="attention",
    )(q_all, k_all, v_t)


def _s5_kernel(u_ref, are_ref, aim_ref, wb_ref, wcr_ref, wci_ref, d_ref, wglu_ref, bglu_ref,
               o_ref, state_ref, sre_ref, sim_ref, tb_ref, *, steps, batch):
    @pl.when(pl.program_id(0) == 0)
    def _():
        state_ref[...] = jnp.zeros_like(state_ref)

    lane_groups = SSM_WIDTH // LANES
    for b in range(batch):
        for c in range(lane_groups):
            c0 = b * SSM_WIDTH + c * LANES
            tb_ref[c, pl.ds(b, steps, stride=batch), :] = u_ref[:, c0:c0 + LANES].astype(F32)
    u32 = jnp.concatenate([tb_ref[c] for c in range(lane_groups)], axis=1)
    u = u32.astype(BF16)
    for k in range(SSM_HALVES):
        bu = _dot(u[:, k * SSM_HALF_IN:(k + 1) * SSM_HALF_IN], wb_ref[k])
        sre_ref[k] = bu[:, :SSM_HALF_STATE]
        sim_ref[k] = bu[:, SSM_HALF_STATE:]

    nt = (((1,), (1,)), ((), ()))
    ys = []
    for k in range(SSM_HALVES):
        ar = are_ref[:, k * SSM_HALF_STATE:(k + 1) * SSM_HALF_STATE]
        ai = aim_ref[:, k * SSM_HALF_STATE:(k + 1) * SSM_HALF_STATE]
        sr = state_ref[k, 0]
        si = state_ref[k, 1]
        for t in range(steps):
            rs = slice(t * batch, (t + 1) * batch)
            nr = ar * sr - ai * si + sre_ref[k, rs, :]
            ni = ar * si + ai * sr + sim_ref[k, rs, :]
            sre_ref[k, rs, :] = nr
            sim_ref[k, rs, :] = ni
            sr, si = nr, ni
        state_ref[k, 0] = sr
        state_ref[k, 1] = si
        ys.append(lax.dot_general(sre_ref[k].astype(BF16), wcr_ref[k], nt, preferred_element_type=F32)
                  - lax.dot_general(sim_ref[k].astype(BF16), wci_ref[k], nt, preferred_element_type=F32))

    y = jnp.concatenate(ys, axis=1) + d_ref[...] * u32
    g = _gelu_tanh(y)
    z = _dot(g.astype(BF16), wglu_ref[...]) + bglu_ref[...]
    out = g * jax.nn.sigmoid(z)
    for c in range(lane_groups):
        tb_ref[c] = out[:, c * LANES:(c + 1) * LANES]
    for b in range(batch):
        for c in range(lane_groups):
            c0 = b * SSM_WIDTH + c * LANES
            o_ref[:, c0:c0 + LANES] = tb_ref[c, pl.ds(b, steps, stride=batch), :].astype(BF16)


def _s5(u_tm, a_re_t, a_im_t, wb, wcr, wci, d_skip, w_glu, b_glu, batch):
    length = u_tm.shape[0]
    width = SSM_WIDTH
    steps = SSM_CHUNK
    rows = steps * batch
    const2 = lambda i: (0, 0)
    const3 = lambda i: (0, 0, 0)
    return pl.pallas_call(
        functools.partial(_s5_kernel, steps=steps, batch=batch),
        grid=(length // steps,),
        in_specs=[
            pl.BlockSpec((steps, batch * width), lambda i: (i, 0)),
            pl.BlockSpec(a_re_t.shape, const2),
            pl.BlockSpec(a_im_t.shape, const2),
            pl.BlockSpec(wb.shape, const3),
            pl.BlockSpec(wcr.shape, const3),
            pl.BlockSpec(wci.shape, const3),
            pl.BlockSpec((1, width), const2),
            pl.BlockSpec((width, width), const2),
            pl.BlockSpec((1, width), const2),
        ],
        out_specs=pl.BlockSpec((steps, batch * width), lambda i: (i, 0)),
        out_shape=jax.ShapeDtypeStruct((length, batch * width), BF16),
        scratch_shapes=[
            pltpu.VMEM((SSM_HALVES, 2, batch, SSM_HALF_STATE), F32),
            pltpu.VMEM((SSM_HALVES, rows, SSM_HALF_STATE), F32),
            pltpu.VMEM((SSM_HALVES, rows, SSM_HALF_STATE), F32),
            pltpu.VMEM((width // LANES, rows, LANES), F32),
        ],
        compiler_params=pltpu.CompilerParams(
            dimension_semantics=("arbitrary",), vmem_limit_bytes=VMEM_LIMIT),
        name="s5",
    )(u_tm, a_re_t, a_im_t, wb, wcr, wci, d_skip, w_glu, b_glu)


def _merge_kernel(x_ref, attn_ref, ssm_ref, gate_ref, wba_ref, wbs_ref, wout_ref, gpost_ref, o_ref):
    ga = gate_ref[:, :D_MODEL].astype(F32)
    gs = gate_ref[:, D_MODEL:].astype(F32)
    merged = ga * _dot(attn_ref[...], wba_ref[...]) + gs * _dot(ssm_ref[...], wbs_ref[...])
    m2 = _dot(merged.astype(BF16), wout_ref[...])
    o_ref[...] = x_ref[...] + _rms(m2, gpost_ref[...])


def _merge(x, attn, ssm_tm, gates, w_ba, w_bs, w_out, g_post):
    b, l, d = x.shape
    tl = ROW_TILE
    const = lambda *_: (0, 0)
    return pl.pallas_call(
        _merge_kernel,
        grid=(b, l // tl),
        in_specs=[
            pl.BlockSpec((None, tl, d), lambda bi, i: (bi, i, 0)),
            pl.BlockSpec((None, tl, N_HEADS * V_HEAD_DIM), lambda bi, i: (bi, i, 0)),
            pl.BlockSpec((tl, SSM_WIDTH), lambda bi, i: (i, bi)),
            pl.BlockSpec((None, tl, 2 * d), lambda bi, i: (bi, i, 0)),
            pl.BlockSpec(w_ba.shape, const),
            pl.BlockSpec(w_bs.shape, const),
            pl.BlockSpec(w_out.shape, const),
            pl.BlockSpec((1, d), const),
        ],
        out_specs=pl.BlockSpec((None, tl, d), lambda bi, i: (bi, i, 0)),
        out_shape=jax.ShapeDtypeStruct((b, l, d), F32),
        compiler_params=pltpu.CompilerParams(
            dimension_semantics=("parallel", "parallel"), vmem_limit_bytes=VMEM_LIMIT),
        name="merge",
    )(x, attn, ssm_tm, gates, w_ba, w_bs, w_out, g_post)


def _ffn_kernel(x_ref, gpre_ref, wup_ref, cw_ref, cb_ref, wdown_ref, gpost_ref, o_ref,
                act_ref, perm_ref, tail_ref):
    i = pl.program_id(1)
    tl, d = x_ref.shape
    n = tl // FFN_PHASES
    lane_groups = d // LANES

    @pl.when(i == 0)
    def _():
        tail_ref[...] = jnp.zeros(tail_ref.shape, F32)

    for c in range(lane_groups):
        perm_ref[c] = x_ref[:, c * LANES:(c + 1) * LANES]
    x1 = jnp.concatenate(
        [jnp.concatenate([perm_ref[c, pl.ds(b, n, stride=FFN_PHASES), :] for b in range(FFN_PHASES)], axis=0)
         for c in range(lane_groups)], axis=1)
    hn = _rms(x1, gpre_ref[...]).astype(BF16)

    first_row = lax.broadcasted_iota(jnp.int32, (n, 1), 0) == 0

    def conv(h, c0):
        cols = slice(c0, c0 + FFN_COL_CHUNK)
        blocks = [h[b * n:(b + 1) * n] for b in range(FFN_PHASES)]

        def moved_down(block, prev_token):
            return jnp.where(first_row, prev_token, pltpu.roll(block, 1, 0))

        s1 = moved_down(blocks[FFN_PHASES - 1], tail_ref[1:2, cols])
        s2 = moved_down(blocks[FFN_PHASES - 2], tail_ref[0:1, cols])
        tail_ref[0:1, cols] = blocks[FFN_PHASES - 2][n - 1:n]
        tail_ref[1:2, cols] = blocks[FFN_PHASES - 1][n - 1:n]
        tap1 = [s1] + blocks[:-1]
        tap2 = [s2, s1] + blocks[:-2]
        w0, w1, w2, bias = cw_ref[0:1, cols], cw_ref[1:2, cols], cw_ref[2:3, cols], cb_ref[:, cols]
        return jnp.concatenate(
            [bias + w2 * blocks[b] + w1 * tap1[b] + w0 * tap2[b] for b in range(FFN_PHASES)], axis=0)

    for j in range(D_FF // FFN_COL_CHUNK):
        c0 = j * FFN_COL_CHUNK
        hg = _dot(hn, wup_ref[:, c0:c0 + FFN_COL_CHUNK])
        hv = _dot(hn, wup_ref[:, D_FF + c0:D_FF + c0 + FFN_COL_CHUNK])
        act = _gelu_tanh(conv(hg, c0)) * conv(hv, D_FF + c0)
        act_ref[:, c0:c0 + FFN_COL_CHUNK] = act.astype(BF16)

    ff = _dot(act_ref[...], wdown_ref[...])
    out = x1 + _rms(ff, gpost_ref[...])

    for c in range(lane_groups):
        for b in range(FFN_PHASES):
            perm_ref[c, pl.ds(b, n, stride=FFN_PHASES), :] = out[b * n:(b + 1) * n, c * LANES:(c + 1) * LANES]
    for c in range(lane_groups):
        o_ref[:, c * LANES:(c + 1) * LANES] = perm_ref[c]


def _conv_ffn(x1, g_pre, w_up, conv_w, conv_b, w_down, g_post):
    b, l, d = x1.shape
    tl = ROW_TILE
    const = lambda *_: (0, 0)
    return pl.pallas_call(
        _ffn_kernel,
        grid=(b, l // tl),
        in_specs=[
            pl.BlockSpec((None, tl, d), lambda bi, i: (bi, i, 0)),
            pl.BlockSpec((1, d), const),
            pl.BlockSpec(w_up.shape, const, pipeline_mode=pl.Buffered(1)),
            pl.BlockSpec(conv_w.shape, const),
            pl.BlockSpec(conv_b.shape, const),
            pl.BlockSpec(w_down.shape, const, pipeline_mode=pl.Buffered(1)),
            pl.BlockSpec((1, d), const),
        ],
        out_specs=pl.BlockSpec((None, tl, d), lambda bi, i: (bi, i, 0)),
        out_shape=jax.ShapeDtypeStruct((b, l, d), F32),
        scratch_shapes=[pltpu.VMEM((tl, D_FF), BF16),
                        pltpu.VMEM((d // LANES, tl, LANES), F32),
                        pltpu.VMEM((CONV_WIDTH - 1, 2 * D_FF), F32)],
        compiler_params=pltpu.CompilerParams(
            dimension_semantics=("parallel", "arbitrary"), vmem_limit_bytes=VMEM_LIMIT),
        name="conv_ffn",
    )(x1, g_pre, w_up, conv_w, conv_b, w_down, g_post)


def _head_lane_groups(nope, rope):
    k = (nope if nope is not None else rope).shape[0]
    nope = jnp.zeros((k, N_HEADS, QK_NOPE_DIM), BF16) if nope is None else nope
    rope = jnp.zeros((k, N_HEADS, QK_ROPE_DIM), BF16) if rope is None else rope
    return jnp.concatenate([nope, rope, rope], axis=2).astype(BF16).reshape(k, -1)


def _layer(x, pos_f, p):
    b, l, d = x.shape
    row = lambda v: v.reshape(1, -1)

    w_in = p["w_in"]
    off_u = P_KR + QK_ROPE_DIM
    off_gate = off_u + SSM_WIDTH
    w_kr = w_in[:, P_KR:off_u]
    w_low = jnp.concatenate([w_in[:, :P_KR], jnp.zeros((d, QK_NOPE_DIM), w_in.dtype), w_kr, w_kr],
                            axis=1).astype(BF16)
    w_u = w_in[:, off_u:off_gate].astype(BF16)
    w_gate = w_in[:, off_gate:].astype(BF16)
    w_uq = p["w_uq"].reshape(Q_LORA_RANK, N_HEADS, QK_HEAD_DIM)
    w_uq_p = _head_lane_groups(w_uq[:, :, :QK_NOPE_DIM], w_uq[:, :, QK_NOPE_DIM:])
    w_uk_p = _head_lane_groups(p["w_uk"].reshape(KV_LORA_RANK, N_HEADS, QK_NOPE_DIM), None)

    q_all, k_all, v_t, u_tm, gates = _in_proj(
        x, pos_f, row(p["mix_norm_pre"]), w_low, w_u, w_gate, row(p["q_norm"]), w_uq_p, row(p["kv_norm"]),
        w_uk_p, p["w_uv"].T.astype(BF16), row(p["b_gate"]))

    attn = _attention(q_all, k_all, v_t)

    a_re_t, a_im_t, wb, wcr, wci = _s5_prep(
        p["ssm_lambda_re"], p["ssm_lambda_im"], p["ssm_log_dt"], p["ssm_b_re"], p["ssm_b_im"],
        p["ssm_c_re"], p["ssm_c_im"], b)
    ssm_tm = _s5(u_tm, a_re_t, a_im_t, wb, wcr, wci,
                 row(p["ssm_d"]), p["w_glu"].astype(BF16), row(p["b_glu"]), b)

    x1 = _merge(x, attn, ssm_tm, gates, p["w_branch_attn"].astype(BF16), p["w_branch_ssm"].astype(BF16),
                p["w_out"].astype(BF16), row(p["mix_norm_post"]))
    return _conv_ffn(x1, row(p["ffn_norm_pre"]), p["w_up"].astype(BF16), p["conv_w"], row(p["conv_b"]),
                     p["w_down"].astype(BF16), row(p["ffn_norm_post"]))


def kernel(x, positions, mix_norm_pre, w_in, q_norm, w_uq, kv_norm, w_uk, w_uv, ssm_lambda_re, ssm_lambda_im, ssm_log_dt, ssm_b_re, ssm_b_im, ssm_c_re, ssm_c_im, ssm_d, w_glu, b_glu, w_branch_attn, w_branch_ssm, b_gate, w_out, mix_norm_post, ffn_norm_pre, w_up, conv_w, conv_b, w_down, ffn_norm_post):
    b, l, _ = x.shape
    params = dict(mix_norm_pre=mix_norm_pre, w_in=w_in, q_norm=q_norm, w_uq=w_uq, kv_norm=kv_norm,
                  w_uk=w_uk, w_uv=w_uv, ssm_lambda_re=ssm_lambda_re, ssm_lambda_im=ssm_lambda_im,
                  ssm_log_dt=ssm_log_dt, ssm_b_re=ssm_b_re, ssm_b_im=ssm_b_im, ssm_c_re=ssm_c_re,
                  ssm_c_im=ssm_c_im, ssm_d=ssm_d, w_glu=w_glu, b_glu=b_glu, w_branch_attn=w_branch_attn,
                  w_branch_ssm=w_branch_ssm, b_gate=b_gate, w_out=w_out, mix_norm_post=mix_norm_post,
                  ffn_norm_pre=ffn_norm_pre, w_up=w_up, conv_w=conv_w, conv_b=conv_b, w_down=w_down,
                  ffn_norm_post=ffn_norm_post)
    pos_f = positions.astype(F32).reshape(b, 1, l)
    for layer in range(mix_norm_pre.shape[0]):
        x = _layer(x, pos_f, {k: v[layer] for k, v in params.items()})
    return x
```

```python
import functools
import math

import jax
import jax.numpy as jnp
from jax import lax
from jax.experimental import pallas as pl
from jax.experimental.pallas import tpu as pltpu

D_MODEL = 1024
N_HEADS = 8
QK_NOPE_DIM = 64
QK_ROPE_DIM = 32
QK_HEAD_DIM = QK_NOPE_DIM + QK_ROPE_DIM
V_HEAD_DIM = 64
Q_LORA_RANK = 384
KV_LORA_RANK = 256
ROPE_THETA = 10000.0
SSM_WIDTH = 512
SSM_GROUP = 16
SSM_GROUPS = SSM_WIDTH // SSM_GROUP
SSM_STATE = 64
D_FF = 2816
CONV_WIDTH = 3
EPS = 1e-6

LANES = 128
HEAD_PAD = LANES
ROPE_LANE0 = QK_NOPE_DIM
ROPE_HALF = QK_ROPE_DIM // 2
N_PAIRS = N_HEADS // 2
SUM_ROW = (V_HEAD_DIM, 0)
SSM_HALVES = 2
SSM_HALF_IN = SSM_WIDTH // SSM_HALVES
SSM_HALF_STATE = SSM_GROUPS * SSM_STATE // SSM_HALVES

P_CQ = 0
P_CKV = P_CQ + Q_LORA_RANK
P_KR = P_CKV + KV_LORA_RANK
P_U = P_KR + LANES

Q_SCALE = (1.0 / math.sqrt(QK_HEAD_DIM)) * math.log2(math.e)
NEG_BIG = -1e30

ROW_TILE = 512
ATTN_TILE = ROW_TILE
ATTN_KEY_BLOCK = ATTN_TILE // 2
SSM_CHUNK = 64
FFN_COL_CHUNK = 256
FFN_PHASES = 8
VMEM_LIMIT = 56 * 1024 * 1024

BF16 = jnp.bfloat16
F32 = jnp.float32


def _rms(x, g):
    return x * lax.rsqrt(jnp.mean(x * x, axis=-1, keepdims=True) + EPS) * g


def _dot(a, b):
    return jnp.dot(a, b, preferred_element_type=F32)


GELU_C = math.sqrt(2.0 / math.pi)


def _gelu_tanh(x):
    k = -2.0 * GELU_C * math.log2(math.e)
    e = jnp.exp2(x * (x * x * (k * 0.044715) + k))
    return x / (1.0 + e)


def _s5_prep_kernel(lr_ref, li_ref, logdt_ref, br_ref, bi_ref, cr_ref, ci_ref,
                    are_ref, aim_ref, wb_ref, wcr_ref, wci_ref, *, batch):
    lr = lr_ref[...]
    li = li_ref[...]
    dt = jnp.exp(logdt_ref[...])
    mag = jnp.exp(lr * dt)
    ang = li * dt
    a_re = mag * jnp.cos(ang)
    a_im = mag * jnp.sin(ang)
    den = lr * lr + li * li
    n_re = a_re - 1.0
    n_im = a_im
    z_re = (n_re * lr + n_im * li) / den
    z_im = (n_im * lr - n_re * li) / den
    are_ref[...] = jnp.broadcast_to(a_re, (batch, a_re.shape[1]))
    aim_ref[...] = jnp.broadcast_to(a_im, (batch, a_im.shape[1]))
    br = br_ref[...]
    bi = bi_ref[...]
    bb_re = z_re * br - z_im * bi
    bb_im = z_re * bi + z_im * br

    groups_per_half = SSM_GROUPS // SSM_HALVES
    row_group = jnp.right_shift(lax.broadcasted_iota(jnp.int32, (SSM_HALF_IN, SSM_HALF_STATE), 0),
                                SSM_GROUP.bit_length() - 1)
    col_group = jnp.right_shift(lax.broadcasted_iota(jnp.int32, (SSM_HALF_IN, SSM_HALF_STATE), 1),
                                SSM_STATE.bit_length() - 1)
    on_diag = row_group == col_group

    def diag(v):
        return jnp.where(on_diag, jnp.concatenate([v] * groups_per_half, axis=0), 0.0)

    for k in range(SSM_HALVES):
        cols = slice(k * SSM_HALF_STATE, (k + 1) * SSM_HALF_STATE)
        wb_ref[k, :, :SSM_HALF_STATE] = diag(bb_re[:, cols]).astype(BF16)
        wb_ref[k, :, SSM_HALF_STATE:] = diag(bb_im[:, cols]).astype(BF16)
        wcr_ref[k] = diag(cr_ref[:, cols]).astype(BF16)
        wci_ref[k] = diag(ci_ref[:, cols]).astype(BF16)


def _s5_prep(lam_re, lam_im, log_dt, b_re, b_im, c_re, c_im, batch):
    n = SSM_GROUPS * SSM_STATE
    lanes = lambda v: v.reshape(1, n)
    chan_major = lambda v, perm: v.transpose(perm).reshape(SSM_GROUP, n)
    return pl.pallas_call(
        functools.partial(_s5_prep_kernel, batch=batch),
        out_shape=(jax.ShapeDtypeStruct((batch, n), F32), jax.ShapeDtypeStruct((batch, n), F32),
                   jax.ShapeDtypeStruct((SSM_HALVES, SSM_HALF_IN, 2 * SSM_HALF_STATE), BF16),
                   jax.ShapeDtypeStruct((SSM_HALVES, SSM_HALF_IN, SSM_HALF_STATE), BF16),
                   jax.ShapeDtypeStruct((SSM_HALVES, SSM_HALF_IN, SSM_HALF_STATE), BF16)),
        name="s5_prep",
    )(lanes(lam_re), lanes(lam_im), lanes(jnp.repeat(log_dt, SSM_STATE)),
      chan_major(b_re, (2, 0, 1)), chan_major(b_im, (2, 0, 1)),
      chan_major(c_re, (1, 0, 2)), chan_major(c_im, (1, 0, 2)))


def _in_proj_kernel(x_ref, pos_ref, gpre_ref, wlow_ref, wu_ref, qn_ref, wuq_ref, kvn_ref, wuk_ref,
                    wuv_ref, q_ref, k_ref, vt_ref, u_ref):
    x = x_ref[...]
    tl = x.shape[0]
    hn = _rms(x, gpre_ref[...]).astype(BF16)

    freq_idx = lax.broadcasted_iota(jnp.int32, (ROPE_HALF, 1), 0).astype(F32)
    inv_freq = jnp.exp(freq_idx * (-2.0 * math.log(ROPE_THETA) / QK_ROPE_DIM))
    ang = inv_freq * pos_ref[...]
    c16 = jnp.cos(ang)
    s16 = jnp.sin(ang)
    ones = jnp.ones((ROPE_LANE0, tl), F32)
    zeros = lambda n: jnp.zeros((n, tl), F32)
    cos = jnp.concatenate([ones, zeros(ROPE_HALF), c16, c16, zeros(ROPE_HALF)], axis=0).T
    sin = jnp.concatenate([zeros(ROPE_LANE0 + ROPE_HALF), s16, -s16, zeros(ROPE_HALF)], axis=0).T
    cos_q = cos * Q_SCALE
    sin_q = sin * Q_SCALE

    low = _dot(hn, wlow_ref[...])
    cq = low[:, P_CQ:P_CKV]
    ckv = low[:, P_CKV:P_KR]
    kr = low[:, P_KR:P_U]
    cqn = _rms(cq, qn_ref[...]).astype(BF16)
    q = _dot(cqn, wuq_ref[...])
    for h in range(N_HEADS):
        sl = slice(h * HEAD_PAD, (h + 1) * HEAD_PAD)
        t = q[:, sl]
        q_ref[:, sl] = (t * cos_q + pltpu.roll(t, ROPE_HALF, 1) * sin_q).astype(BF16)

    ckvn = _rms(ckv, kvn_ref[...]).astype(BF16)
    v_t = lax.dot_general(wuv_ref[...], ckvn, (((1,), (1,)), ((), ())),
                          preferred_element_type=F32).astype(BF16)
    for c in range(vt_ref.shape[0]):
        vt_ref[c] = v_t[:, c * ATTN_KEY_BLOCK:(c + 1) * ATTN_KEY_BLOCK]
    kn = _dot(ckvn, wuk_ref[...])
    kr = kr * cos + pltpu.roll(kr, ROPE_HALF, 1) * sin
    for h in range(N_HEADS):
        sl = slice(h * HEAD_PAD, (h + 1) * HEAD_PAD)
        k_ref[:, sl] = (kn[:, sl] + kr).astype(BF16)

    u_ref[...] = _dot(hn, wu_ref[...]).astype(BF16)


def _in_proj(x, pos_f, g_pre, w_low, w_u, q_norm, w_uq_p, kv_norm, w_uk_p, w_uv):
    b, l, d = x.shape
    tl = ROW_TILE
    const = lambda *_: (0, 0)
    return pl.pallas_call(
        _in_proj_kernel,
        grid=(b, l // tl),
        in_specs=[
            pl.BlockSpec((None, tl, d), lambda bi, i: (bi, i, 0)),
            pl.BlockSpec((None, 1, tl), lambda bi, i: (bi, 0, i)),
            pl.BlockSpec((1, d), const),
            pl.BlockSpec(w_low.shape, const),
            pl.BlockSpec(w_u.shape, const),
            pl.BlockSpec((1, Q_LORA_RANK), const),
            pl.BlockSpec((Q_LORA_RANK, N_HEADS * HEAD_PAD), const),
            pl.BlockSpec((1, KV_LORA_RANK), const),
            pl.BlockSpec((KV_LORA_RANK, N_HEADS * HEAD_PAD), const),
            pl.BlockSpec((N_HEADS * V_HEAD_DIM, KV_LORA_RANK), const),
        ],
        out_specs=[
            pl.BlockSpec((None, tl, N_HEADS * HEAD_PAD), lambda bi, i: (bi, i, 0)),
            pl.BlockSpec((None, tl, N_HEADS * HEAD_PAD), lambda bi, i: (bi, i, 0)),
            pl.BlockSpec((None, tl // ATTN_KEY_BLOCK, N_HEADS * V_HEAD_DIM, ATTN_KEY_BLOCK),
                         lambda bi, i: (bi, i, 0, 0)),
            pl.BlockSpec((tl, SSM_WIDTH), lambda bi, i: (i, bi)),
        ],
        out_shape=[
            jax.ShapeDtypeStruct((b, l, N_HEADS * HEAD_PAD), BF16),
            jax.ShapeDtypeStruct((b, l, N_HEADS * HEAD_PAD), BF16),
            jax.ShapeDtypeStruct((b, l // ATTN_KEY_BLOCK, N_HEADS * V_HEAD_DIM, ATTN_KEY_BLOCK), BF16),
            jax.ShapeDtypeStruct((l, b * SSM_WIDTH), BF16),
        ],
        compiler_params=pltpu.CompilerParams(
            dimension_semantics=("parallel", "parallel"), vmem_limit_bytes=VMEM_LIMIT),
        name="in_proj",
    )(x, pos_f, g_pre, w_low, w_u, q_norm, w_uq_p, kv_norm, w_uk_p, w_uv)


def _attn_kernel(q_ref, k_ref, vt_ref, o_ref, s_ref, mx_ref, m_ref, acc_ref, *, tq, tk):
    n_q = q_ref.shape[0] // tq
    key = lax.broadcasted_iota(jnp.int32, (tk, tq), 0)
    qry = lax.broadcasted_iota(jnp.int32, (tk, tq), 1)
    diag_mask = key <= qry
    vrow = lax.broadcasted_iota(jnp.int32, (2 * V_HEAD_DIM, tk), 0)
    one = jnp.ones((), BF16)

    def query_tile(i, carry):
        def scores(j, slot, mask=None, tile=None, cols=slice(None)):
            r0 = pl.multiple_of(j * tk, tk)
            nq = tq if cols.start is None else cols.stop - cols.start
            q0 = pl.multiple_of((i if tile is None else tile) * tq + (cols.start or 0), nq)
            for hh in range(2):
                lanes = slice(hh * HEAD_PAD, (hh + 1) * HEAD_PAD)
                kj = k_ref[pl.ds(r0, tk), lanes]
                qt = q_ref[pl.ds(q0, nq), lanes]
                s = lax.dot_general(kj, qt, (((1,), (1,)), ((), ())), preferred_element_type=F32)
                if mask is not None:
                    s = jnp.where(mask, s, NEG_BIG)
                s_ref[slot, hh, :, cols] = s
                mx_ref[slot, hh, :, cols] = jnp.max(s, axis=0, keepdims=True)

        def softmax_pv(j, slot, cols=slice(None)):
            vt = vt_ref[j]
            for hh in range(2):
                m = m_ref[hh, :, cols]
                m_new = jnp.maximum(m, mx_ref[slot, hh, :, cols])
                alpha = jnp.exp2(m - m_new)
                p = jnp.exp2(s_ref[slot, hh, :, cols] - m_new)
                vt_h = jnp.where(vrow == SUM_ROW[hh], one, vt)
                acc_ref[hh, :, cols] = alpha * acc_ref[hh, :, cols] + _dot(vt_h, p.astype(BF16))
                m_ref[hh, :, cols] = m_new

        m_ref[...] = jnp.full(m_ref.shape, NEG_BIG, F32)
        acc_ref[...] = jnp.zeros(acc_ref.shape, F32)

        def two_blocks(t, last=False):
            scores(2 * t + 1, 1)
            softmax_pv(2 * t, 0)
            scores(2 * t + 2, 0, diag_mask if last else None)
            softmax_pv(2 * t + 1, 1)

        def four_blocks(t, c):
            two_blocks(2 * t)
            two_blocks(2 * t + 1)
            return c

        n_pairs = jnp.maximum(i - 1, 0)
        lax.fori_loop(0, n_pairs // 2, four_blocks, 0)

        @pl.when(n_pairs % 2 == 1)
        def _():
            two_blocks(n_pairs - 1)

        def diagonal_tile():
            late = slice(tk, tq)
            scores(2 * i + 1, 1, diag_mask[:, :tq - tk], cols=late)
            softmax_pv(2 * i, 0)
            scores(0, 0, tile=jnp.minimum(i + 1, n_q - 1))
            softmax_pv(2 * i + 1, 1, cols=late)

        @pl.when(i > 0)
        def _():
            two_blocks(i - 1, last=True)
            diagonal_tile()

        @pl.when(i == 0)
        def _():
            scores(0, 0, diag_mask)
            diagonal_tile()

        a0 = acc_ref[0]
        a1 = acc_ref[1]
        o0 = a0[:V_HEAD_DIM] / a0[SUM_ROW[0]:SUM_ROW[0] + 1]
        o1 = a1[V_HEAD_DIM:] / a1[SUM_ROW[1]:SUM_ROW[1] + 1]
        out_t = jnp.concatenate([o0, o1], axis=0)
        o_ref[pl.ds(pl.multiple_of(i * tq, tq), tq), :] = out_t.T.astype(BF16)
        return carry

    lax.fori_loop(0, n_q, query_tile, 0)


def _attention(q_all, k_all, v_t):
    b, l, _ = q_all.shape
    tq, tk = ATTN_TILE, ATTN_KEY_BLOCK
    return pl.pallas_call(
        functools.partial(_attn_kernel, tq=tq, tk=tk),
        grid=(b, N_PAIRS),
        in_specs=[
            pl.BlockSpec((None, l, 2 * HEAD_PAD), lambda bi, p: (bi, 0, p)),
            pl.BlockSpec((None, l, 2 * HEAD_PAD), lambda bi, p: (bi, 0, p)),
            pl.BlockSpec((None, l // tk, 2 * V_HEAD_DIM, tk), lambda bi, p: (bi, 0, p, 0)),
        ],
        out_specs=pl.BlockSpec((None, l, 2 * V_HEAD_DIM), lambda bi, p: (bi, 0, p)),
        out_shape=jax.ShapeDtypeStruct((b, l, N_HEADS * V_HEAD_DIM), BF16),
        scratch_shapes=[
            pltpu.VMEM((2, 2, tk, tq), F32),
            pltpu.VMEM((2, 2, 1, tq), F32),
            pltpu.VMEM((2, 1, tq), F32),
            pltpu.VMEM((2, 2 * V_HEAD_DIM, tq), F32),
        ],
        compiler_params=pltpu.CompilerParams(
            dimension_semantics=("parallel", "parallel"), vmem_limit_bytes=VMEM_LIMIT),
        name="attention",
    )(q_all, k_all, v_t)


def _s5_kernel(u_ref, are_ref, aim_ref, wb_ref, wcr_ref, wci_ref, d_ref, wglu_ref, bglu_ref,
               o_ref, state_ref, sre_ref, sim_ref, tb_ref, *, steps, batch):
    @pl.when(pl.program_id(0) == 0)
    def _():
        state_ref[...] = jnp.zeros_like(state_ref)

    lane_groups = SSM_WIDTH // LANES
    for b in range(batch):
        for c in range(lane_groups):
            c0 = b * SSM_WIDTH + c * LANES
            tb_ref[c, pl.ds(b, steps, stride=batch), :] = u_ref[:, c0:c0 + LANES].astype(F32)
    u32 = jnp.concatenate([tb_ref[c] for c in range(lane_groups)], axis=1)
    u = u32.astype(BF16)
    for k in range(SSM_HALVES):
        bu = _dot(u[:, k * SSM_HALF_IN:(k + 1) * SSM_HALF_IN], wb_ref[k])
        sre_ref[k] = bu[:, :SSM_HALF_STATE]
        sim_ref[k] = bu[:, SSM_HALF_STATE:]

    nt = (((1,), (1,)), ((), ()))
    ys = []
    for k in range(SSM_HALVES):
        ar = are_ref[:, k * SSM_HALF_STATE:(k + 1) * SSM_HALF_STATE]
        ai = aim_ref[:, k * SSM_HALF_STATE:(k + 1) * SSM_HALF_STATE]
        sr = state_ref[k, 0]
        si = state_ref[k, 1]
        for t in range(steps):
            rs = slice(t * batch, (t + 1) * batch)
            nr = ar * sr - ai * si + sre_ref[k, rs, :]
            ni = ar * si + ai * sr + sim_ref[k, rs, :]
            sre_ref[k, rs, :] = nr
            sim_ref[k, rs, :] = ni
            sr, si = nr, ni
        state_ref[k, 0] = sr
        state_ref[k, 1] = si
        ys.append(lax.dot_general(sre_ref[k].astype(BF16), wcr_ref[k], nt, preferred_element_type=F32)
                  - lax.dot_general(sim_ref[k].astype(BF16), wci_ref[k], nt, preferred_element_type=F32))

    y = jnp.concatenate(ys, axis=1) + d_ref[...] * u32
    g = _gelu_tanh(y)
    z = _dot(g.astype(BF16), wglu_ref[...]) + bglu_ref[...]
    out = g * jax.nn.sigmoid(z)
    for c in range(lane_groups):
        tb_ref[c] = out[:, c * LANES:(c + 1) * LANES]
    for b in range(batch):
        for c in range(lane_groups):
            c0 = b * SSM_WIDTH + c * LANES
            o_ref[:, c0:c0 + LANES] = tb_ref[c, pl.ds(b, steps, stride=batch), :].astype(BF16)


def _s5(u_tm, a_re_t, a_im_t, wb, wcr, wci, d_skip, w_glu, b_glu, batch):
    length = u_tm.shape[0]
    width = SSM_WIDTH
    steps = SSM_CHUNK
    rows = steps * batch
    const2 = lambda i: (0, 0)
    const3 = lambda i: (0, 0, 0)
    return pl.pallas_call(
        functools.partial(_s5_kernel, steps=steps, batch=batch),
        grid=(length // steps,),
        in_specs=[
            pl.BlockSpec((steps, batch * width), lambda i: (i, 0)),
            pl.BlockSpec(a_re_t.shape, const2),
            pl.BlockSpec(a_im_t.shape, const2),
            pl.BlockSpec(wb.shape, const3),
            pl.BlockSpec(wcr.shape, const3),
            pl.BlockSpec(wci.shape, const3),
            pl.BlockSpec((1, width), const2),
            pl.BlockSpec((width, width), const2),
            pl.BlockSpec((1, width), const2),
        ],
        out_specs=pl.BlockSpec((steps, batch * width), lambda i: (i, 0)),
        out_shape=jax.ShapeDtypeStruct((length, batch * width), BF16),
        scratch_shapes=[
            pltpu.VMEM((SSM_HALVES, 2, batch, SSM_HALF_STATE), F32),
            pltpu.VMEM((SSM_HALVES, rows, SSM_HALF_STATE), F32),
            pltpu.VMEM((SSM_HALVES, rows, SSM_HALF_STATE), F32),
            pltpu.VMEM((width // LANES, rows, LANES), F32),
        ],
        compiler_params=pltpu.CompilerParams(
            dimension_semantics=("arbitrary",), vmem_limit_bytes=VMEM_LIMIT),
        name="s5",
    )(u_tm, a_re_t, a_im_t, wb, wcr, wci, d_skip, w_glu, b_glu)


def _merge_kernel(x_ref, attn_ref, ssm_ref, gpre_ref, wgate_ref, bgate_ref, wba_ref, wbs_ref, wout_ref,
                  gpost_ref, o_ref, perm_ref):
    x = x_ref[...]
    tl, d = x.shape
    n = tl // FFN_PHASES
    hn = _rms(x, gpre_ref[...]).astype(BF16)
    gates = jax.nn.sigmoid(_dot(hn, wgate_ref[...]) + bgate_ref[...])
    merged = (gates[:, :d] * _dot(attn_ref[...], wba_ref[...])
              + gates[:, d:] * _dot(ssm_ref[...], wbs_ref[...]))
    m2 = _dot(merged.astype(BF16), wout_ref[...])
    out = x + _rms(m2, gpost_ref[...])
    for c in range(d // LANES):
        perm_ref[c] = out[:, c * LANES:(c + 1) * LANES]
    for c in range(d // LANES):
        for b in range(FFN_PHASES):
            o_ref[b * n:(b + 1) * n, c * LANES:(c + 1) * LANES] = perm_ref[c, pl.ds(b, n, stride=FFN_PHASES), :]


def _merge(x, attn, ssm_tm, g_pre, w_gate, b_gate, w_ba, w_bs, w_out, g_post):
    b, l, d = x.shape
    tl = ROW_TILE
    const = lambda *_: (0, 0)
    return pl.pallas_call(
        _merge_kernel,
        grid=(b, l // tl),
        in_specs=[
            pl.BlockSpec((None, tl, d), lambda bi, i: (bi, i, 0)),
            pl.BlockSpec((None, tl, N_HEADS * V_HEAD_DIM), lambda bi, i: (bi, i, 0)),
            pl.BlockSpec((tl, SSM_WIDTH), lambda bi, i: (i, bi)),
            pl.BlockSpec((1, d), const),
            pl.BlockSpec(w_gate.shape, const),
            pl.BlockSpec((1, 2 * d), const),
            pl.BlockSpec(w_ba.shape, const),
            pl.BlockSpec(w_bs.shape, const),
            pl.BlockSpec(w_out.shape, const),
            pl.BlockSpec((1, d), const),
        ],
        out_specs=pl.BlockSpec((None, tl, d), lambda bi, i: (bi, i, 0)),
        out_shape=jax.ShapeDtypeStruct((b, l, d), F32),
        scratch_shapes=[pltpu.VMEM((d // LANES, tl, LANES), F32)],
        compiler_params=pltpu.CompilerParams(
            dimension_semantics=("parallel", "parallel"), vmem_limit_bytes=VMEM_LIMIT),
        name="merge",
    )(x, attn, ssm_tm, g_pre, w_gate, b_gate, w_ba, w_bs, w_out, g_post)


def _ffn_kernel(x_ref, gpre_ref, wup_ref, cw_ref, cb_ref, wdown_ref, gpost_ref, o_ref,
                act_ref, perm_ref, tail_ref):
    i = pl.program_id(1)
    tl, d = x_ref.shape
    n = tl // FFN_PHASES
    lane_groups = d // LANES

    @pl.when(i == 0)
    def _():
        tail_ref[...] = jnp.zeros(tail_ref.shape, F32)

    x1 = x_ref[...]
    hn = _rms(x1, gpre_ref[...]).astype(BF16)

    first_row = lax.broadcasted_iota(jnp.int32, (n, 1), 0) == 0

    def conv(h, c0):
        cols = slice(c0, c0 + FFN_COL_CHUNK)
        blocks = [h[b * n:(b + 1) * n] for b in range(FFN_PHASES)]

        def moved_down(block, prev_token):
            return jnp.where(first_row, prev_token, pltpu.roll(block, 1, 0))

        s1 = moved_down(blocks[FFN_PHASES - 1], tail_ref[1:2, cols])
        s2 = moved_down(blocks[FFN_PHASES - 2], tail_ref[0:1, cols])
        tail_ref[0:1, cols] = blocks[FFN_PHASES - 2][n - 1:n]
        tail_ref[1:2, cols] = blocks[FFN_PHASES - 1][n - 1:n]
        tap1 = [s1] + blocks[:-1]
        tap2 = [s2, s1] + blocks[:-2]
        w0, w1, w2, bias = cw_ref[0:1, cols], cw_ref[1:2, cols], cw_ref[2:3, cols], cb_ref[:, cols]
        return jnp.concatenate(
            [bias + w2 * blocks[b] + w1 * tap1[b] + w0 * tap2[b] for b in range(FFN_PHASES)], axis=0)

    for j in range(D_FF // FFN_COL_CHUNK):
        c0 = j * FFN_COL_CHUNK
        hg = _dot(hn, wup_ref[:, c0:c0 + FFN_COL_CHUNK])
        hv = _dot(hn, wup_ref[:, D_FF + c0:D_FF + c0 + FFN_COL_CHUNK])
        act = _gelu_tanh(conv(hg, c0)) * conv(hv, D_FF + c0)
        act_ref[:, c0:c0 + FFN_COL_CHUNK] = act.astype(BF16)

    ff = _dot(act_ref[...], wdown_ref[...])
    out = x1 + _rms(ff, gpost_ref[...])

    for c in range(lane_groups):
        for b in range(FFN_PHASES):
            perm_ref[c, pl.ds(b, n, stride=FFN_PHASES), :] = out[b * n:(b + 1) * n, c * LANES:(c + 1) * LANES]
    for c in range(lane_groups):
        o_ref[:, c * LANES:(c + 1) * LANES] = perm_ref[c]


def _conv_ffn(x1, g_pre, w_up, conv_w, conv_b, w_down, g_post):
    b, l, d = x1.shape
    tl = ROW_TILE
    const = lambda *_: (0, 0)
    return pl.pallas_call(
        _ffn_kernel,
        grid=(b, l // tl),
        in_specs=[
            pl.BlockSpec((None, tl, d), lambda bi, i: (bi, i, 0)),
            pl.BlockSpec((1, d), const),
            pl.BlockSpec(w_up.shape, const, pipeline_mode=pl.Buffered(1)),
            pl.BlockSpec(conv_w.shape, const),
            pl.BlockSpec(conv_b.shape, const),
            pl.BlockSpec(w_down.shape, const, pipeline_mode=pl.Buffered(1)),
            pl.BlockSpec((1, d), const),
        ],
        out_specs=pl.BlockSpec((None, tl, d), lambda bi, i: (bi, i, 0)),
        out_shape=jax.ShapeDtypeStruct((b, l, d), F32),
        scratch_shapes=[pltpu.VMEM((tl, D_FF), BF16),
                        pltpu.VMEM((d // LANES, tl, LANES), F32),
                        pltpu.VMEM((CONV_WIDTH - 1, 2 * D_FF), F32)],
        compiler_params=pltpu.CompilerParams(
            dimension_semantics=("parallel", "arbitrary"), vmem_limit_bytes=VMEM_LIMIT),
        name="conv_ffn",
    )(x1, g_pre, w_up, conv_w, conv_b, w_down, g_post)


def _head_lane_groups(nope, rope):
    k = (nope if nope is not None else rope).shape[0]
    nope = jnp.zeros((k, N_HEADS, QK_NOPE_DIM), BF16) if nope is None else nope
    rope = jnp.zeros((k, N_HEADS, QK_ROPE_DIM), BF16) if rope is None else rope
    return jnp.concatenate([nope, rope, rope], axis=2).astype(BF16).reshape(k, -1)


def _layer(x, pos_f, p):
    b, l, d = x.shape
    row = lambda v: v.reshape(1, -1)

    w_in = p["w_in"]
    off_u = P_KR + QK_ROPE_DIM
    off_gate = off_u + SSM_WIDTH
    w_kr = w_in[:, P_KR:off_u]
    w_low = jnp.concatenate([w_in[:, :P_KR], jnp.zeros((d, QK_NOPE_DIM), w_in.dtype), w_kr, w_kr],
                            axis=1).astype(BF16)
    w_u = w_in[:, off_u:off_gate].astype(BF16)
    w_gate = w_in[:, off_gate:].astype(BF16)
    w_uq = p["w_uq"].reshape(Q_LORA_RANK, N_HEADS, QK_HEAD_DIM)
    w_uq_p = _head_lane_groups(w_uq[:, :, :QK_NOPE_DIM], w_uq[:, :, QK_NOPE_DIM:])
    w_uk_p = _head_lane_groups(p["w_uk"].reshape(KV_LORA_RANK, N_HEADS, QK_NOPE_DIM), None)

    q_all, k_all, v_t, u_tm = _in_proj(
        x, pos_f, row(p["mix_norm_pre"]), w_low, w_u, row(p["q_norm"]), w_uq_p, row(p["kv_norm"]),
        w_uk_p, p["w_uv"].T.astype(BF16))

    attn = _attention(q_all, k_all, v_t)

    a_re_t, a_im_t, wb, wcr, wci = _s5_prep(
        p["ssm_lambda_re"], p["ssm_lambda_im"], p["ssm_log_dt"], p["ssm_b_re"], p["ssm_b_im"],
        p["ssm_c_re"], p["ssm_c_im"], b)
    ssm_tm = _s5(u_tm, a_re_t, a_im_t, wb, wcr, wci,
                 row(p["ssm_d"]), p["w_glu"].astype(BF16), row(p["b_glu"]), b)

    x1 = _merge(x, attn, ssm_tm, row(p["mix_norm_pre"]), w_gate, row(p["b_gate"]),
                p["w_branch_attn"].astype(BF16), p["w_branch_ssm"].astype(BF16),
                p["w_out"].astype(BF16), row(p["mix_norm_post"]))
    return _conv_ffn(x1, row(p["ffn_norm_pre"]), p["w_up"].astype(BF16), p["conv_w"], row(p["conv_b"]),
                     p["w_down"].astype(BF16), row(p["ffn_norm_post"]))


def kernel(x, positions, mix_norm_pre, w_in, q_norm, w_uq, kv_norm, w_uk, w_uv, ssm_lambda_re, ssm_lambda_im, ssm_log_dt, ssm_b_re, ssm_b_im, ssm_c_re, ssm_c_im, ssm_d, w_glu, b_glu, w_branch_attn, w_branch_ssm, b_gate, w_out, mix_norm_post, ffn_norm_pre, w_up, conv_w, conv_b, w_down, ffn_norm_post):
    b, l, _ = x.shape
    params = dict(mix_norm_pre=mix_norm_pre, w_in=w_in, q_norm=q_norm, w_uq=w_uq, kv_norm=kv_norm,
                  w_uk=w_uk, w_uv=w_uv, ssm_lambda_re=ssm_lambda_re, ssm_lambda_im=ssm_lambda_im,
                  ssm_log_dt=ssm_log_dt, ssm_b_re=ssm_b_re, ssm_b_im=ssm_b_im, ssm_c_re=ssm_c_re,
                  ssm_c_im=ssm_c_im, ssm_d=ssm_d, w_glu=w_glu, b_glu=b_glu, w_branch_attn=w_branch_attn,
                  w_branch_ssm=w_branch_ssm, b_gate=b_gate, w_out=w_out, mix_norm_post=mix_norm_post,
                  ffn_norm_pre=ffn_norm_pre, w_up=w_up, conv_w=conv_w, conv_b=conv_b, w_down=w_down,
                  ffn_norm_post=ffn_norm_post)
    pos_f = positions.astype(F32).reshape(b, 1, l)
    for layer in range(mix_norm_pre.shape[0]):
        x = _layer(x, pos_f, {k: v[layer] for k, v in params.items()})
    return x
```

```python
import functools
import math

import jax
import jax.numpy as jnp
from jax import lax
from jax.experimental import pallas as pl
from jax.experimental.pallas import tpu as pltpu

D_MODEL = 1024
N_HEADS = 8
QK_NOPE_DIM = 64
QK_ROPE_DIM = 32
QK_HEAD_DIM = QK_NOPE_DIM + QK_ROPE_DIM
V_HEAD_DIM = 64
Q_LORA_RANK = 384
KV_LORA_RANK = 256
ROPE_THETA = 10000.0
SSM_WIDTH = 512
SSM_GROUP = 16
SSM_GROUPS = SSM_WIDTH // SSM_GROUP
SSM_STATE = 64
D_FF = 2816
CONV_WIDTH = 3
EPS = 1e-6

LANES = 128
HEAD_PAD = LANES
ROPE_LANE0 = QK_NOPE_DIM
ROPE_HALF = QK_ROPE_DIM // 2
N_PAIRS = N_HEADS // 2
SUM_ROW = (V_HEAD_DIM, 0)
SSM_HALVES = 2
SSM_HALF_IN = SSM_WIDTH // SSM_HALVES
SSM_HALF_STATE = SSM_GROUPS * SSM_STATE // SSM_HALVES

P_CQ = 0
P_CKV = P_CQ + Q_LORA_RANK
P_KR = P_CKV + KV_LORA_RANK
P_U = P_KR + LANES

Q_SCALE = (1.0 / math.sqrt(QK_HEAD_DIM)) * math.log2(math.e)
NEG_BIG = -1e30

ROW_TILE = 512
ATTN_TILE = ROW_TILE
ATTN_KEY_BLOCK = ATTN_TILE // 2
SSM_CHUNK = 64
MERGE_SUBTILES = 2
FFN_COL_CHUNK = 256
FFN_PHASES = 8
VMEM_LIMIT = 56 * 1024 * 1024

BF16 = jnp.bfloat16
F32 = jnp.float32


def _rms(x, g):
    return x * lax.rsqrt(jnp.mean(x * x, axis=-1, keepdims=True) + EPS) * g


def _dot(a, b):
    return jnp.dot(a, b, preferred_element_type=F32)


GELU_C = math.sqrt(2.0 / math.pi)


def _gelu_tanh(x):
    k = -2.0 * GELU_C * math.log2(math.e)
    e = jnp.exp2(x * (x * x * (k * 0.044715) + k))
    return x / (1.0 + e)


def _s5_prep_kernel(lr_ref, li_ref, logdt_ref, br_ref, bi_ref, cr_ref, ci_ref,
                    are_ref, aim_ref, wb_ref, wcr_ref, wci_ref, *, batch):
    lr = lr_ref[...]
    li = li_ref[...]
    dt = jnp.exp(logdt_ref[...])
    mag = jnp.exp(lr * dt)
    ang = li * dt
    a_re = mag * jnp.cos(ang)
    a_im = mag * jnp.sin(ang)
    den = lr * lr + li * li
    n_re = a_re - 1.0
    n_im = a_im
    z_re = (n_re * lr + n_im * li) / den
    z_im = (n_im * lr - n_re * li) / den
    are_ref[...] = jnp.broadcast_to(a_re, (batch, a_re.shape[1]))
    aim_ref[...] = jnp.broadcast_to(a_im, (batch, a_im.shape[1]))
    br = br_ref[...]
    bi = bi_ref[...]
    bb_re = z_re * br - z_im * bi
    bb_im = z_re * bi + z_im * br

    groups_per_half = SSM_GROUPS // SSM_HALVES
    row_group = jnp.right_shift(lax.broadcasted_iota(jnp.int32, (SSM_HALF_IN, SSM_HALF_STATE), 0),
                                SSM_GROUP.bit_length() - 1)
    col_group = jnp.right_shift(lax.broadcasted_iota(jnp.int32, (SSM_HALF_IN, SSM_HALF_STATE), 1),
                                SSM_STATE.bit_length() - 1)
    on_diag = row_group == col_group

    def diag(v):
        return jnp.where(on_diag, jnp.concatenate([v] * groups_per_half, axis=0), 0.0)

    for k in range(SSM_HALVES):
        cols = slice(k * SSM_HALF_STATE, (k + 1) * SSM_HALF_STATE)
        wb_ref[k, :, :SSM_HALF_STATE] = diag(bb_re[:, cols]).astype(BF16)
        wb_ref[k, :, SSM_HALF_STATE:] = diag(bb_im[:, cols]).astype(BF16)
        wcr_ref[k] = diag(cr_ref[:, cols]).astype(BF16)
        wci_ref[k] = diag(ci_ref[:, cols]).astype(BF16)


def _s5_prep(lam_re, lam_im, log_dt, b_re, b_im, c_re, c_im, batch):
    n = SSM_GROUPS * SSM_STATE
    lanes = lambda v: v.reshape(1, n)
    chan_major = lambda v, perm: v.transpose(perm).reshape(SSM_GROUP, n)
    return pl.pallas_call(
        functools.partial(_s5_prep_kernel, batch=batch),
        out_shape=(jax.ShapeDtypeStruct((batch, n), F32), jax.ShapeDtypeStruct((batch, n), F32),
                   jax.ShapeDtypeStruct((SSM_HALVES, SSM_HALF_IN, 2 * SSM_HALF_STATE), BF16),
                   jax.ShapeDtypeStruct((SSM_HALVES, SSM_HALF_IN, SSM_HALF_STATE), BF16),
                   jax.ShapeDtypeStruct((SSM_HALVES, SSM_HALF_IN, SSM_HALF_STATE), BF16)),
        name="s5_prep",
    )(lanes(lam_re), lanes(lam_im), lanes(jnp.repeat(log_dt, SSM_STATE)),
      chan_major(b_re, (2, 0, 1)), chan_major(b_im, (2, 0, 1)),
      chan_major(c_re, (1, 0, 2)), chan_major(c_im, (1, 0, 2)))


def _in_proj_kernel(x_ref, pos_ref, gpre_ref, wlow_ref, wu_ref, qn_ref, wuq_ref, kvn_ref, wuk_ref,
                    wuv_ref, q_ref, k_ref, vt_ref, u_ref):
    x = x_ref[...]
    tl = x.shape[0]
    hn = _rms(x, gpre_ref[...]).astype(BF16)

    freq_idx = lax.broadcasted_iota(jnp.int32, (ROPE_HALF, 1), 0).astype(F32)
    inv_freq = jnp.exp(freq_idx * (-2.0 * math.log(ROPE_THETA) / QK_ROPE_DIM))
    ang = inv_freq * pos_ref[...]
    c16 = jnp.cos(ang)
    s16 = jnp.sin(ang)
    ones = jnp.ones((ROPE_LANE0, tl), F32)
    zeros = lambda n: jnp.zeros((n, tl), F32)
    cos = jnp.concatenate([ones, zeros(ROPE_HALF), c16, c16, zeros(ROPE_HALF)], axis=0).T
    sin = jnp.concatenate([zeros(ROPE_LANE0 + ROPE_HALF), s16, -s16, zeros(ROPE_HALF)], axis=0).T
    cos_q = cos * Q_SCALE
    sin_q = sin * Q_SCALE

    low = _dot(hn, wlow_ref[...])
    cq = low[:, P_CQ:P_CKV]
    ckv = low[:, P_CKV:P_KR]
    kr = low[:, P_KR:P_U]
    cqn = _rms(cq, qn_ref[...]).astype(BF16)
    q = _dot(cqn, wuq_ref[...])
    for h in range(N_HEADS):
        sl = slice(h * HEAD_PAD, (h + 1) * HEAD_PAD)
        t = q[:, sl]
        q_ref[:, sl] = (t * cos_q + pltpu.roll(t, ROPE_HALF, 1) * sin_q).astype(BF16)

    ckvn = _rms(ckv, kvn_ref[...]).astype(BF16)
    v_t = lax.dot_general(wuv_ref[...], ckvn, (((1,), (1,)), ((), ())),
                          preferred_element_type=F32).astype(BF16)
    for c in range(vt_ref.shape[0]):
        vt_ref[c] = v_t[:, c * ATTN_KEY_BLOCK:(c + 1) * ATTN_KEY_BLOCK]
    kn = _dot(ckvn, wuk_ref[...])
    kr = kr * cos + pltpu.roll(kr, ROPE_HALF, 1) * sin
    for h in range(N_HEADS):
        sl = slice(h * HEAD_PAD, (h + 1) * HEAD_PAD)
        k_ref[:, sl] = (kn[:, sl] + kr).astype(BF16)

    u_ref[...] = _dot(hn, wu_ref[...]).astype(BF16)


def _in_proj(x, pos_f, g_pre, w_low, w_u, q_norm, w_uq_p, kv_norm, w_uk_p, w_uv):
    b, l, d = x.shape
    tl = ROW_TILE
    const = lambda *_: (0, 0)
    return pl.pallas_call(
        _in_proj_kernel,
        grid=(b, l // tl),
        in_specs=[
            pl.BlockSpec((None, tl, d), lambda bi, i: (bi, i, 0)),
            pl.BlockSpec((None, 1, tl), lambda bi, i: (bi, 0, i)),
            pl.BlockSpec((1, d), const),
            pl.BlockSpec(w_low.shape, const),
            pl.BlockSpec(w_u.shape, const),
            pl.BlockSpec((1, Q_LORA_RANK), const),
            pl.BlockSpec((Q_LORA_RANK, N_HEADS * HEAD_PAD), const),
            pl.BlockSpec((1, KV_LORA_RANK), const),
            pl.BlockSpec((KV_LORA_RANK, N_HEADS * HEAD_PAD), const),
            pl.BlockSpec((N_HEADS * V_HEAD_DIM, KV_LORA_RANK), const),
        ],
        out_specs=[
            pl.BlockSpec((None, tl, N_HEADS * HEAD_PAD), lambda bi, i: (bi, i, 0)),
            pl.BlockSpec((None, tl, N_HEADS * HEAD_PAD), lambda bi, i: (bi, i, 0)),
            pl.BlockSpec((None, tl // ATTN_KEY_BLOCK, N_HEADS * V_HEAD_DIM, ATTN_KEY_BLOCK),
                         lambda bi, i: (bi, i, 0, 0)),
            pl.BlockSpec((tl, SSM_WIDTH), lambda bi, i: (i, bi)),
        ],
        out_shape=[
            jax.ShapeDtypeStruct((b, l, N_HEADS * HEAD_PAD), BF16),
            jax.ShapeDtypeStruct((b, l, N_HEADS * HEAD_PAD), BF16),
            jax.ShapeDtypeStruct((b, l // ATTN_KEY_BLOCK, N_HEADS * V_HEAD_DIM, ATTN_KEY_BLOCK), BF16),
            jax.ShapeDtypeStruct((l, b * SSM_WIDTH), BF16),
        ],
        compiler_params=pltpu.CompilerParams(
            dimension_semantics=("parallel", "parallel"), vmem_limit_bytes=VMEM_LIMIT),
        name="in_proj",
    )(x, pos_f, g_pre, w_low, w_u, q_norm, w_uq_p, kv_norm, w_uk_p, w_uv)


def _attn_kernel(q_ref, k_ref, vt_ref, o_ref, s_ref, mx_ref, m_ref, acc_ref, *, tq, tk):
    n_q = q_ref.shape[0] // tq
    key = lax.broadcasted_iota(jnp.int32, (tk, tq), 0)
    qry = lax.broadcasted_iota(jnp.int32, (tk, tq), 1)
    diag_mask = key <= qry
    vrow = lax.broadcasted_iota(jnp.int32, (2 * V_HEAD_DIM, tk), 0)
    one = jnp.ones((), BF16)

    def query_tile(i, carry):
        def scores(j, slot, mask=None, tile=None, cols=slice(None)):
            r0 = pl.multiple_of(j * tk, tk)
            nq = tq if cols.start is None else cols.stop - cols.start
            q0 = pl.multiple_of((i if tile is None else tile) * tq + (cols.start or 0), nq)
            for hh in range(2):
                lanes = slice(hh * HEAD_PAD, (hh + 1) * HEAD_PAD)
                kj = k_ref[pl.ds(r0, tk), lanes]
                qt = q_ref[pl.ds(q0, nq), lanes]
                s = lax.dot_general(kj, qt, (((1,), (1,)), ((), ())), preferred_element_type=F32)
                if mask is not None:
                    s = jnp.where(mask, s, NEG_BIG)
                s_ref[slot, hh, :, cols] = s
                mx_ref[slot, hh, :, cols] = jnp.max(s, axis=0, keepdims=True)

        def softmax_pv(j, slot, cols=slice(None)):
            vt = vt_ref[j]
            for hh in range(2):
                m = m_ref[hh, :, cols]
                m_new = jnp.maximum(m, mx_ref[slot, hh, :, cols])
                alpha = jnp.exp2(m - m_new)
                p = jnp.exp2(s_ref[slot, hh, :, cols] - m_new)
                vt_h = jnp.where(vrow == SUM_ROW[hh], one, vt)
                acc_ref[hh, :, cols] = alpha * acc_ref[hh, :, cols] + _dot(vt_h, p.astype(BF16))
                m_ref[hh, :, cols] = m_new

        m_ref[...] = jnp.full(m_ref.shape, NEG_BIG, F32)
        acc_ref[...] = jnp.zeros(acc_ref.shape, F32)

        def two_blocks(t, last=False):
            scores(2 * t + 1, 1)
            softmax_pv(2 * t, 0)
            scores(2 * t + 2, 0, diag_mask if last else None)
            softmax_pv(2 * t + 1, 1)

        def four_blocks(t, c):
            two_blocks(2 * t)
            two_blocks(2 * t + 1)
            return c

        n_pairs = jnp.maximum(i - 1, 0)
        lax.fori_loop(0, n_pairs // 2, four_blocks, 0)

        @pl.when(n_pairs % 2 == 1)
        def _():
            two_blocks(n_pairs - 1)

        def diagonal_tile():
            late = slice(tk, tq)
            scores(2 * i + 1, 1, diag_mask[:, :tq - tk], cols=late)
            softmax_pv(2 * i, 0)
            scores(0, 0, tile=jnp.minimum(i + 1, n_q - 1))
            softmax_pv(2 * i + 1, 1, cols=late)

        @pl.when(i > 0)
        def _():
            two_blocks(i - 1, last=True)
            diagonal_tile()

        @pl.when(i == 0)
        def _():
            scores(0, 0, diag_mask)
            diagonal_tile()

        a0 = acc_ref[0]
        a1 = acc_ref[1]
        o0 = a0[:V_HEAD_DIM] / a0[SUM_ROW[0]:SUM_ROW[0] + 1]
        o1 = a1[V_HEAD_DIM:] / a1[SUM_ROW[1]:SUM_ROW[1] + 1]
        out_t = jnp.concatenate([o0, o1], axis=0)
        o_ref[pl.ds(pl.multiple_of(i * tq, tq), tq), :] = out_t.T.astype(BF16)
        return carry

    lax.fori_loop(0, n_q, query_tile, 0)


def _attention(q_all, k_all, v_t):
    b, l, _ = q_all.shape
    tq, tk = ATTN_TILE, ATTN_KEY_BLOCK
    return pl.pallas_call(
        functools.partial(_attn_kernel, tq=tq, tk=tk),
        grid=(b, N_PAIRS),
        in_specs=[
            pl.BlockSpec((None, l, 2 * HEAD_PAD), lambda bi, p: (bi, 0, p)),
            pl.BlockSpec((None, l, 2 * HEAD_PAD), lambda bi, p: (bi, 0, p)),
            pl.BlockSpec((None, l // tk, 2 * V_HEAD_DIM, tk), lambda bi, p: (bi, 0, p, 0)),
        ],
        out_specs=pl.BlockSpec((None, l, 2 * V_HEAD_DIM), lambda bi, p: (bi, 0, p)),
        out_shape=jax.ShapeDtypeStruct((b, l, N_HEADS * V_HEAD_DIM), BF16),
        scratch_shapes=[
            pltpu.VMEM((2, 2, tk, tq), F32),
            pltpu.VMEM((2, 2, 1, tq), F32),
            pltpu.VMEM((2, 1, tq), F32),
            pltpu.VMEM((2, 2 * V_HEAD_DIM, tq), F32),
        ],
        compiler_params=pltpu.CompilerParams(
            dimension_semantics=("parallel", "parallel"), vmem_limit_bytes=VMEM_LIMIT),
        name="attention",
    )(q_all, k_all, v_t)


def _s5_kernel(u_ref, are_ref, aim_ref, wb_ref, wcr_ref, wci_ref, d_ref, wglu_ref, bglu_ref,
               o_ref, state_ref, sre_ref, sim_ref, tb_ref, *, steps, batch):
    @pl.when(pl.program_id(0) == 0)
    def _():
        state_ref[...] = jnp.zeros_like(state_ref)

    lane_groups = SSM_WIDTH // LANES
    for b in range(batch):
        for c in range(lane_groups):
            c0 = b * SSM_WIDTH + c * LANES
            tb_ref[c, pl.ds(b, steps, stride=batch), :] = u_ref[:, c0:c0 + LANES].astype(F32)
    u32 = jnp.concatenate([tb_ref[c] for c in range(lane_groups)], axis=1)
    u = u32.astype(BF16)
    for k in range(SSM_HALVES):
        bu = _dot(u[:, k * SSM_HALF_IN:(k + 1) * SSM_HALF_IN], wb_ref[k])
        sre_ref[k] = bu[:, :SSM_HALF_STATE]
        sim_ref[k] = bu[:, SSM_HALF_STATE:]

    nt = (((1,), (1,)), ((), ()))
    ys = []
    for k in range(SSM_HALVES):
        ar = are_ref[:, k * SSM_HALF_STATE:(k + 1) * SSM_HALF_STATE]
        ai = aim_ref[:, k * SSM_HALF_STATE:(k + 1) * SSM_HALF_STATE]
        sr = state_ref[k, 0]
        si = state_ref[k, 1]
        for t in range(steps):
            rs = slice(t * batch, (t + 1) * batch)
            nr = ar * sr - ai * si + sre_ref[k, rs, :]
            ni = ar * si + ai * sr + sim_ref[k, rs, :]
            sre_ref[k, rs, :] = nr
            sim_ref[k, rs, :] = ni
            sr, si = nr, ni
        state_ref[k, 0] = sr
        state_ref[k, 1] = si
        ys.append(lax.dot_general(sre_ref[k].astype(BF16), wcr_ref[k], nt, preferred_element_type=F32)
                  - lax.dot_general(sim_ref[k].astype(BF16), wci_ref[k], nt, preferred_element_type=F32))

    y = jnp.concatenate(ys, axis=1) + d_ref[...] * u32
    g = _gelu_tanh(y)
    z = _dot(g.astype(BF16), wglu_ref[...]) + bglu_ref[...]
    out = g * jax.nn.sigmoid(z)
    for c in range(lane_groups):
        tb_ref[c] = out[:, c * LANES:(c + 1) * LANES]
    for b in range(batch):
        for c in range(lane_groups):
            c0 = b * SSM_WIDTH + c * LANES
            o_ref[:, c0:c0 + LANES] = tb_ref[c, pl.ds(b, steps, stride=batch), :].astype(BF16)


def _s5(u_tm, a_re_t, a_im_t, wb, wcr, wci, d_skip, w_glu, b_glu, batch):
    length = u_tm.shape[0]
    width = SSM_WIDTH
    steps = SSM_CHUNK
    rows = steps * batch
    const2 = lambda i: (0, 0)
    const3 = lambda i: (0, 0, 0)
    return pl.pallas_call(
        functools.partial(_s5_kernel, steps=steps, batch=batch),
        grid=(length // steps,),
        in_specs=[
            pl.BlockSpec((steps, batch * width), lambda i: (i, 0)),
            pl.BlockSpec(a_re_t.shape, const2),
            pl.BlockSpec(a_im_t.shape, const2),
            pl.BlockSpec(wb.shape, const3),
            pl.BlockSpec(wcr.shape, const3),
            pl.BlockSpec(wci.shape, const3),
            pl.BlockSpec((1, width), const2),
            pl.BlockSpec((width, width), const2),
            pl.BlockSpec((1, width), const2),
        ],
        out_specs=pl.BlockSpec((steps, batch * width), lambda i: (i, 0)),
        out_shape=jax.ShapeDtypeStruct((length, batch * width), BF16),
        scratch_shapes=[
            pltpu.VMEM((SSM_HALVES, 2, batch, SSM_HALF_STATE), F32),
            pltpu.VMEM((SSM_HALVES, rows, SSM_HALF_STATE), F32),
            pltpu.VMEM((SSM_HALVES, rows, SSM_HALF_STATE), F32),
            pltpu.VMEM((width // LANES, rows, LANES), F32),
        ],
        compiler_params=pltpu.CompilerParams(
            dimension_semantics=("arbitrary",), vmem_limit_bytes=VMEM_LIMIT),
        name="s5",
    )(u_tm, a_re_t, a_im_t, wb, wcr, wci, d_skip, w_glu, b_glu)


def _merge_kernel(x_ref, attn_ref, ssm_ref, gpre_ref, wgate_ref, bgate_ref, wba_ref, wbs_ref, wout_ref,
                  gpost_ref, o_ref, perm_ref):
    tl, d = x_ref.shape
    n = tl // FFN_PHASES
    sub = tl // MERGE_SUBTILES
    ns = n // MERGE_SUBTILES

    def mixed(st):
        rows = slice(st * sub, (st + 1) * sub)
        hn = _rms(x_ref[rows, :], gpre_ref[...]).astype(BF16)
        gates = jax.nn.sigmoid(_dot(hn, wgate_ref[...]) + bgate_ref[...])
        merged = (gates[:, :d] * _dot(attn_ref[rows, :], wba_ref[...])
                  + gates[:, d:] * _dot(ssm_ref[rows, :], wbs_ref[...]))
        return merged.astype(BF16)

    def finish(st, m2):
        rows = slice(st * sub, (st + 1) * sub)
        out = x_ref[rows, :] + _rms(m2, gpost_ref[...])
        for c in range(d // LANES):
            perm_ref[st, c] = out[:, c * LANES:(c + 1) * LANES]
        for c in range(d // LANES):
            for b in range(FFN_PHASES):
                r0 = b * n + st * ns
                o_ref[r0:r0 + ns, c * LANES:(c + 1) * LANES] = perm_ref[st, c, pl.ds(b, ns, stride=FFN_PHASES), :]

    pending = None
    for st in range(MERGE_SUBTILES):
        merged = mixed(st)
        if pending is not None:
            finish(*pending)
        pending = (st, _dot(merged, wout_ref[...]))
    finish(*pending)


def _merge(x, attn, ssm_tm, g_pre, w_gate, b_gate, w_ba, w_bs, w_out, g_post):
    b, l, d = x.shape
    tl = ROW_TILE
    const = lambda *_: (0, 0)
    return pl.pallas_call(
        _merge_kernel,
        grid=(b, l // tl),
        in_specs=[
            pl.BlockSpec((None, tl, d), lambda bi, i: (bi, i, 0)),
            pl.BlockSpec((None, tl, N_HEADS * V_HEAD_DIM), lambda bi, i: (bi, i, 0)),
            pl.BlockSpec((tl, SSM_WIDTH), lambda bi, i: (i, bi)),
            pl.BlockSpec((1, d), const),
            pl.BlockSpec(w_gate.shape, const),
            pl.BlockSpec((1, 2 * d), const),
            pl.BlockSpec(w_ba.shape, const),
            pl.BlockSpec(w_bs.shape, const),
            pl.BlockSpec(w_out.shape, const),
            pl.BlockSpec((1, d), const),
        ],
        out_specs=pl.BlockSpec((None, tl, d), lambda bi, i: (bi, i, 0)),
        out_shape=jax.ShapeDtypeStruct((b, l, d), F32),
        scratch_shapes=[pltpu.VMEM((MERGE_SUBTILES, d // LANES, tl // MERGE_SUBTILES, LANES), F32)],
        compiler_params=pltpu.CompilerParams(
            dimension_semantics=("parallel", "parallel"), vmem_limit_bytes=VMEM_LIMIT),
        name="merge",
    )(x, attn, ssm_tm, g_pre, w_gate, b_gate, w_ba, w_bs, w_out, g_post)


def _ffn_kernel(x_ref, gpre_ref, wup_ref, cw_ref, cb_ref, wdown_ref, gpost_ref, o_ref,
                act_ref, perm_ref, tail_ref):
    i = pl.program_id(1)
    tl, d = x_ref.shape
    n = tl // FFN_PHASES
    lane_groups = d // LANES

    @pl.when(i == 0)
    def _():
        tail_ref[...] = jnp.zeros(tail_ref.shape, F32)

    x1 = x_ref[...]
    hn = _rms(x1, gpre_ref[...]).astype(BF16)

    first_row = lax.broadcasted_iota(jnp.int32, (n, 1), 0) == 0

    def conv(h, c0):
        cols = slice(c0, c0 + FFN_COL_CHUNK)
        blocks = [h[b * n:(b + 1) * n] for b in range(FFN_PHASES)]

        def moved_down(block, prev_token):
            return jnp.where(first_row, prev_token, pltpu.roll(block, 1, 0))

        s1 = moved_down(blocks[FFN_PHASES - 1], tail_ref[1:2, cols])
        s2 = moved_down(blocks[FFN_PHASES - 2], tail_ref[0:1, cols])
        tail_ref[0:1, cols] = blocks[FFN_PHASES - 2][n - 1:n]
        tail_ref[1:2, cols] = blocks[FFN_PHASES - 1][n - 1:n]
        tap1 = [s1] + blocks[:-1]
        tap2 = [s2, s1] + blocks[:-2]
        w0, w1, w2, bias = cw_ref[0:1, cols], cw_ref[1:2, cols], cw_ref[2:3, cols], cb_ref[:, cols]
        return jnp.concatenate(
            [bias + w2 * blocks[b] + w1 * tap1[b] + w0 * tap2[b] for b in range(FFN_PHASES)], axis=0)

    for j in range(D_FF // FFN_COL_CHUNK):
        c0 = j * FFN_COL_CHUNK
        hg = _dot(hn, wup_ref[:, c0:c0 + FFN_COL_CHUNK])
        hv = _dot(hn, wup_ref[:, D_FF + c0:D_FF + c0 + FFN_COL_CHUNK])
        act = _gelu_tanh(conv(hg, c0)) * conv(hv, D_FF + c0)
        act_ref[:, c0:c0 + FFN_COL_CHUNK] = act.astype(BF16)

    ff = _dot(act_ref[...], wdown_ref[...])
    out = x1 + _rms(ff, gpost_ref[...])

    for c in range(lane_groups):
        for b in range(FFN_PHASES):
            perm_ref[c, pl.ds(b, n, stride=FFN_PHASES), :] = out[b * n:(b + 1) * n, c * LANES:(c + 1) * LANES]
    for c in range(lane_groups):
        o_ref[:, c * LANES:(c + 1) * LANES] = perm_ref[c]


def _conv_ffn(x1, g_pre, w_up, conv_w, conv_b, w_down, g_post):
    b, l, d = x1.shape
    tl = ROW_TILE
    const = lambda *_: (0, 0)
    return pl.pallas_call(
        _ffn_kernel,
        grid=(b, l // tl),
        in_specs=[
            pl.BlockSpec((None, tl, d), lambda bi, i: (bi, i, 0)),
            pl.BlockSpec((1, d), const),
            pl.BlockSpec(w_up.shape, const, pipeline_mode=pl.Buffered(1)),
            pl.BlockSpec(conv_w.shape, const),
            pl.BlockSpec(conv_b.shape, const),
            pl.BlockSpec(w_down.shape, const, pipeline_mode=pl.Buffered(1)),
            pl.BlockSpec((1, d), const),
        ],
        out_specs=pl.BlockSpec((None, tl, d), lambda bi, i: (bi, i, 0)),
        out_shape=jax.ShapeDtypeStruct((b, l, d), F32),
        scratch_shapes=[pltpu.VMEM((tl, D_FF), BF16),
                        pltpu.VMEM((d // LANES, tl, LANES), F32),
                        pltpu.VMEM((CONV_WIDTH - 1, 2 * D_FF), F32)],
        compiler_params=pltpu.CompilerParams(
            dimension_semantics=("parallel", "arbitrary"), vmem_limit_bytes=VMEM_LIMIT),
        name="conv_ffn",
    )(x1, g_pre, w_up, conv_w, conv_b, w_down, g_post)


def _head_lane_groups(nope, rope):
    k = (nope if nope is not None else rope).shape[0]
    nope = jnp.zeros((k, N_HEADS, QK_NOPE_DIM), BF16) if nope is None else nope
    rope = jnp.zeros((k, N_HEADS, QK_ROPE_DIM), BF16) if rope is None else rope
    return jnp.concatenate([nope, rope, rope], axis=2).astype(BF16).reshape(k, -1)


def _layer(x, pos_f, p):
    b, l, d = x.shape
    row = lambda v: v.reshape(1, -1)

    w_in = p["w_in"]
    off_u = P_KR + QK_ROPE_DIM
    off_gate = off_u + SSM_WIDTH
    w_kr = w_in[:, P_KR:off_u]
    w_low = jnp.concatenate([w_in[:, :P_KR], jnp.zeros((d, QK_NOPE_DIM), w_in.dtype), w_kr, w_kr],
                            axis=1).astype(BF16)
    w_u = w_in[:, off_u:off_gate].astype(BF16)
    w_gate = w_in[:, off_gate:].astype(BF16)
    w_uq = p["w_uq"].reshape(Q_LORA_RANK, N_HEADS, QK_HEAD_DIM)
    w_uq_p = _head_lane_groups(w_uq[:, :, :QK_NOPE_DIM], w_uq[:, :, QK_NOPE_DIM:])
    w_uk_p = _head_lane_groups(p["w_uk"].reshape(KV_LORA_RANK, N_HEADS, QK_NOPE_DIM), None)

    q_all, k_all, v_t, u_tm = _in_proj(
        x, pos_f, row(p["mix_norm_pre"]), w_low, w_u, row(p["q_norm"]), w_uq_p, row(p["kv_norm"]),
        w_uk_p, p["w_uv"].T.astype(BF16))

    attn = _attention(q_all, k_all, v_t)

    a_re_t, a_im_t, wb, wcr, wci = _s5_prep(
        p["ssm_lambda_re"], p["ssm_lambda_im"], p["ssm_log_dt"], p["ssm_b_re"], p["ssm_b_im"],
        p["ssm_c_re"], p["ssm_c_im"], b)
    ssm_tm = _s5(u_tm, a_re_t, a_im_t, wb, wcr, wci,
                 row(p["ssm_d"]), p["w_glu"].astype(BF16), row(p["b_glu"]), b)

    x1 = _merge(x, attn, ssm_tm, row(p["mix_norm_pre"]), w_gate, row(p["b_gate"]),
                p["w_branch_attn"].astype(BF16), p["w_branch_ssm"].astype(BF16),
                p["w_out"].astype(BF16), row(p["mix_norm_post"]))
    return _conv_ffn(x1, row(p["ffn_norm_pre"]), p["w_up"].astype(BF16), p["conv_w"], row(p["conv_b"]),
                     p["w_down"].astype(BF16), row(p["ffn_norm_post"]))


def kernel(x, positions, mix_norm_pre, w_in, q_norm, w_uq, kv_norm, w_uk, w_uv, ssm_lambda_re, ssm_lambda_im, ssm_log_dt, ssm_b_re, ssm_b_im, ssm_c_re, ssm_c_im, ssm_d, w_glu, b_glu, w_branch_attn, w_branch_ssm, b_gate, w_out, mix_norm_post, ffn_norm_pre, w_up, conv_w, conv_b, w_down, ffn_norm_post):
    b, l, _ = x.shape
    params = dict(mix_norm_pre=mix_norm_pre, w_in=w_in, q_norm=q_norm, w_uq=w_uq, kv_norm=kv_norm,
                  w_uk=w_uk, w_uv=w_uv, ssm_lambda_re=ssm_lambda_re, ssm_lambda_im=ssm_lambda_im,
                  ssm_log_dt=ssm_log_dt, ssm_b_re=ssm_b_re, ssm_b_im=ssm_b_im, ssm_c_re=ssm_c_re,
                  ssm_c_im=ssm_c_im, ssm_d=ssm_d, w_glu=w_glu, b_glu=b_glu, w_branch_attn=w_branch_attn,
                  w_branch_ssm=w_branch_ssm, b_gate=b_gate, w_out=w_out, mix_norm_post=mix_norm_post,
                  ffn_norm_pre=ffn_norm_pre, w_up=w_up, conv_w=conv_w, conv_b=conv_b, w_down=w_down,
                  ffn_norm_post=ffn_norm_post)
    pos_f = positions.astype(F32).reshape(b, 1, l)
    for layer in range(mix_norm_pre.shape[0]):
        x = _layer(x, pos_f, {k: v[layer] for k, v in params.items()})
    return x
```

```python
import functools
import math

import jax
import jax.numpy as jnp
from jax import lax
from jax.experimental import pallas as pl
from jax.experimental.pallas import tpu as pltpu

D_MODEL = 1024
N_HEADS = 8
QK_NOPE_DIM = 64
QK_ROPE_DIM = 32
QK_HEAD_DIM = QK_NOPE_DIM + QK_ROPE_DIM
V_HEAD_DIM = 64
Q_LORA_RANK = 384
KV_LORA_RANK = 256
ROPE_THETA = 10000.0
SSM_WIDTH = 512
SSM_GROUP = 16
SSM_GROUPS = SSM_WIDTH // SSM_GROUP
SSM_STATE = 64
D_FF = 2816
CONV_WIDTH = 3
EPS = 1e-6

LANES = 128
HEAD_PAD = LANES
ROPE_LANE0 = QK_NOPE_DIM
ROPE_HALF = QK_ROPE_DIM // 2
N_PAIRS = N_HEADS // 2
SUM_ROW = (V_HEAD_DIM, 0)
SSM_HALVES = 2
SSM_HALF_IN = SSM_WIDTH // SSM_HALVES
SSM_HALF_STATE = SSM_GROUPS * SSM_STATE // SSM_HALVES

P_CQ = 0
P_CKV = P_CQ + Q_LORA_RANK
P_KR = P_CKV + KV_LORA_RANK
P_U = P_KR + LANES

Q_SCALE = (1.0 / math.sqrt(QK_HEAD_DIM)) * math.log2(math.e)
NEG_BIG = -1e30

ROW_TILE = 512
ATTN_TILE = ROW_TILE
ATTN_KEY_BLOCK = ATTN_TILE // 2
SSM_CHUNK = 64
MERGE_SUBTILES = 2
FFN_COL_CHUNK = 256
FFN_PHASES = 8
VMEM_LIMIT = 56 * 1024 * 1024

BF16 = jnp.bfloat16
F32 = jnp.float32


def _rms(x, g):
    return x * lax.rsqrt(jnp.mean(x * x, axis=-1, keepdims=True) + EPS) * g


def _dot(a, b):
    return jnp.dot(a, b, preferred_element_type=F32)


GELU_C = math.sqrt(2.0 / math.pi)


def _gelu_tanh(x):
    k = -2.0 * GELU_C * math.log2(math.e)
    e = jnp.exp2(x * (x * x * (k * 0.044715) + k))
    return x / (1.0 + e)


def _s5_prep_kernel(lr_ref, li_ref, logdt_ref, br_ref, bi_ref, cr_ref, ci_ref,
                    are_ref, aim_ref, wb_ref, wcr_ref, wci_ref, *, batch):
    lr = lr_ref[...]
    li = li_ref[...]
    dt = jnp.exp(logdt_ref[...])
    mag = jnp.exp(lr * dt)
    ang = li * dt
    a_re = mag * jnp.cos(ang)
    a_im = mag * jnp.sin(ang)
    den = lr * lr + li * li
    n_re = a_re - 1.0
    n_im = a_im
    z_re = (n_re * lr + n_im * li) / den
    z_im = (n_im * lr - n_re * li) / den
    are_ref[...] = jnp.broadcast_to(a_re, (batch, a_re.shape[1]))
    aim_ref[...] = jnp.broadcast_to(a_im, (batch, a_im.shape[1]))
    br = br_ref[...]
    bi = bi_ref[...]
    bb_re = z_re * br - z_im * bi
    bb_im = z_re * bi + z_im * br

    groups_per_half = SSM_GROUPS // SSM_HALVES
    row_group = jnp.right_shift(lax.broadcasted_iota(jnp.int32, (SSM_HALF_IN, SSM_HALF_STATE), 0),
                                SSM_GROUP.bit_length() - 1)
    col_group = jnp.right_shift(lax.broadcasted_iota(jnp.int32, (SSM_HALF_IN, SSM_HALF_STATE), 1),
                                SSM_STATE.bit_length() - 1)
    on_diag = row_group == col_group

    def diag(v):
        return jnp.where(on_diag, jnp.concatenate([v] * groups_per_half, axis=0), 0.0)

    for k in range(SSM_HALVES):
        cols = slice(k * SSM_HALF_STATE, (k + 1) * SSM_HALF_STATE)
        wb_ref[k, :, :SSM_HALF_STATE] = diag(bb_re[:, cols]).astype(BF16)
        wb_ref[k, :, SSM_HALF_STATE:] = diag(bb_im[:, cols]).astype(BF16)
        wcr_ref[k] = diag(cr_ref[:, cols]).astype(BF16)
        wci_ref[k] = diag(ci_ref[:, cols]).astype(BF16)


def _s5_prep(lam_re, lam_im, log_dt, b_re, b_im, c_re, c_im, batch):
    n = SSM_GROUPS * SSM_STATE
    lanes = lambda v: v.reshape(1, n)
    chan_major = lambda v, perm: v.transpose(perm).reshape(SSM_GROUP, n)
    return pl.pallas_call(
        functools.partial(_s5_prep_kernel, batch=batch),
        out_shape=(jax.ShapeDtypeStruct((batch, n), F32), jax.ShapeDtypeStruct((batch, n), F32),
                   jax.ShapeDtypeStruct((SSM_HALVES, SSM_HALF_IN, 2 * SSM_HALF_STATE), BF16),
                   jax.ShapeDtypeStruct((SSM_HALVES, SSM_HALF_IN, SSM_HALF_STATE), BF16),
                   jax.ShapeDtypeStruct((SSM_HALVES, SSM_HALF_IN, SSM_HALF_STATE), BF16)),
        name="s5_prep",
    )(lanes(lam_re), lanes(lam_im), lanes(jnp.repeat(log_dt, SSM_STATE)),
      chan_major(b_re, (2, 0, 1)), chan_major(b_im, (2, 0, 1)),
      chan_major(c_re, (1, 0, 2)), chan_major(c_im, (1, 0, 2)))


def _in_proj_kernel(x_ref, pos_ref, gpre_ref, wlow_ref, wu_ref, qn_ref, wuq_ref, kvn_ref, wuk_ref,
                    wuv_ref, q_ref, k_ref, vt_ref, u_ref):
    x = x_ref[...]
    tl = x.shape[0]
    hn = _rms(x, gpre_ref[...]).astype(BF16)

    freq_idx = lax.broadcasted_iota(jnp.int32, (ROPE_HALF, 1), 0).astype(F32)
    inv_freq = jnp.exp(freq_idx * (-2.0 * math.log(ROPE_THETA) / QK_ROPE_DIM))
    ang = inv_freq * pos_ref[...]
    c16 = jnp.cos(ang)
    s16 = jnp.sin(ang)
    ones = jnp.ones((ROPE_LANE0, tl), F32)
    zeros = lambda n: jnp.zeros((n, tl), F32)
    cos = jnp.concatenate([ones, zeros(ROPE_HALF), c16, c16, zeros(ROPE_HALF)], axis=0).T
    sin = jnp.concatenate([zeros(ROPE_LANE0 + ROPE_HALF), s16, -s16, zeros(ROPE_HALF)], axis=0).T
    cos_q = cos * Q_SCALE
    sin_q = sin * Q_SCALE

    low = _dot(hn, wlow_ref[...])
    cq = low[:, P_CQ:P_CKV]
    ckv = low[:, P_CKV:P_KR]
    kr = low[:, P_KR:P_U]
    cqn = _rms(cq, qn_ref[...]).astype(BF16)
    q = _dot(cqn, wuq_ref[...])
    for h in range(N_HEADS):
        sl = slice(h * HEAD_PAD, (h + 1) * HEAD_PAD)
        t = q[:, sl]
        q_ref[:, sl] = (t * cos_q + pltpu.roll(t, ROPE_HALF, 1) * sin_q).astype(BF16)

    ckvn = _rms(ckv, kvn_ref[...]).astype(BF16)
    v_t = lax.dot_general(wuv_ref[...], ckvn, (((1,), (1,)), ((), ())),
                          preferred_element_type=F32).astype(BF16)
    for c in range(vt_ref.shape[0]):
        vt_ref[c] = v_t[:, c * ATTN_KEY_BLOCK:(c + 1) * ATTN_KEY_BLOCK]
    kn = _dot(ckvn, wuk_ref[...])
    kr = kr * cos + pltpu.roll(kr, ROPE_HALF, 1) * sin
    for h in range(N_HEADS):
        sl = slice(h * HEAD_PAD, (h + 1) * HEAD_PAD)
        k_ref[:, sl] = (kn[:, sl] + kr).astype(BF16)

    u_ref[...] = _dot(hn, wu_ref[...]).astype(BF16)


def _in_proj(x, pos_f, g_pre, w_low, w_u, q_norm, w_uq_p, kv_norm, w_uk_p, w_uv):
    b, l, d = x.shape
    tl = ROW_TILE
    const = lambda *_: (0, 0)
    return pl.pallas_call(
        _in_proj_kernel,
        grid=(b, l // tl),
        in_specs=[
            pl.BlockSpec((None, tl, d), lambda bi, i: (bi, i, 0)),
            pl.BlockSpec((None, 1, tl), lambda bi, i: (bi, 0, i)),
            pl.BlockSpec((1, d), const),
            pl.BlockSpec(w_low.shape, const),
            pl.BlockSpec(w_u.shape, const),
            pl.BlockSpec((1, Q_LORA_RANK), const),
            pl.BlockSpec((Q_LORA_RANK, N_HEADS * HEAD_PAD), const),
            pl.BlockSpec((1, KV_LORA_RANK), const),
            pl.BlockSpec((KV_LORA_RANK, N_HEADS * HEAD_PAD), const),
            pl.BlockSpec((N_HEADS * V_HEAD_DIM, KV_LORA_RANK), const),
        ],
        out_specs=[
            pl.BlockSpec((None, tl, N_HEADS * HEAD_PAD), lambda bi, i: (bi, i, 0)),
            pl.BlockSpec((None, tl, N_HEADS * HEAD_PAD), lambda bi, i: (bi, i, 0)),
            pl.BlockSpec((None, tl // ATTN_KEY_BLOCK, N_HEADS * V_HEAD_DIM, ATTN_KEY_BLOCK),
                         lambda bi, i: (bi, i, 0, 0)),
            pl.BlockSpec((tl, SSM_WIDTH), lambda bi, i: (i, bi)),
        ],
        out_shape=[
            jax.ShapeDtypeStruct((b, l, N_HEADS * HEAD_PAD), BF16),
            jax.ShapeDtypeStruct((b, l, N_HEADS * HEAD_PAD), BF16),
            jax.ShapeDtypeStruct((b, l // ATTN_KEY_BLOCK, N_HEADS * V_HEAD_DIM, ATTN_KEY_BLOCK), BF16),
            jax.ShapeDtypeStruct((l, b * SSM_WIDTH), BF16),
        ],
        compiler_params=pltpu.CompilerParams(
            dimension_semantics=("parallel", "parallel"), vmem_limit_bytes=VMEM_LIMIT),
        name="in_proj",
    )(x, pos_f, g_pre, w_low, w_u, q_norm, w_uq_p, kv_norm, w_uk_p, w_uv)


def _attn_kernel(q_ref, k_ref, vt_ref, o_ref, s_ref, mx_ref, m_ref, acc_ref, *, tq, tk):
    n_q = q_ref.shape[0] // tq
    key = lax.broadcasted_iota(jnp.int32, (tk, tq), 0)
    qry = lax.broadcasted_iota(jnp.int32, (tk, tq), 1)
    diag_mask = key <= qry
    vrow = lax.broadcasted_iota(jnp.int32, (2 * V_HEAD_DIM, tk), 0)
    one = jnp.ones((), BF16)

    def query_tile(i):
        def scores(j, slot, mask=None, tile=None, cols=slice(None)):
            r0 = j * tk
            nq = tq if cols.start is None else cols.stop - cols.start
            q0 = (i if tile is None else tile) * tq + (cols.start or 0)
            for hh in range(2):
                lanes = slice(hh * HEAD_PAD, (hh + 1) * HEAD_PAD)
                kj = k_ref[pl.ds(r0, tk), lanes]
                qt = q_ref[pl.ds(q0, nq), lanes]
                s = lax.dot_general(kj, qt, (((1,), (1,)), ((), ())), preferred_element_type=F32)
                if mask is not None:
                    s = jnp.where(mask, s, NEG_BIG)
                s_ref[slot, hh, :, cols] = s
                mx_ref[slot, hh, :, cols] = jnp.max(s, axis=0, keepdims=True)

        def softmax_pv(j, slot, cols=slice(None)):
            vt = vt_ref[j]
            for hh in range(2):
                m = m_ref[hh, :, cols]
                m_new = jnp.maximum(m, mx_ref[slot, hh, :, cols])
                alpha = jnp.exp2(m - m_new)
                p = jnp.exp2(s_ref[slot, hh, :, cols] - m_new)
                vt_h = jnp.where(vrow == SUM_ROW[hh], one, vt)
                acc_ref[hh, :, cols] = alpha * acc_ref[hh, :, cols] + _dot(vt_h, p.astype(BF16))
                m_ref[hh, :, cols] = m_new

        m_ref[...] = jnp.full(m_ref.shape, NEG_BIG, F32)
        acc_ref[...] = jnp.zeros(acc_ref.shape, F32)

        def two_blocks(t, last=False):
            scores(2 * t + 1, 1)
            softmax_pv(2 * t, 0)
            scores(2 * t + 2, 0, diag_mask if last else None)
            softmax_pv(2 * t + 1, 1)

        for t in range(max(i - 1, 0)):
            two_blocks(t)

        def diagonal_tile():
            late = slice(tk, tq)
            scores(2 * i + 1, 1, diag_mask[:, :tq - tk], cols=late)
            softmax_pv(2 * i, 0)
            scores(0, 0, tile=min(i + 1, n_q - 1))
            softmax_pv(2 * i + 1, 1, cols=late)

        if i > 0:
            two_blocks(i - 1, last=True)
        else:
            scores(0, 0, diag_mask)
        diagonal_tile()

        a0 = acc_ref[0]
        a1 = acc_ref[1]
        o0 = a0[:V_HEAD_DIM] / a0[SUM_ROW[0]:SUM_ROW[0] + 1]
        o1 = a1[V_HEAD_DIM:] / a1[SUM_ROW[1]:SUM_ROW[1] + 1]
        out_t = jnp.concatenate([o0, o1], axis=0)
        o_ref[i * tq:(i + 1) * tq, :] = out_t.T.astype(BF16)

    for i in range(n_q):
        query_tile(i)


def _attention(q_all, k_all, v_t):
    b, l, _ = q_all.shape
    tq, tk = ATTN_TILE, ATTN_KEY_BLOCK
    return pl.pallas_call(
        functools.partial(_attn_kernel, tq=tq, tk=tk),
        grid=(b, N_PAIRS),
        in_specs=[
            pl.BlockSpec((None, l, 2 * HEAD_PAD), lambda bi, p: (bi, 0, p)),
            pl.BlockSpec((None, l, 2 * HEAD_PAD), lambda bi, p: (bi, 0, p)),
            pl.BlockSpec((None, l // tk, 2 * V_HEAD_DIM, tk), lambda bi, p: (bi, 0, p, 0)),
        ],
        out_specs=pl.BlockSpec((None, l, 2 * V_HEAD_DIM), lambda bi, p: (bi, 0, p)),
        out_shape=jax.ShapeDtypeStruct((b, l, N_HEADS * V_HEAD_DIM), BF16),
        scratch_shapes=[
            pltpu.VMEM((2, 2, tk, tq), F32),
            pltpu.VMEM((2, 2, 1, tq), F32),
            pltpu.VMEM((2, 1, tq), F32),
            pltpu.VMEM((2, 2 * V_HEAD_DIM, tq), F32),
        ],
        compiler_params=pltpu.CompilerParams(
            dimension_semantics=("parallel", "parallel"), vmem_limit_bytes=VMEM_LIMIT),
        name="attention",
    )(q_all, k_all, v_t)


def _s5_kernel(u_ref, are_ref, aim_ref, wb_ref, wcr_ref, wci_ref, d_ref, wglu_ref, bglu_ref,
               o_ref, state_ref, sre_ref, sim_ref, tb_ref, *, steps, batch):
    @pl.when(pl.program_id(0) == 0)
    def _():
        state_ref[...] = jnp.zeros_like(state_ref)

    lane_groups = SSM_WIDTH // LANES
    for b in range(batch):
        for c in range(lane_groups):
            c0 = b * SSM_WIDTH + c * LANES
            tb_ref[c, pl.ds(b, steps, stride=batch), :] = u_ref[:, c0:c0 + LANES].astype(F32)
    u32 = jnp.concatenate([tb_ref[c] for c in range(lane_groups)], axis=1)
    u = u32.astype(BF16)
    for k in range(SSM_HALVES):
        bu = _dot(u[:, k * SSM_HALF_IN:(k + 1) * SSM_HALF_IN], wb_ref[k])
        sre_ref[k] = bu[:, :SSM_HALF_STATE]
        sim_ref[k] = bu[:, SSM_HALF_STATE:]

    nt = (((1,), (1,)), ((), ()))
    ys = []
    for k in range(SSM_HALVES):
        ar = are_ref[:, k * SSM_HALF_STATE:(k + 1) * SSM_HALF_STATE]
        ai = aim_ref[:, k * SSM_HALF_STATE:(k + 1) * SSM_HALF_STATE]
        sr = state_ref[k, 0]
        si = state_ref[k, 1]
        for t in range(steps):
            rs = slice(t * batch, (t + 1) * batch)
            nr = ar * sr - ai * si + sre_ref[k, rs, :]
            ni = ar * si + ai * sr + sim_ref[k, rs, :]
            sre_ref[k, rs, :] = nr
            sim_ref[k, rs, :] = ni
            sr, si = nr, ni
        state_ref[k, 0] = sr
        state_ref[k, 1] = si
        ys.append(lax.dot_general(sre_ref[k].astype(BF16), wcr_ref[k], nt, preferred_element_type=F32)
                  - lax.dot_general(sim_ref[k].astype(BF16), wci_ref[k], nt, preferred_element_type=F32))

    y = jnp.concatenate(ys, axis=1) + d_ref[...] * u32
    g = _gelu_tanh(y)
    z = _dot(g.astype(BF16), wglu_ref[...]) + bglu_ref[...]
    out = g * jax.nn.sigmoid(z)
    for c in range(lane_groups):
        tb_ref[c] = out[:, c * LANES:(c + 1) * LANES]
    for b in range(batch):
        for c in range(lane_groups):
            c0 = b * SSM_WIDTH + c * LANES
            o_ref[:, c0:c0 + LANES] = tb_ref[c, pl.ds(b, steps, stride=batch), :].astype(BF16)


def _s5(u_tm, a_re_t, a_im_t, wb, wcr, wci, d_skip, w_glu, b_glu, batch):
    length = u_tm.shape[0]
    width = SSM_WIDTH
    steps = SSM_CHUNK
    rows = steps * batch
    const2 = lambda i: (0, 0)
    const3 = lambda i: (0, 0, 0)
    return pl.pallas_call(
        functools.partial(_s5_kernel, steps=steps, batch=batch),
        grid=(length // steps,),
        in_specs=[
            pl.BlockSpec((steps, batch * width), lambda i: (i, 0)),
            pl.BlockSpec(a_re_t.shape, const2),
            pl.BlockSpec(a_im_t.shape, const2),
            pl.BlockSpec(wb.shape, const3),
            pl.BlockSpec(wcr.shape, const3),
            pl.BlockSpec(wci.shape, const3),
            pl.BlockSpec((1, width), const2),
            pl.BlockSpec((width, width), const2),
            pl.BlockSpec((1, width), const2),
        ],
        out_specs=pl.BlockSpec((steps, batch * width), lambda i: (i, 0)),
        out_shape=jax.ShapeDtypeStruct((length, batch * width), BF16),
        scratch_shapes=[
            pltpu.VMEM((SSM_HALVES, 2, batch, SSM_HALF_STATE), F32),
            pltpu.VMEM((SSM_HALVES, rows, SSM_HALF_STATE), F32),
            pltpu.VMEM((SSM_HALVES, rows, SSM_HALF_STATE), F32),
            pltpu.VMEM((width // LANES, rows, LANES), F32),
        ],
        compiler_params=pltpu.CompilerParams(
            dimension_semantics=("arbitrary",), vmem_limit_bytes=VMEM_LIMIT),
        name="s5",
    )(u_tm, a_re_t, a_im_t, wb, wcr, wci, d_skip, w_glu, b_glu)


def _merge_kernel(x_ref, attn_ref, ssm_ref, gpre_ref, wgate_ref, bgate_ref, wba_ref, wbs_ref, wout_ref,
                  gpost_ref, o_ref, perm_ref):
    tl, d = x_ref.shape
    n = tl // FFN_PHASES
    sub = tl // MERGE_SUBTILES
    ns = n // MERGE_SUBTILES

    def mixed(st):
        rows = slice(st * sub, (st + 1) * sub)
        hn = _rms(x_ref[rows, :], gpre_ref[...]).astype(BF16)
        gates = jax.nn.sigmoid(_dot(hn, wgate_ref[...]) + bgate_ref[...])
        merged = (gates[:, :d] * _dot(attn_ref[rows, :], wba_ref[...])
                  + gates[:, d:] * _dot(ssm_ref[rows, :], wbs_ref[...]))
        return merged.astype(BF16)

    def finish(st, m2):
        rows = slice(st * sub, (st + 1) * sub)
        out = x_ref[rows, :] + _rms(m2, gpost_ref[...])
        for c in range(d // LANES):
            perm_ref[st, c] = out[:, c * LANES:(c + 1) * LANES]
        for c in range(d // LANES):
            for b in range(FFN_PHASES):
                r0 = b * n + st * ns
                o_ref[r0:r0 + ns, c * LANES:(c + 1) * LANES] = perm_ref[st, c, pl.ds(b, ns, stride=FFN_PHASES), :]

    pending = None
    for st in range(MERGE_SUBTILES):
        merged = mixed(st)
        if pending is not None:
            finish(*pending)
        pending = (st, _dot(merged, wout_ref[...]))
    finish(*pending)


def _merge(x, attn, ssm_tm, g_pre, w_gate, b_gate, w_ba, w_bs, w_out, g_post):
    b, l, d = x.shape
    tl = ROW_TILE
    const = lambda *_: (0, 0)
    return pl.pallas_call(
        _merge_kernel,
        grid=(b, l // tl),
        in_specs=[
            pl.BlockSpec((None, tl, d), lambda bi, i: (bi, i, 0)),
            pl.BlockSpec((None, tl, N_HEADS * V_HEAD_DIM), lambda bi, i: (bi, i, 0)),
            pl.BlockSpec((tl, SSM_WIDTH), lambda bi, i: (i, bi)),
            pl.BlockSpec((1, d), const),
            pl.BlockSpec(w_gate.shape, const),
            pl.BlockSpec((1, 2 * d), const),
            pl.BlockSpec(w_ba.shape, const),
            pl.BlockSpec(w_bs.shape, const),
            pl.BlockSpec(w_out.shape, const),
            pl.BlockSpec((1, d), const),
        ],
        out_specs=pl.BlockSpec((None, tl, d), lambda bi, i: (bi, i, 0)),
        out_shape=jax.ShapeDtypeStruct((b, l, d), F32),
        scratch_shapes=[pltpu.VMEM((MERGE_SUBTILES, d // LANES, tl // MERGE_SUBTILES, LANES), F32)],
        compiler_params=pltpu.CompilerParams(
            dimension_semantics=("parallel", "parallel"), vmem_limit_bytes=VMEM_LIMIT),
        name="merge",
    )(x, attn, ssm_tm, g_pre, w_gate, b_gate, w_ba, w_bs, w_out, g_post)


def _ffn_kernel(x_ref, gpre_ref, wup_ref, cw_ref, cb_ref, wdown_ref, gpost_ref, o_ref,
                act_ref, perm_ref, tail_ref):
    i = pl.program_id(1)
    tl, d = x_ref.shape
    n = tl // FFN_PHASES
    lane_groups = d // LANES

    @pl.when(i == 0)
    def _():
        tail_ref[...] = jnp.zeros(tail_ref.shape, F32)

    x1 = x_ref[...]
    hn = _rms(x1, gpre_ref[...]).astype(BF16)

    first_row = lax.broadcasted_iota(jnp.int32, (n, 1), 0) == 0

    def conv(h, c0):
        cols = slice(c0, c0 + FFN_COL_CHUNK)
        blocks = [h[b * n:(b + 1) * n] for b in range(FFN_PHASES)]

        def moved_down(block, prev_token):
            return jnp.where(first_row, prev_token, pltpu.roll(block, 1, 0))

        s1 = moved_down(blocks[FFN_PHASES - 1], tail_ref[1:2, cols])
        s2 = moved_down(blocks[FFN_PHASES - 2], tail_ref[0:1, cols])
        tail_ref[0:1, cols] = blocks[FFN_PHASES - 2][n - 1:n]
        tail_ref[1:2, cols] = blocks[FFN_PHASES - 1][n - 1:n]
        tap1 = [s1] + blocks[:-1]
        tap2 = [s2, s1] + blocks[:-2]
        w0, w1, w2, bias = cw_ref[0:1, cols], cw_ref[1:2, cols], cw_ref[2:3, cols], cb_ref[:, cols]
        return jnp.concatenate(
            [bias + w2 * blocks[b] + w1 * tap1[b] + w0 * tap2[b] for b in range(FFN_PHASES)], axis=0)

    for j in range(D_FF // FFN_COL_CHUNK):
        c0 = j * FFN_COL_CHUNK
        hg = _dot(hn, wup_ref[:, c0:c0 + FFN_COL_CHUNK])
        hv = _dot(hn, wup_ref[:, D_FF + c0:D_FF + c0 + FFN_COL_CHUNK])
        act = _gelu_tanh(conv(hg, c0)) * conv(hv, D_FF + c0)
        act_ref[:, c0:c0 + FFN_COL_CHUNK] = act.astype(BF16)

    ff = _dot(act_ref[...], wdown_ref[...])
    out = x1 + _rms(ff, gpost_ref[...])

    for c in range(lane_groups):
        for b in range(FFN_PHASES):
            perm_ref[c, pl.ds(b, n, stride=FFN_PHASES), :] = out[b * n:(b + 1) * n, c * LANES:(c + 1) * LANES]
    for c in range(lane_groups):
        o_ref[:, c * LANES:(c + 1) * LANES] = perm_ref[c]


def _conv_ffn(x1, g_pre, w_up, conv_w, conv_b, w_down, g_post):
    b, l, d = x1.shape
    tl = ROW_TILE
    const = lambda *_: (0, 0)
    return pl.pallas_call(
        _ffn_kernel,
        grid=(b, l // tl),
        in_specs=[
            pl.BlockSpec((None, tl, d), lambda bi, i: (bi, i, 0)),
            pl.BlockSpec((1, d), const),
            pl.BlockSpec(w_up.shape, const, pipeline_mode=pl.Buffered(1)),
            pl.BlockSpec(conv_w.shape, const),
            pl.BlockSpec(conv_b.shape, const),
            pl.BlockSpec(w_down.shape, const, pipeline_mode=pl.Buffered(1)),
            pl.BlockSpec((1, d), const),
        ],
        out_specs=pl.BlockSpec((None, tl, d), lambda bi, i: (bi, i, 0)),
        out_shape=jax.ShapeDtypeStruct((b, l, d), F32),
        scratch_shapes=[pltpu.VMEM((tl, D_FF), BF16),
                        pltpu.VMEM((d // LANES, tl, LANES), F32),
                        pltpu.VMEM((CONV_WIDTH - 1, 2 * D_FF), F32)],
        compiler_params=pltpu.CompilerParams(
            dimension_semantics=("parallel", "arbitrary"), vmem_limit_bytes=VMEM_LIMIT),
        name="conv_ffn",
    )(x1, g_pre, w_up, conv_w, conv_b, w_down, g_post)


def _head_lane_groups(nope, rope):
    k = (nope if nope is not None else rope).shape[0]
    nope = jnp.zeros((k, N_HEADS, QK_NOPE_DIM), BF16) if nope is None else nope
    rope = jnp.zeros((k, N_HEADS, QK_ROPE_DIM), BF16) if rope is None else rope
    return jnp.concatenate([nope, rope, rope], axis=2).astype(BF16).reshape(k, -1)


def _layer(x, pos_f, p):
    b, l, d = x.shape
    row = lambda v: v.reshape(1, -1)

    w_in = p["w_in"]
    off_u = P_KR + QK_ROPE_DIM
    off_gate = off_u + SSM_WIDTH
    w_kr = w_in[:, P_KR:off_u]
    w_low = jnp.concatenate([w_in[:, :P_KR], jnp.zeros((d, QK_NOPE_DIM), w_in.dtype), w_kr, w_kr],
                            axis=1).astype(BF16)
    w_u = w_in[:, off_u:off_gate].astype(BF16)
    w_gate = w_in[:, off_gate:].astype(BF16)
    w_uq = p["w_uq"].reshape(Q_LORA_RANK, N_HEADS, QK_HEAD_DIM)
    w_uq_p = _head_lane_groups(w_uq[:, :, :QK_NOPE_DIM], w_uq[:, :, QK_NOPE_DIM:])
    w_uk_p = _head_lane_groups(p["w_uk"].reshape(KV_LORA_RANK, N_HEADS, QK_NOPE_DIM), None)

    q_all, k_all, v_t, u_tm = _in_proj(
        x, pos_f, row(p["mix_norm_pre"]), w_low, w_u, row(p["q_norm"]), w_uq_p, row(p["kv_norm"]),
        w_uk_p, p["w_uv"].T.astype(BF16))

    attn = _attention(q_all, k_all, v_t)

    a_re_t, a_im_t, wb, wcr, wci = _s5_prep(
        p["ssm_lambda_re"], p["ssm_lambda_im"], p["ssm_log_dt"], p["ssm_b_re"], p["ssm_b_im"],
        p["ssm_c_re"], p["ssm_c_im"], b)
    ssm_tm = _s5(u_tm, a_re_t, a_im_t, wb, wcr, wci,
                 row(p["ssm_d"]), p["w_glu"].astype(BF16), row(p["b_glu"]), b)

    x1 = _merge(x, attn, ssm_tm, row(p["mix_norm_pre"]), w_gate, row(p["b_gate"]),
                p["w_branch_attn"].astype(BF16), p["w_branch_ssm"].astype(BF16),
                p["w_out"].astype(BF16), row(p["mix_norm_post"]))
    return _conv_ffn(x1, row(p["ffn_norm_pre"]), p["w_up"].astype(BF16), p["conv_w"], row(p["conv_b"]),
                     p["w_down"].astype(BF16), row(p["ffn_norm_post"]))


def kernel(x, positions, mix_norm_pre, w_in, q_norm, w_uq, kv_norm, w_uk, w_uv, ssm_lambda_re, ssm_lambda_im, ssm_log_dt, ssm_b_re, ssm_b_im, ssm_c_re, ssm_c_im, ssm_d, w_glu, b_glu, w_branch_attn, w_branch_ssm, b_gate, w_out, mix_norm_post, ffn_norm_pre, w_up, conv_w, conv_b, w_down, ffn_norm_post):
    b, l, _ = x.shape
    params = dict(mix_norm_pre=mix_norm_pre, w_in=w_in, q_norm=q_norm, w_uq=w_uq, kv_norm=kv_norm,
                  w_uk=w_uk, w_uv=w_uv, ssm_lambda_re=ssm_lambda_re, ssm_lambda_im=ssm_lambda_im,
                  ssm_log_dt=ssm_log_dt, ssm_b_re=ssm_b_re, ssm_b_im=ssm_b_im, ssm_c_re=ssm_c_re,
                  ssm_c_im=ssm_c_im, ssm_d=ssm_d, w_glu=w_glu, b_glu=b_glu, w_branch_attn=w_branch_attn,
                  w_branch_ssm=w_branch_ssm, b_gate=b_gate, w_out=w_out, mix_norm_post=mix_norm_post,
                  ffn_norm_pre=ffn_norm_pre, w_up=w_up, conv_w=conv_w, conv_b=conv_b, w_down=w_down,
                  ffn_norm_post=ffn_norm_post)
    pos_f = positions.astype(F32).reshape(b, 1, l)
    for layer in range(mix_norm_pre.shape[0]):
        x = _layer(x, pos_f, {k: v[layer] for k, v in params.items()})
    return x
```

```python
import functools
import math

import jax
import jax.numpy as jnp
from jax import lax
from jax.experimental import pallas as pl
from jax.experimental.pallas import tpu as pltpu

D_MODEL = 1024
N_HEADS = 8
QK_NOPE_DIM = 64
QK_ROPE_DIM = 32
QK_HEAD_DIM = QK_NOPE_DIM + QK_ROPE_DIM
V_HEAD_DIM = 64
Q_LORA_RANK = 384
KV_LORA_RANK = 256
ROPE_THETA = 10000.0
SSM_WIDTH = 512
SSM_GROUP = 16
SSM_GROUPS = SSM_WIDTH // SSM_GROUP
SSM_STATE = 64
D_FF = 2816
CONV_WIDTH = 3
EPS = 1e-6

LANES = 128
HEAD_PAD = LANES
ROPE_LANE0 = QK_NOPE_DIM
ROPE_HALF = QK_ROPE_DIM // 2
N_PAIRS = N_HEADS // 2
SUM_ROW = (V_HEAD_DIM, 0)
SSM_HALVES = 2
SSM_HALF_IN = SSM_WIDTH // SSM_HALVES
SSM_HALF_STATE = SSM_GROUPS * SSM_STATE // SSM_HALVES

P_CQ = 0
P_CKV = P_CQ + Q_LORA_RANK
P_KR = P_CKV + KV_LORA_RANK
P_U = P_KR + LANES

Q_SCALE = (1.0 / math.sqrt(QK_HEAD_DIM)) * math.log2(math.e)
NEG_BIG = -1e30

ROW_TILE = 512
ATTN_TILE = ROW_TILE
ATTN_KEY_BLOCK = ATTN_TILE // 2
SSM_CHUNK = 64
MERGE_SUBTILES = 2
FFN_COL_CHUNK = 256
FFN_PHASES = 8
VMEM_LIMIT = 56 * 1024 * 1024

BF16 = jnp.bfloat16
F32 = jnp.float32


def _rms(x, g):
    return x * lax.rsqrt(jnp.mean(x * x, axis=-1, keepdims=True) + EPS) * g


def _dot(a, b):
    return jnp.dot(a, b, preferred_element_type=F32)


GELU_C = math.sqrt(2.0 / math.pi)


def _gelu_tanh(x):
    k = -2.0 * GELU_C * math.log2(math.e)
    e = jnp.exp2(x * (x * x * (k * 0.044715) + k))
    return x / (1.0 + e)


def _s5_prep_kernel(lr_ref, li_ref, logdt_ref, br_ref, bi_ref, cr_ref, ci_ref,
                    are_ref, aim_ref, wb_ref, wcr_ref, wci_ref, *, batch):
    lr = lr_ref[...]
    li = li_ref[...]
    dt = jnp.exp(logdt_ref[...])
    mag = jnp.exp(lr * dt)
    ang = li * dt
    a_re = mag * jnp.cos(ang)
    a_im = mag * jnp.sin(ang)
    den = lr * lr + li * li
    n_re = a_re - 1.0
    n_im = a_im
    z_re = (n_re * lr + n_im * li) / den
    z_im = (n_im * lr - n_re * li) / den
    are_ref[...] = jnp.broadcast_to(a_re, (batch, a_re.shape[1]))
    aim_ref[...] = jnp.broadcast_to(a_im, (batch, a_im.shape[1]))
    br = br_ref[...]
    bi = bi_ref[...]
    bb_re = z_re * br - z_im * bi
    bb_im = z_re * bi + z_im * br

    groups_per_half = SSM_GROUPS // SSM_HALVES
    row_group = jnp.right_shift(lax.broadcasted_iota(jnp.int32, (SSM_HALF_IN, SSM_HALF_STATE), 0),
                                SSM_GROUP.bit_length() - 1)
    col_group = jnp.right_shift(lax.broadcasted_iota(jnp.int32, (SSM_HALF_IN, SSM_HALF_STATE), 1),
                                SSM_STATE.bit_length() - 1)
    on_diag = row_group == col_group

    def diag(v):
        return jnp.where(on_diag, jnp.concatenate([v] * groups_per_half, axis=0), 0.0)

    for k in range(SSM_HALVES):
        cols = slice(k * SSM_HALF_STATE, (k + 1) * SSM_HALF_STATE)
        wb_ref[k, :, :SSM_HALF_STATE] = diag(bb_re[:, cols]).astype(BF16)
        wb_ref[k, :, SSM_HALF_STATE:] = diag(bb_im[:, cols]).astype(BF16)
        wcr_ref[k] = diag(cr_ref[:, cols]).astype(BF16)
        wci_ref[k] = diag(ci_ref[:, cols]).astype(BF16)


def _s5_prep(lam_re, lam_im, log_dt, b_re, b_im, c_re, c_im, batch):
    n = SSM_GROUPS * SSM_STATE
    lanes = lambda v: v.reshape(1, n)
    chan_major = lambda v, perm: v.transpose(perm).reshape(SSM_GROUP, n)
    return pl.pallas_call(
        functools.partial(_s5_prep_kernel, batch=batch),
        out_shape=(jax.ShapeDtypeStruct((batch, n), F32), jax.ShapeDtypeStruct((batch, n), F32),
                   jax.ShapeDtypeStruct((SSM_HALVES, SSM_HALF_IN, 2 * SSM_HALF_STATE), BF16),
                   jax.ShapeDtypeStruct((SSM_HALVES, SSM_HALF_IN, SSM_HALF_STATE), BF16),
                   jax.ShapeDtypeStruct((SSM_HALVES, SSM_HALF_IN, SSM_HALF_STATE), BF16)),
        name="s5_prep",
    )(lanes(lam_re), lanes(lam_im), lanes(jnp.repeat(log_dt, SSM_STATE)),
      chan_major(b_re, (2, 0, 1)), chan_major(b_im, (2, 0, 1)),
      chan_major(c_re, (1, 0, 2)), chan_major(c_im, (1, 0, 2)))


def _in_proj_kernel(x_ref, pos_ref, gpre_ref, wlow_ref, wu_ref, qn_ref, wuq_ref, kvn_ref, wuk_ref,
                    wuv_ref, q_ref, k_ref, vt_ref, u_ref):
    x = x_ref[...]
    tl = x.shape[0]
    hn = _rms(x, gpre_ref[...]).astype(BF16)

    freq_idx = lax.broadcasted_iota(jnp.int32, (ROPE_HALF, 1), 0).astype(F32)
    inv_freq = jnp.exp(freq_idx * (-2.0 * math.log(ROPE_THETA) / QK_ROPE_DIM))
    ang = inv_freq * pos_ref[...]
    c16 = jnp.cos(ang)
    s16 = jnp.sin(ang)
    ones = jnp.ones((ROPE_LANE0, tl), F32)
    zeros = lambda n: jnp.zeros((n, tl), F32)
    cos = jnp.concatenate([ones, zeros(ROPE_HALF), c16, c16, zeros(ROPE_HALF)], axis=0).T
    sin = jnp.concatenate([zeros(ROPE_LANE0 + ROPE_HALF), s16, -s16, zeros(ROPE_HALF)], axis=0).T
    cos_q = cos * Q_SCALE
    sin_q = sin * Q_SCALE

    low = _dot(hn, wlow_ref[...])
    cq = low[:, P_CQ:P_CKV]
    ckv = low[:, P_CKV:P_KR]
    kr = low[:, P_KR:P_U]
    cqn = _rms(cq, qn_ref[...]).astype(BF16)
    q = _dot(cqn, wuq_ref[...])
    for h in range(N_HEADS):
        sl = slice(h * HEAD_PAD, (h + 1) * HEAD_PAD)
        t = q[:, sl]
        q_ref[:, sl] = (t * cos_q + pltpu.roll(t, ROPE_HALF, 1) * sin_q).astype(BF16)

    ckvn = _rms(ckv, kvn_ref[...]).astype(BF16)
    v_t = lax.dot_general(wuv_ref[...], ckvn, (((1,), (1,)), ((), ())),
                          preferred_element_type=F32).astype(BF16)
    for c in range(vt_ref.shape[0]):
        vt_ref[c] = v_t[:, c * ATTN_KEY_BLOCK:(c + 1) * ATTN_KEY_BLOCK]
    kn = _dot(ckvn, wuk_ref[...])
    kr = kr * cos + pltpu.roll(kr, ROPE_HALF, 1) * sin
    for h in range(N_HEADS):
        sl = slice(h * HEAD_PAD, (h + 1) * HEAD_PAD)
        k_ref[:, sl] = (kn[:, sl] + kr).astype(BF16)

    u_ref[...] = _dot(hn, wu_ref[...]).astype(BF16)


def _in_proj(x, pos_f, g_pre, w_low, w_u, q_norm, w_uq_p, kv_norm, w_uk_p, w_uv):
    b, l, d = x.shape
    tl = ROW_TILE
    const = lambda *_: (0, 0)
    return pl.pallas_call(
        _in_proj_kernel,
        grid=(b, l // tl),
        in_specs=[
            pl.BlockSpec((None, tl, d), lambda bi, i: (bi, i, 0)),
            pl.BlockSpec((None, 1, tl), lambda bi, i: (bi, 0, i)),
            pl.BlockSpec((1, d), const),
            pl.BlockSpec(w_low.shape, const),
            pl.BlockSpec(w_u.shape, const),
            pl.BlockSpec((1, Q_LORA_RANK), const),
            pl.BlockSpec((Q_LORA_RANK, N_HEADS * HEAD_PAD), const),
            pl.BlockSpec((1, KV_LORA_RANK), const),
            pl.BlockSpec((KV_LORA_RANK, N_HEADS * HEAD_PAD), const),
            pl.BlockSpec((N_HEADS * V_HEAD_DIM, KV_LORA_RANK), const),
        ],
        out_specs=[
            pl.BlockSpec((None, tl, N_HEADS * HEAD_PAD), lambda bi, i: (bi, i, 0)),
            pl.BlockSpec((None, tl, N_HEADS * HEAD_PAD), lambda bi, i: (bi, i, 0)),
            pl.BlockSpec((None, tl // ATTN_KEY_BLOCK, N_HEADS * V_HEAD_DIM, ATTN_KEY_BLOCK),
                         lambda bi, i: (bi, i, 0, 0)),
            pl.BlockSpec((tl, SSM_WIDTH), lambda bi, i: (i, bi)),
        ],
        out_shape=[
            jax.ShapeDtypeStruct((b, l, N_HEADS * HEAD_PAD), BF16),
            jax.ShapeDtypeStruct((b, l, N_HEADS * HEAD_PAD), BF16),
            jax.ShapeDtypeStruct((b, l // ATTN_KEY_BLOCK, N_HEADS * V_HEAD_DIM, ATTN_KEY_BLOCK), BF16),
            jax.ShapeDtypeStruct((l, b * SSM_WIDTH), BF16),
        ],
        compiler_params=pltpu.CompilerParams(
            dimension_semantics=("parallel", "parallel"), vmem_limit_bytes=VMEM_LIMIT),
        name="in_proj",
    )(x, pos_f, g_pre, w_low, w_u, q_norm, w_uq_p, kv_norm, w_uk_p, w_uv)


def _attn_kernel(q_ref, k_ref, vt_ref, o_ref, s_ref, mx_ref, m_ref, acc_ref, *, tq, tk):
    n_q = q_ref.shape[0] // tq
    key = lax.broadcasted_iota(jnp.int32, (tk, tq), 0)
    qry = lax.broadcasted_iota(jnp.int32, (tk, tq), 1)
    diag_mask = key <= qry
    vrow = lax.broadcasted_iota(jnp.int32, (2 * V_HEAD_DIM, tk), 0)
    one = jnp.ones((), BF16)

    def query_tile(i):
        def scores(j, slot, mask=None, tile=None, cols=slice(None)):
            r0 = j * tk
            nq = tq if cols.start is None else cols.stop - cols.start
            q0 = (i if tile is None else tile) * tq + (cols.start or 0)
            for hh in range(2):
                lanes = slice(hh * HEAD_PAD, (hh + 1) * HEAD_PAD)
                kj = k_ref[pl.ds(r0, tk), lanes]
                qt = q_ref[pl.ds(q0, nq), lanes]
                s = lax.dot_general(kj, qt, (((1,), (1,)), ((), ())), preferred_element_type=F32)
                if mask is not None:
                    s = jnp.where(mask, s, NEG_BIG)
                s_ref[slot, hh, :, cols] = s
                mx_ref[slot, hh, :, cols] = jnp.max(s, axis=0, keepdims=True)

        def softmax_pv(j, slot, cols=slice(None)):
            vt = vt_ref[j]
            for hh in range(2):
                vt_h = jnp.where(vrow == SUM_ROW[hh], one, vt)
                if j == 0:
                    m_new = mx_ref[slot, hh]
                    p = jnp.exp2(s_ref[slot, hh] - m_new)
                    acc_ref[hh] = _dot(vt_h, p.astype(BF16))
                else:
                    m = m_ref[hh, :, cols]
                    m_new = jnp.maximum(m, mx_ref[slot, hh, :, cols])
                    alpha = jnp.exp2(m - m_new)
                    p = jnp.exp2(s_ref[slot, hh, :, cols] - m_new)
                    acc_ref[hh, :, cols] = alpha * acc_ref[hh, :, cols] + _dot(vt_h, p.astype(BF16))
                m_ref[hh, :, cols] = m_new

        def two_blocks(t, last=False):
            scores(2 * t + 1, 1)
            softmax_pv(2 * t, 0)
            scores(2 * t + 2, 0, diag_mask if last else None)
            softmax_pv(2 * t + 1, 1)

        for t in range(max(i - 1, 0)):
            two_blocks(t)

        def diagonal_tile():
            late = slice(tk, tq)
            scores(2 * i + 1, 1, diag_mask[:, :tq - tk], cols=late)
            softmax_pv(2 * i, 0)
            scores(0, 0, tile=min(i + 1, n_q - 1))
            softmax_pv(2 * i + 1, 1, cols=late)

        if i > 0:
            two_blocks(i - 1, last=True)
        else:
            scores(0, 0, diag_mask)
        diagonal_tile()

        a0 = acc_ref[0]
        a1 = acc_ref[1]
        o0 = a0[:V_HEAD_DIM] / a0[SUM_ROW[0]:SUM_ROW[0] + 1]
        o1 = a1[V_HEAD_DIM:] / a1[SUM_ROW[1]:SUM_ROW[1] + 1]
        out_t = jnp.concatenate([o0, o1], axis=0)
        o_ref[i * tq:(i + 1) * tq, :] = out_t.T.astype(BF16)

    for i in range(n_q):
        query_tile(i)


def _attention(q_all, k_all, v_t):
    b, l, _ = q_all.shape
    tq, tk = ATTN_TILE, ATTN_KEY_BLOCK
    return pl.pallas_call(
        functools.partial(_attn_kernel, tq=tq, tk=tk),
        grid=(b, N_PAIRS),
        in_specs=[
            pl.BlockSpec((None, l, 2 * HEAD_PAD), lambda bi, p: (bi, 0, p)),
            pl.BlockSpec((None, l, 2 * HEAD_PAD), lambda bi, p: (bi, 0, p)),
            pl.BlockSpec((None, l // tk, 2 * V_HEAD_DIM, tk), lambda bi, p: (bi, 0, p, 0)),
        ],
        out_specs=pl.BlockSpec((None, l, 2 * V_HEAD_DIM), lambda bi, p: (bi, 0, p)),
        out_shape=jax.ShapeDtypeStruct((b, l, N_HEADS * V_HEAD_DIM), BF16),
        scratch_shapes=[
            pltpu.VMEM((2, 2, tk, tq), F32),
            pltpu.VMEM((2, 2, 1, tq), F32),
            pltpu.VMEM((2, 1, tq), F32),
            pltpu.VMEM((2, 2 * V_HEAD_DIM, tq), F32),
        ],
        compiler_params=pltpu.CompilerParams(
            dimension_semantics=("parallel", "parallel"), vmem_limit_bytes=VMEM_LIMIT),
        name="attention",
    )(q_all, k_all, v_t)


def _s5_kernel(u_ref, are_ref, aim_ref, wb_ref, wcr_ref, wci_ref, d_ref, wglu_ref, bglu_ref,
               o_ref, state_ref, sre_ref, sim_ref, tb_ref, *, steps, batch):
    @pl.when(pl.program_id(0) == 0)
    def _():
        state_ref[...] = jnp.zeros_like(state_ref)

    lane_groups = SSM_WIDTH // LANES
    for b in range(batch):
        for c in range(lane_groups):
            c0 = b * SSM_WIDTH + c * LANES
            tb_ref[c, pl.ds(b, steps, stride=batch), :] = u_ref[:, c0:c0 + LANES].astype(F32)
    u32 = jnp.concatenate([tb_ref[c] for c in range(lane_groups)], axis=1)
    u = u32.astype(BF16)
    for k in range(SSM_HALVES):
        bu = _dot(u[:, k * SSM_HALF_IN:(k + 1) * SSM_HALF_IN], wb_ref[k])
        sre_ref[k] = bu[:, :SSM_HALF_STATE]
        sim_ref[k] = bu[:, SSM_HALF_STATE:]

    nt = (((1,), (1,)), ((), ()))
    ys = []
    for k in range(SSM_HALVES):
        ar = are_ref[:, k * SSM_HALF_STATE:(k + 1) * SSM_HALF_STATE]
        ai = aim_ref[:, k * SSM_HALF_STATE:(k + 1) * SSM_HALF_STATE]
        sr = state_ref[k, 0]
        si = state_ref[k, 1]
        for t in range(steps):
            rs = slice(t * batch, (t + 1) * batch)
            nr = ar * sr - ai * si + sre_ref[k, rs, :]
            ni = ar * si + ai * sr + sim_ref[k, rs, :]
            sre_ref[k, rs, :] = nr
            sim_ref[k, rs, :] = ni
            sr, si = nr, ni
        state_ref[k, 0] = sr
        state_ref[k, 1] = si
        ys.append(lax.dot_general(sre_ref[k].astype(BF16), wcr_ref[k], nt, preferred_element_type=F32)
                  - lax.dot_general(sim_ref[k].astype(BF16), wci_ref[k], nt, preferred_element_type=F32))

    y = jnp.concatenate(ys, axis=1) + d_ref[...] * u32
    g = _gelu_tanh(y)
    z = _dot(g.astype(BF16), wglu_ref[...]) + bglu_ref[...]
    out = g * jax.nn.sigmoid(z)
    for c in range(lane_groups):
        tb_ref[c] = out[:, c * LANES:(c + 1) * LANES]
    for b in range(batch):
        for c in range(lane_groups):
            c0 = b * SSM_WIDTH + c * LANES
            o_ref[:, c0:c0 + LANES] = tb_ref[c, pl.ds(b, steps, stride=batch), :].astype(BF16)


def _s5(u_tm, a_re_t, a_im_t, wb, wcr, wci, d_skip, w_glu, b_glu, batch):
    length = u_tm.shape[0]
    width = SSM_WIDTH
    steps = SSM_CHUNK
    rows = steps * batch
    const2 = lambda i: (0, 0)
    const3 = lambda i: (0, 0, 0)
    return pl.pallas_call(
        functools.partial(_s5_kernel, steps=steps, batch=batch),
        grid=(length // steps,),
        in_specs=[
            pl.BlockSpec((steps, batch * width), lambda i: (i, 0)),
            pl.BlockSpec(a_re_t.shape, const2),
            pl.BlockSpec(a_im_t.shape, const2),
            pl.BlockSpec(wb.shape, const3),
            pl.BlockSpec(wcr.shape, const3),
            pl.BlockSpec(wci.shape, const3),
            pl.BlockSpec((1, width), const2),
            pl.BlockSpec((width, width), const2),
            pl.BlockSpec((1, width), const2),
        ],
        out_specs=pl.BlockSpec((steps, batch * width), lambda i: (i, 0)),
        out_shape=jax.ShapeDtypeStruct((length, batch * width), BF16),
        scratch_shapes=[
            pltpu.VMEM((SSM_HALVES, 2, batch, SSM_HALF_STATE), F32),
            pltpu.VMEM((SSM_HALVES, rows, SSM_HALF_STATE), F32),
            pltpu.VMEM((SSM_HALVES, rows, SSM_HALF_STATE), F32),
            pltpu.VMEM((width // LANES, rows, LANES), F32),
        ],
        compiler_params=pltpu.CompilerParams(
            dimension_semantics=("arbitrary",), vmem_limit_bytes=VMEM_LIMIT),
        name="s5",
    )(u_tm, a_re_t, a_im_t, wb, wcr, wci, d_skip, w_glu, b_glu)


def _merge_kernel(x_ref, attn_ref, ssm_ref, gpre_ref, wgate_ref, bgate_ref, wba_ref, wbs_ref, wout_ref,
                  gpost_ref, o_ref, perm_ref):
    tl, d = x_ref.shape
    n = tl // FFN_PHASES
    sub = tl // MERGE_SUBTILES
    ns = n // MERGE_SUBTILES

    def mixed(st):
        rows = slice(st * sub, (st + 1) * sub)
        hn = _rms(x_ref[rows, :], gpre_ref[...]).astype(BF16)
        gates = jax.nn.sigmoid(_dot(hn, wgate_ref[...]) + bgate_ref[...])
        merged = (gates[:, :d] * _dot(attn_ref[rows, :], wba_ref[...])
                  + gates[:, d:] * _dot(ssm_ref[rows, :], wbs_ref[...]))
        return merged.astype(BF16)

    def finish(st, m2):
        rows = slice(st * sub, (st + 1) * sub)
        out = x_ref[rows, :] + _rms(m2, gpost_ref[...])
        for c in range(d // LANES):
            perm_ref[st, c] = out[:, c * LANES:(c + 1) * LANES]
        for c in range(d // LANES):
            for b in range(FFN_PHASES):
                r0 = b * n + st * ns
                o_ref[r0:r0 + ns, c * LANES:(c + 1) * LANES] = perm_ref[st, c, pl.ds(b, ns, stride=FFN_PHASES), :]

    pending = None
    for st in range(MERGE_SUBTILES):
        merged = mixed(st)
        if pending is not None:
            finish(*pending)
        pending = (st, _dot(merged, wout_ref[...]))
    finish(*pending)


def _merge(x, attn, ssm_tm, g_pre, w_gate, b_gate, w_ba, w_bs, w_out, g_post):
    b, l, d = x.shape
    tl = ROW_TILE
    const = lambda *_: (0, 0)
    return pl.pallas_call(
        _merge_kernel,
        grid=(b, l // tl),
        in_specs=[
            pl.BlockSpec((None, tl, d), lambda bi, i: (bi, i, 0)),
            pl.BlockSpec((None, tl, N_HEADS * V_HEAD_DIM), lambda bi, i: (bi, i, 0)),
            pl.BlockSpec((tl, SSM_WIDTH), lambda bi, i: (i, bi)),
            pl.BlockSpec((1, d), const),
            pl.BlockSpec(w_gate.shape, const),
            pl.BlockSpec((1, 2 * d), const),
            pl.BlockSpec(w_ba.shape, const),
            pl.BlockSpec(w_bs.shape, const),
            pl.BlockSpec(w_out.shape, const),
            pl.BlockSpec((1, d), const),
        ],
        out_specs=pl.BlockSpec((None, tl, d), lambda bi, i: (bi, i, 0)),
        out_shape=jax.ShapeDtypeStruct((b, l, d), F32),
        scratch_shapes=[pltpu.VMEM((MERGE_SUBTILES, d // LANES, tl // MERGE_SUBTILES, LANES), F32)],
        compiler_params=pltpu.CompilerParams(
            dimension_semantics=("parallel", "parallel"), vmem_limit_bytes=VMEM_LIMIT),
        name="merge",
    )(x, attn, ssm_tm, g_pre, w_gate, b_gate, w_ba, w_bs, w_out, g_post)


def _ffn_kernel(x_ref, gpre_ref, wup_ref, cw_ref, cb_ref, wdown_ref, gpost_ref, o_ref,
                act_ref, perm_ref, tail_ref):
    i = pl.program_id(1)
    tl, d = x_ref.shape
    n = tl // FFN_PHASES
    lane_groups = d // LANES

    @pl.when(i == 0)
    def _():
        tail_ref[...] = jnp.zeros(tail_ref.shape, F32)

    x1 = x_ref[...]
    hn = _rms(x1, gpre_ref[...]).astype(BF16)

    first_row = lax.broadcasted_iota(jnp.int32, (n, 1), 0) == 0

    def conv(h, c0):
        cols = slice(c0, c0 + FFN_COL_CHUNK)
        blocks = [h[b * n:(b + 1) * n] for b in range(FFN_PHASES)]

        def moved_down(block, prev_token):
            return jnp.where(first_row, prev_token, pltpu.roll(block, 1, 0))

        s1 = moved_down(blocks[FFN_PHASES - 1], tail_ref[1:2, cols])
        s2 = moved_down(blocks[FFN_PHASES - 2], tail_ref[0:1, cols])
        tail_ref[0:1, cols] = blocks[FFN_PHASES - 2][n - 1:n]
        tail_ref[1:2, cols] = blocks[FFN_PHASES - 1][n - 1:n]
        tap1 = [s1] + blocks[:-1]
        tap2 = [s2, s1] + blocks[:-2]
        w0, w1, w2, bias = cw_ref[0:1, cols], cw_ref[1:2, cols], cw_ref[2:3, cols], cb_ref[:, cols]
        return jnp.concatenate(
            [bias + w2 * blocks[b] + w1 * tap1[b] + w0 * tap2[b] for b in range(FFN_PHASES)], axis=0)

    for j in range(D_FF // FFN_COL_CHUNK):
        c0 = j * FFN_COL_CHUNK
        hg = _dot(hn, wup_ref[:, c0:c0 + FFN_COL_CHUNK])
        hv = _dot(hn, wup_ref[:, D_FF + c0:D_FF + c0 + FFN_COL_CHUNK])
        act = _gelu_tanh(conv(hg, c0)) * conv(hv, D_FF + c0)
        act_ref[:, c0:c0 + FFN_COL_CHUNK] = act.astype(BF16)

    ff = _dot(act_ref[...], wdown_ref[...])
    out = x1 + _rms(ff, gpost_ref[...])

    for c in range(lane_groups):
        for b in range(FFN_PHASES):
            perm_ref[c, pl.ds(b, n, stride=FFN_PHASES), :] = out[b * n:(b + 1) * n, c * LANES:(c + 1) * LANES]
    for c in range(lane_groups):
        o_ref[:, c * LANES:(c + 1) * LANES] = perm_ref[c]


def _conv_ffn(x1, g_pre, w_up, conv_w, conv_b, w_down, g_post):
    b, l, d = x1.shape
    tl = ROW_TILE
    const = lambda *_: (0, 0)
    return pl.pallas_call(
        _ffn_kernel,
        grid=(b, l // tl),
        in_specs=[
            pl.BlockSpec((None, tl, d), lambda bi, i: (bi, i, 0)),
            pl.BlockSpec((1, d), const),
            pl.BlockSpec(w_up.shape, const, pipeline_mode=pl.Buffered(1)),
            pl.BlockSpec(conv_w.shape, const),
            pl.BlockSpec(conv_b.shape, const),
            pl.BlockSpec(w_down.shape, const, pipeline_mode=pl.Buffered(1)),
            pl.BlockSpec((1, d), const),
        ],
        out_specs=pl.BlockSpec((None, tl, d), lambda bi, i: (bi, i, 0)),
        out_shape=jax.ShapeDtypeStruct((b, l, d), F32),
        scratch_shapes=[pltpu.VMEM((tl, D_FF), BF16),
                        pltpu.VMEM((d // LANES, tl, LANES), F32),
                        pltpu.VMEM((CONV_WIDTH - 1, 2 * D_FF), F32)],
        compiler_params=pltpu.CompilerParams(
            dimension_semantics=("parallel", "arbitrary"), vmem_limit_bytes=VMEM_LIMIT),
        name="conv_ffn",
    )(x1, g_pre, w_up, conv_w, conv_b, w_down, g_post)


def _head_lane_groups(nope, rope):
    k = (nope if nope is not None else rope).shape[0]
    nope = jnp.zeros((k, N_HEADS, QK_NOPE_DIM), BF16) if nope is None else nope
    rope = jnp.zeros((k, N_HEADS, QK_ROPE_DIM), BF16) if rope is None else rope
    return jnp.concatenate([nope, rope, rope], axis=2).astype(BF16).reshape(k, -1)


def _layer(x, pos_f, p):
    b, l, d = x.shape
    row = lambda v: v.reshape(1, -1)

    w_in = p["w_in"]
    off_u = P_KR + QK_ROPE_DIM
    off_gate = off_u + SSM_WIDTH
    w_kr = w_in[:, P_KR:off_u]
    w_low = jnp.concatenate([w_in[:, :P_KR], jnp.zeros((d, QK_NOPE_DIM), w_in.dtype), w_kr, w_kr],
                            axis=1).astype(BF16)
    w_u = w_in[:, off_u:off_gate].astype(BF16)
    w_gate = w_in[:, off_gate:].astype(BF16)
    w_uq = p["w_uq"].reshape(Q_LORA_RANK, N_HEADS, QK_HEAD_DIM)
    w_uq_p = _head_lane_groups(w_uq[:, :, :QK_NOPE_DIM], w_uq[:, :, QK_NOPE_DIM:])
    w_uk_p = _head_lane_groups(p["w_uk"].reshape(KV_LORA_RANK, N_HEADS, QK_NOPE_DIM), None)

    q_all, k_all, v_t, u_tm = _in_proj(
        x, pos_f, row(p["mix_norm_pre"]), w_low, w_u, row(p["q_norm"]), w_uq_p, row(p["kv_norm"]),
        w_uk_p, p["w_uv"].T.astype(BF16))

    attn = _attention(q_all, k_all, v_t)

    a_re_t, a_im_t, wb, wcr, wci = _s5_prep(
        p["ssm_lambda_re"], p["ssm_lambda_im"], p["ssm_log_dt"], p["ssm_b_re"], p["ssm_b_im"],
        p["ssm_c_re"], p["ssm_c_im"], b)
    ssm_tm = _s5(u_tm, a_re_t, a_im_t, wb, wcr, wci,
                 row(p["ssm_d"]), p["w_glu"].astype(BF16), row(p["b_glu"]), b)

    x1 = _merge(x, attn, ssm_tm, row(p["mix_norm_pre"]), w_gate, row(p["b_gate"]),
                p["w_branch_attn"].astype(BF16), p["w_branch_ssm"].astype(BF16),
                p["w_out"].astype(BF16), row(p["mix_norm_post"]))
    return _conv_ffn(x1, row(p["ffn_norm_pre"]), p["w_up"].astype(BF16), p["conv_w"], row(p["conv_b"]),
                     p["w_down"].astype(BF16), row(p["ffn_norm_post"]))


def kernel(x, positions, mix_norm_pre, w_in, q_norm, w_uq, kv_norm, w_uk, w_uv, ssm_lambda_re, ssm_lambda_im, ssm_log_dt, ssm_b_re, ssm_b_im, ssm_c_re, ssm_c_im, ssm_d, w_glu, b_glu, w_branch_attn, w_branch_ssm, b_gate, w_out, mix_norm_post, ffn_norm_pre, w_up, conv_w, conv_b, w_down, ffn_norm_post):
    b, l, _ = x.shape
    params = dict(mix_norm_pre=mix_norm_pre, w_in=w_in, q_norm=q_norm, w_uq=w_uq, kv_norm=kv_norm,
                  w_uk=w_uk, w_uv=w_uv, ssm_lambda_re=ssm_lambda_re, ssm_lambda_im=ssm_lambda_im,
                  ssm_log_dt=ssm_log_dt, ssm_b_re=ssm_b_re, ssm_b_im=ssm_b_im, ssm_c_re=ssm_c_re,
                  ssm_c_im=ssm_c_im, ssm_d=ssm_d, w_glu=w_glu, b_glu=b_glu, w_branch_attn=w_branch_attn,
                  w_branch_ssm=w_branch_ssm, b_gate=b_gate, w_out=w_out, mix_norm_post=mix_norm_post,
                  ffn_norm_pre=ffn_norm_pre, w_up=w_up, conv_w=conv_w, conv_b=conv_b, w_down=w_down,
                  ffn_norm_post=ffn_norm_post)
    pos_f = positions.astype(F32).reshape(b, 1, l)
    for layer in range(mix_norm_pre.shape[0]):
        x = _layer(x, pos_f, {k: v[layer] for k, v in params.items()})
    return x
```

```python
import functools
import math

import jax
import jax.numpy as jnp
from jax import lax
from jax.experimental import pallas as pl
from jax.experimental.pallas import tpu as pltpu

D_MODEL = 1024
N_HEADS = 8
QK_NOPE_DIM = 64
QK_ROPE_DIM = 32
QK_HEAD_DIM = QK_NOPE_DIM + QK_ROPE_DIM
V_HEAD_DIM = 64
Q_LORA_RANK = 384
KV_LORA_RANK = 256
ROPE_THETA = 10000.0
SSM_WIDTH = 512
SSM_GROUP = 16
SSM_GROUPS = SSM_WIDTH // SSM_GROUP
SSM_STATE = 64
D_FF = 2816
CONV_WIDTH = 3
EPS = 1e-6

LANES = 128
HEAD_PAD = LANES
ROPE_LANE0 = QK_NOPE_DIM
ROPE_HALF = QK_ROPE_DIM // 2
N_PAIRS = N_HEADS // 2
SUM_ROW = (V_HEAD_DIM, 0)
SSM_HALVES = 2
SSM_HALF_IN = SSM_WIDTH // SSM_HALVES
SSM_HALF_STATE = SSM_GROUPS * SSM_STATE // SSM_HALVES

P_CQ = 0
P_CKV = P_CQ + Q_LORA_RANK
P_KR = P_CKV + KV_LORA_RANK
P_U = P_KR + LANES

Q_SCALE = (1.0 / math.sqrt(QK_HEAD_DIM)) * math.log2(math.e)
NEG_BIG = -1e30

ROW_TILE = 512
ATTN_TILE = ROW_TILE
ATTN_KEY_BLOCK = ATTN_TILE // 2
SSM_CHUNK = 64
MERGE_SUBTILES = 2
FFN_COL_CHUNK = 256
FFN_PHASES = 8
VMEM_LIMIT = 56 * 1024 * 1024

BF16 = jnp.bfloat16
F32 = jnp.float32


def _rms(x, g):
    return x * lax.rsqrt(jnp.mean(x * x, axis=-1, keepdims=True) + EPS) * g


def _dot(a, b):
    return jnp.dot(a, b, preferred_element_type=F32)


GELU_C = math.sqrt(2.0 / math.pi)


def _gelu_tanh(x):
    k = -2.0 * GELU_C * math.log2(math.e)
    e = jnp.exp2(x * (x * x * (k * 0.044715) + k))
    return x / (1.0 + e)


def _s5_prep_kernel(lr_ref, li_ref, logdt_ref, br_ref, bi_ref, cr_ref, ci_ref,
                    are_ref, aim_ref, wb_ref, wcr_ref, wci_ref, *, batch):
    lr = lr_ref[...]
    li = li_ref[...]
    dt = jnp.exp(logdt_ref[...])
    mag = jnp.exp(lr * dt)
    ang = li * dt
    a_re = mag * jnp.cos(ang)
    a_im = mag * jnp.sin(ang)
    den = lr * lr + li * li
    n_re = a_re - 1.0
    n_im = a_im
    z_re = (n_re * lr + n_im * li) / den
    z_im = (n_im * lr - n_re * li) / den
    are_ref[...] = jnp.broadcast_to(a_re, (batch, a_re.shape[1]))
    aim_ref[...] = jnp.broadcast_to(a_im, (batch, a_im.shape[1]))
    br = br_ref[...]
    bi = bi_ref[...]
    bb_re = z_re * br - z_im * bi
    bb_im = z_re * bi + z_im * br

    groups_per_half = SSM_GROUPS // SSM_HALVES
    row_group = jnp.right_shift(lax.broadcasted_iota(jnp.int32, (SSM_HALF_IN, SSM_HALF_STATE), 0),
                                SSM_GROUP.bit_length() - 1)
    col_group = jnp.right_shift(lax.broadcasted_iota(jnp.int32, (SSM_HALF_IN, SSM_HALF_STATE), 1),
                                SSM_STATE.bit_length() - 1)
    on_diag = row_group == col_group

    def diag(v):
        return jnp.where(on_diag, jnp.concatenate([v] * groups_per_half, axis=0), 0.0)

    for k in range(SSM_HALVES):
        cols = slice(k * SSM_HALF_STATE, (k + 1) * SSM_HALF_STATE)
        wb_ref[k, :, :SSM_HALF_STATE] = diag(bb_re[:, cols]).astype(BF16)
        wb_ref[k, :, SSM_HALF_STATE:] = diag(bb_im[:, cols]).astype(BF16)
        wcr_ref[k] = diag(cr_ref[:, cols]).astype(BF16)
        wci_ref[k] = diag(ci_ref[:, cols]).astype(BF16)


def _s5_prep(lam_re, lam_im, log_dt, b_re, b_im, c_re, c_im, batch):
    n = SSM_GROUPS * SSM_STATE
    lanes = lambda v: v.reshape(1, n)
    chan_major = lambda v, perm: v.transpose(perm).reshape(SSM_GROUP, n)
    return pl.pallas_call(
        functools.partial(_s5_prep_kernel, batch=batch),
        out_shape=(jax.ShapeDtypeStruct((batch, n), F32), jax.ShapeDtypeStruct((batch, n), F32),
                   jax.ShapeDtypeStruct((SSM_HALVES, SSM_HALF_IN, 2 * SSM_HALF_STATE), BF16),
                   jax.ShapeDtypeStruct((SSM_HALVES, SSM_HALF_IN, SSM_HALF_STATE), BF16),
                   jax.ShapeDtypeStruct((SSM_HALVES, SSM_HALF_IN, SSM_HALF_STATE), BF16)),
        name="s5_prep",
    )(lanes(lam_re), lanes(lam_im), lanes(jnp.repeat(log_dt, SSM_STATE)),
      chan_major(b_re, (2, 0, 1)), chan_major(b_im, (2, 0, 1)),
      chan_major(c_re, (1, 0, 2)), chan_major(c_im, (1, 0, 2)))


def _in_proj_kernel(x_ref, pos_ref, gpre_ref, wlow_ref, wu_ref, qn_ref, wuq_ref, kvn_ref, wuk_ref,
                    wuv_ref, q_ref, k_ref, vt_ref, u_ref):
    x = x_ref[...]
    tl = x.shape[0]
    hn = _rms(x, gpre_ref[...]).astype(BF16)

    freq_idx = lax.broadcasted_iota(jnp.int32, (ROPE_HALF, 1), 0).astype(F32)
    inv_freq = jnp.exp(freq_idx * (-2.0 * math.log(ROPE_THETA) / QK_ROPE_DIM))
    ang = inv_freq * pos_ref[...]
    c16 = jnp.cos(ang)
    s16 = jnp.sin(ang)
    ones = jnp.ones((ROPE_LANE0, tl), F32)
    zeros = lambda n: jnp.zeros((n, tl), F32)
    cos = jnp.concatenate([ones, zeros(ROPE_HALF), c16, c16, zeros(ROPE_HALF)], axis=0).T
    sin = jnp.concatenate([zeros(ROPE_LANE0 + ROPE_HALF), s16, -s16, zeros(ROPE_HALF)], axis=0).T
    cos_q = cos * Q_SCALE
    sin_q = sin * Q_SCALE

    low = _dot(hn, wlow_ref[...])
    cq = low[:, P_CQ:P_CKV]
    ckv = low[:, P_CKV:P_KR]
    kr = low[:, P_KR:P_U]
    cqn = _rms(cq, qn_ref[...]).astype(BF16)
    q = _dot(cqn, wuq_ref[...])
    for h in range(N_HEADS):
        sl = slice(h * HEAD_PAD, (h + 1) * HEAD_PAD)
        t = q[:, sl]
        q_ref[:, sl] = (t * cos_q + pltpu.roll(t, ROPE_HALF, 1) * sin_q).astype(BF16)

    ckvn = _rms(ckv, kvn_ref[...]).astype(BF16)
    v_t = lax.dot_general(wuv_ref[...], ckvn, (((1,), (1,)), ((), ())),
                          preferred_element_type=F32).astype(BF16)
    for c in range(vt_ref.shape[0]):
        vt_ref[c] = v_t[:, c * ATTN_KEY_BLOCK:(c + 1) * ATTN_KEY_BLOCK]
    kn = _dot(ckvn, wuk_ref[...])
    kr = kr * cos + pltpu.roll(kr, ROPE_HALF, 1) * sin
    for h in range(N_HEADS):
        sl = slice(h * HEAD_PAD, (h + 1) * HEAD_PAD)
        k_ref[:, sl] = (kn[:, sl] + kr).astype(BF16)

    u_ref[...] = _dot(hn, wu_ref[...]).astype(BF16)


def _in_proj(x, pos_f, g_pre, w_low, w_u, q_norm, w_uq_p, kv_norm, w_uk_p, w_uv):
    b, l, d = x.shape
    tl = ROW_TILE
    const = lambda *_: (0, 0)
    return pl.pallas_call(
        _in_proj_kernel,
        grid=(b, l // tl),
        in_specs=[
            pl.BlockSpec((None, tl, d), lambda bi, i: (bi, i, 0)),
            pl.BlockSpec((None, 1, tl), lambda bi, i: (bi, 0, i)),
            pl.BlockSpec((1, d), const),
            pl.BlockSpec(w_low.shape, const),
            pl.BlockSpec(w_u.shape, const),
            pl.BlockSpec((1, Q_LORA_RANK), const),
            pl.BlockSpec((Q_LORA_RANK, N_HEADS * HEAD_PAD), const),
            pl.BlockSpec((1, KV_LORA_RANK), const),
            pl.BlockSpec((KV_LORA_RANK, N_HEADS * HEAD_PAD), const),
            pl.BlockSpec((N_HEADS * V_HEAD_DIM, KV_LORA_RANK), const),
        ],
        out_specs=[
            pl.BlockSpec((None, tl, N_HEADS * HEAD_PAD), lambda bi, i: (bi, i, 0)),
            pl.BlockSpec((None, tl, N_HEADS * HEAD_PAD), lambda bi, i: (bi, i, 0)),
            pl.BlockSpec((None, tl // ATTN_KEY_BLOCK, N_HEADS * V_HEAD_DIM, ATTN_KEY_BLOCK),
                         lambda bi, i: (bi, i, 0, 0)),
            pl.BlockSpec((tl, SSM_WIDTH), lambda bi, i: (i, bi)),
        ],
        out_shape=[
            jax.ShapeDtypeStruct((b, l, N_HEADS * HEAD_PAD), BF16),
            jax.ShapeDtypeStruct((b, l, N_HEADS * HEAD_PAD), BF16),
            jax.ShapeDtypeStruct((b, l // ATTN_KEY_BLOCK, N_HEADS * V_HEAD_DIM, ATTN_KEY_BLOCK), BF16),
            jax.ShapeDtypeStruct((l, b * SSM_WIDTH), BF16),
        ],
        compiler_params=pltpu.CompilerParams(
            dimension_semantics=("parallel", "parallel"), vmem_limit_bytes=VMEM_LIMIT),
        name="in_proj",
    )(x, pos_f, g_pre, w_low, w_u, q_norm, w_uq_p, kv_norm, w_uk_p, w_uv)


def _attn_kernel(q_ref, k_ref, vt_ref, o_ref, s_ref, mx_ref, m_ref, acc_ref, *, tq, tk):
    n_q = q_ref.shape[0] // tq
    key = lax.broadcasted_iota(jnp.int32, (tk, tq), 0)
    qry = lax.broadcasted_iota(jnp.int32, (tk, tq), 1)
    diag_mask = key <= qry
    vrow = lax.broadcasted_iota(jnp.int32, (2 * V_HEAD_DIM, tk), 0)
    one = jnp.ones((), BF16)

    def query_tile(i):
        def scores(j, slot, mask=None, tile=None, cols=slice(None)):
            r0 = j * tk
            nq = tq if cols.start is None else cols.stop - cols.start
            q0 = (i if tile is None else tile) * tq + (cols.start or 0)
            for hh in range(2):
                lanes = slice(hh * HEAD_PAD, (hh + 1) * HEAD_PAD)
                kj = k_ref[pl.ds(r0, tk), lanes]
                qt = q_ref[pl.ds(q0, nq), lanes]
                s = lax.dot_general(kj, qt, (((1,), (1,)), ((), ())), preferred_element_type=F32)
                if mask is not None:
                    s = jnp.where(mask, s, NEG_BIG)
                s_ref[slot, hh, :, cols] = s
                mx_ref[slot, hh, :, cols] = jnp.max(s, axis=0, keepdims=True)

        def softmax_pv(j, slot, cols=slice(None)):
            vt = vt_ref[j]
            for hh in range(2):
                vt_h = jnp.where(vrow == SUM_ROW[hh], one, vt)
                if j == 0:
                    m_new = mx_ref[slot, hh]
                    p = jnp.exp2(s_ref[slot, hh] - m_new)
                    acc_ref[hh] = _dot(vt_h, p.astype(BF16))
                else:
                    m = m_ref[hh, :, cols]
                    m_new = jnp.maximum(m, mx_ref[slot, hh, :, cols])
                    alpha = jnp.exp2(m - m_new)
                    p = jnp.exp2(s_ref[slot, hh, :, cols] - m_new)
                    acc_ref[hh, :, cols] = alpha * acc_ref[hh, :, cols] + _dot(vt_h, p.astype(BF16))
                m_ref[hh, :, cols] = m_new

        def two_blocks(t, last=False):
            scores(2 * t + 1, 1)
            softmax_pv(2 * t, 0)
            scores(2 * t + 2, 0, diag_mask if last else None)
            softmax_pv(2 * t + 1, 1)

        for t in range(max(i - 1, 0)):
            two_blocks(t)

        def diagonal_tile():
            late = slice(tk, tq)
            scores(2 * i + 1, 1, diag_mask[:, :tq - tk], cols=late)
            softmax_pv(2 * i, 0)
            scores(0, 0, tile=min(i + 1, n_q - 1))
            softmax_pv(2 * i + 1, 1, cols=late)

        if i > 0:
            two_blocks(i - 1, last=True)
        else:
            scores(0, 0, diag_mask)
        diagonal_tile()

        a0 = acc_ref[0]
        a1 = acc_ref[1]
        o0 = a0[:V_HEAD_DIM] / a0[SUM_ROW[0]:SUM_ROW[0] + 1]
        o1 = a1[V_HEAD_DIM:] / a1[SUM_ROW[1]:SUM_ROW[1] + 1]
        out_t = jnp.concatenate([o0, o1], axis=0)
        o_ref[i * tq:(i + 1) * tq, :] = out_t.T.astype(BF16)

    for i in range(n_q):
        query_tile(i)


def _attention(q_all, k_all, v_t):
    b, l, _ = q_all.shape
    tq, tk = ATTN_TILE, ATTN_KEY_BLOCK
    return pl.pallas_call(
        functools.partial(_attn_kernel, tq=tq, tk=tk),
        grid=(b, N_PAIRS),
        in_specs=[
            pl.BlockSpec((None, l, 2 * HEAD_PAD), lambda bi, p: (bi, 0, p)),
            pl.BlockSpec((None, l, 2 * HEAD_PAD), lambda bi, p: (bi, 0, p)),
            pl.BlockSpec((None, l // tk, 2 * V_HEAD_DIM, tk), lambda bi, p: (bi, 0, p, 0)),
        ],
        out_specs=pl.BlockSpec((None, l, 2 * V_HEAD_DIM), lambda bi, p: (bi, 0, p)),
        out_shape=jax.ShapeDtypeStruct((b, l, N_HEADS * V_HEAD_DIM), BF16),
        scratch_shapes=[
            pltpu.VMEM((2, 2, tk, tq), F32),
            pltpu.VMEM((2, 2, 1, tq), F32),
            pltpu.VMEM((2, 1, tq), F32),
            pltpu.VMEM((2, 2 * V_HEAD_DIM, tq), F32),
        ],
        compiler_params=pltpu.CompilerParams(
            dimension_semantics=("parallel", "parallel"), vmem_limit_bytes=VMEM_LIMIT),
        name="attention",
    )(q_all, k_all, v_t)


def _s5_kernel(u_ref, are_ref, aim_ref, wb_ref, wcr_ref, wci_ref, d_ref, wglu_ref, bglu_ref,
               o_ref, state_ref, sre_ref, sim_ref, tb_ref, *, steps, batch):
    @pl.when(pl.program_id(0) == 0)
    def _():
        state_ref[...] = jnp.zeros_like(state_ref)

    lane_groups = SSM_WIDTH // LANES
    for b in range(batch):
        for c in range(lane_groups):
            c0 = b * SSM_WIDTH + c * LANES
            tb_ref[c, pl.ds(b, steps, stride=batch), :] = u_ref[:, c0:c0 + LANES].astype(F32)
    u32 = jnp.concatenate([tb_ref[c] for c in range(lane_groups)], axis=1)
    u = u32.astype(BF16)
    for k in range(SSM_HALVES):
        bu = _dot(u[:, k * SSM_HALF_IN:(k + 1) * SSM_HALF_IN], wb_ref[k])
        sre_ref[k] = bu[:, :SSM_HALF_STATE]
        sim_ref[k] = bu[:, SSM_HALF_STATE:]

    nt = (((1,), (1,)), ((), ()))
    ys = []
    for k in range(SSM_HALVES):
        ar = are_ref[:, k * SSM_HALF_STATE:(k + 1) * SSM_HALF_STATE]
        ai = aim_ref[:, k * SSM_HALF_STATE:(k + 1) * SSM_HALF_STATE]
        sr = state_ref[k, 0]
        si = state_ref[k, 1]
        for t in range(steps):
            rs = slice(t * batch, (t + 1) * batch)
            nr = ar * sr - ai * si + sre_ref[k, rs, :]
            ni = ar * si + ai * sr + sim_ref[k, rs, :]
            sre_ref[k, rs, :] = nr
            sim_ref[k, rs, :] = ni
            sr, si = nr, ni
        state_ref[k, 0] = sr
        state_ref[k, 1] = si
        ys.append(lax.dot_general(sre_ref[k].astype(BF16), wcr_ref[k], nt, preferred_element_type=F32)
                  - lax.dot_general(sim_ref[k].astype(BF16), wci_ref[k], nt, preferred_element_type=F32))

    y = jnp.concatenate(ys, axis=1) + d_ref[...] * u32
    g = _gelu_tanh(y)
    z = _dot(g.astype(BF16), wglu_ref[...]) + bglu_ref[...]
    out = g * jax.nn.sigmoid(z)
    for c in range(lane_groups):
        tb_ref[c] = out[:, c * LANES:(c + 1) * LANES]
    for b in range(batch):
        for c in range(lane_groups):
            c0 = b * SSM_WIDTH + c * LANES
            o_ref[:, c0:c0 + LANES] = tb_ref[c, pl.ds(b, steps, stride=batch), :].astype(BF16)


def _s5(u_tm, a_re_t, a_im_t, wb, wcr, wci, d_skip, w_glu, b_glu, batch):
    length = u_tm.shape[0]
    width = SSM_WIDTH
    steps = SSM_CHUNK
    rows = steps * batch
    const2 = lambda i: (0, 0)
    const3 = lambda i: (0, 0, 0)
    return pl.pallas_call(
        functools.partial(_s5_kernel, steps=steps, batch=batch),
        grid=(length // steps,),
        in_specs=[
            pl.BlockSpec((steps, batch * width), lambda i: (i, 0)),
            pl.BlockSpec(a_re_t.shape, const2),
            pl.BlockSpec(a_im_t.shape, const2),
            pl.BlockSpec(wb.shape, const3),
            pl.BlockSpec(wcr.shape, const3),
            pl.BlockSpec(wci.shape, const3),
            pl.BlockSpec((1, width), const2),
            pl.BlockSpec((width, width), const2),
            pl.BlockSpec((1, width), const2),
        ],
        out_specs=pl.BlockSpec((steps, batch * width), lambda i: (i, 0)),
        out_shape=jax.ShapeDtypeStruct((length, batch * width), BF16),
        scratch_shapes=[
            pltpu.VMEM((SSM_HALVES, 2, batch, SSM_HALF_STATE), F32),
            pltpu.VMEM((SSM_HALVES, rows, SSM_HALF_STATE), F32),
            pltpu.VMEM((SSM_HALVES, rows, SSM_HALF_STATE), F32),
            pltpu.VMEM((width // LANES, rows, LANES), F32),
        ],
        compiler_params=pltpu.CompilerParams(
            dimension_semantics=("arbitrary",), vmem_limit_bytes=VMEM_LIMIT),
        name="s5",
    )(u_tm, a_re_t, a_im_t, wb, wcr, wci, d_skip, w_glu, b_glu)


def _merge_kernel(x_ref, attn_ref, ssm_ref, gpre_ref, wgate_ref, bgate_ref, wba_ref, wbs_ref, wout_ref,
                  gpost_ref, o_ref, perm_ref):
    tl, d = x_ref.shape
    n = tl // FFN_PHASES
    sub = tl // MERGE_SUBTILES
    ns = n // MERGE_SUBTILES

    def mixed(st):
        rows = slice(st * sub, (st + 1) * sub)
        hn = _rms(x_ref[rows, :], gpre_ref[...]).astype(BF16)
        gates = jax.nn.sigmoid(_dot(hn, wgate_ref[...]) + bgate_ref[...])
        merged = (gates[:, :d] * _dot(attn_ref[rows, :], wba_ref[...])
                  + gates[:, d:] * _dot(ssm_ref[rows, :], wbs_ref[...]))
        return merged.astype(BF16)

    def finish(st, m2):
        rows = slice(st * sub, (st + 1) * sub)
        out = x_ref[rows, :] + _rms(m2, gpost_ref[...])
        for c in range(d // LANES):
            perm_ref[st, c] = out[:, c * LANES:(c + 1) * LANES]
        for c in range(d // LANES):
            for b in range(FFN_PHASES):
                r0 = b * n + st * ns
                o_ref[r0:r0 + ns, c * LANES:(c + 1) * LANES] = perm_ref[st, c, pl.ds(b, ns, stride=FFN_PHASES), :]

    pending = None
    for st in range(MERGE_SUBTILES):
        merged = mixed(st)
        if pending is not None:
            finish(*pending)
        pending = (st, _dot(merged, wout_ref[...]))
    finish(*pending)


def _merge(x, attn, ssm_tm, g_pre, w_gate, b_gate, w_ba, w_bs, w_out, g_post):
    b, l, d = x.shape
    tl = ROW_TILE
    const = lambda *_: (0, 0)
    return pl.pallas_call(
        _merge_kernel,
        grid=(b, l // tl),
        in_specs=[
            pl.BlockSpec((None, tl, d), lambda bi, i: (bi, i, 0)),
            pl.BlockSpec((None, tl, N_HEADS * V_HEAD_DIM), lambda bi, i: (bi, i, 0)),
            pl.BlockSpec((tl, SSM_WIDTH), lambda bi, i: (i, bi)),
            pl.BlockSpec((1, d), const),
            pl.BlockSpec(w_gate.shape, const),
            pl.BlockSpec((1, 2 * d), const),
            pl.BlockSpec(w_ba.shape, const),
            pl.BlockSpec(w_bs.shape, const),
            pl.BlockSpec(w_out.shape, const),
            pl.BlockSpec((1, d), const),
        ],
        out_specs=pl.BlockSpec((None, tl, d), lambda bi, i: (bi, i, 0)),
        out_shape=jax.ShapeDtypeStruct((b, l, d), F32),
        scratch_shapes=[pltpu.VMEM((MERGE_SUBTILES, d // LANES, tl // MERGE_SUBTILES, LANES), F32)],
        compiler_params=pltpu.CompilerParams(
            dimension_semantics=("parallel", "parallel"), vmem_limit_bytes=VMEM_LIMIT),
        name="merge",
    )(x, attn, ssm_tm, g_pre, w_gate, b_gate, w_ba, w_bs, w_out, g_post)


def _ffn_kernel(x_ref, gpre_ref, wup_ref, cw_ref, cb_ref, wdown_ref, gpost_ref, o_ref,
                act_ref, perm_ref, tail_ref):
    i = pl.program_id(1)
    tl, d = x_ref.shape
    n = tl // FFN_PHASES
    lane_groups = d // LANES

    @pl.when(i == 0)
    def _():
        tail_ref[...] = jnp.zeros(tail_ref.shape, F32)

    x1 = x_ref[...]
    hn = _rms(x1, gpre_ref[...]).astype(BF16)

    first_row = lax.broadcasted_iota(jnp.int32, (n, 1), 0) == 0

    def conv(h, c0):
        cols = slice(c0, c0 + FFN_COL_CHUNK)
        blocks = [h[b * n:(b + 1) * n] for b in range(FFN_PHASES)]

        def moved_down(block, prev_token):
            return jnp.where(first_row, prev_token, pltpu.roll(block, 1, 0))

        s1 = moved_down(blocks[FFN_PHASES - 1], tail_ref[1:2, cols])
        s2 = moved_down(blocks[FFN_PHASES - 2], tail_ref[0:1, cols])
        tail_ref[0:1, cols] = blocks[FFN_PHASES - 2][n - 1:n]
        tail_ref[1:2, cols] = blocks[FFN_PHASES - 1][n - 1:n]
        tap1 = [s1] + blocks[:-1]
        tap2 = [s2, s1] + blocks[:-2]
        w0, w1, w2, bias = cw_ref[0:1, cols], cw_ref[1:2, cols], cw_ref[2:3, cols], cb_ref[:, cols]
        return jnp.concatenate(
            [bias + w2 * blocks[b] + w1 * tap1[b] + w0 * tap2[b] for b in range(FFN_PHASES)], axis=0)

    for j in range(D_FF // FFN_COL_CHUNK):
        c0 = j * FFN_COL_CHUNK
        hg = _dot(hn, wup_ref[:, c0:c0 + FFN_COL_CHUNK])
        hv = _dot(hn, wup_ref[:, D_FF + c0:D_FF + c0 + FFN_COL_CHUNK])
        act = _gelu_tanh(conv(hg, c0)) * conv(hv, D_FF + c0)
        act_ref[:, c0:c0 + FFN_COL_CHUNK] = act.astype(BF16)

    half = tl // 2
    blocks_per_half = FFN_PHASES // 2

    def finish(h, ff):
        rows = slice(h * half, (h + 1) * half)
        out = x1[rows] + _rms(ff, gpost_ref[...])
        for c in range(lane_groups):
            for bb in range(blocks_per_half):
                b = h * blocks_per_half + bb
                perm_ref[c, pl.ds(b, n, stride=FFN_PHASES), :] = out[bb * n:(bb + 1) * n, c * LANES:(c + 1) * LANES]

    ff0 = _dot(act_ref[:half, :], wdown_ref[...])
    ff1 = _dot(act_ref[half:, :], wdown_ref[...])
    finish(0, ff0)
    finish(1, ff1)
    for c in range(lane_groups):
        o_ref[:, c * LANES:(c + 1) * LANES] = perm_ref[c]


def _conv_ffn(x1, g_pre, w_up, conv_w, conv_b, w_down, g_post):
    b, l, d = x1.shape
    tl = ROW_TILE
    const = lambda *_: (0, 0)
    return pl.pallas_call(
        _ffn_kernel,
        grid=(b, l // tl),
        in_specs=[
            pl.BlockSpec((None, tl, d), lambda bi, i: (bi, i, 0)),
            pl.BlockSpec((1, d), const),
            pl.BlockSpec(w_up.shape, const, pipeline_mode=pl.Buffered(1)),
            pl.BlockSpec(conv_w.shape, const),
            pl.BlockSpec(conv_b.shape, const),
            pl.BlockSpec(w_down.shape, const, pipeline_mode=pl.Buffered(1)),
            pl.BlockSpec((1, d), const),
        ],
        out_specs=pl.BlockSpec((None, tl, d), lambda bi, i: (bi, i, 0)),
        out_shape=jax.ShapeDtypeStruct((b, l, d), F32),
        scratch_shapes=[pltpu.VMEM((tl, D_FF), BF16),
                        pltpu.VMEM((d // LANES, tl, LANES), F32),
                        pltpu.VMEM((CONV_WIDTH - 1, 2 * D_FF), F32)],
        compiler_params=pltpu.CompilerParams(
            dimension_semantics=("parallel", "arbitrary"), vmem_limit_bytes=VMEM_LIMIT),
        name="conv_ffn",
    )(x1, g_pre, w_up, conv_w, conv_b, w_down, g_post)


def _head_lane_groups(nope, rope):
    k = (nope if nope is not None else rope).shape[0]
    nope = jnp.zeros((k, N_HEADS, QK_NOPE_DIM), BF16) if nope is None else nope
    rope = jnp.zeros((k, N_HEADS, QK_ROPE_DIM), BF16) if rope is None else rope
    return jnp.concatenate([nope, rope, rope], axis=2).astype(BF16).reshape(k, -1)


def _layer(x, pos_f, p):
    b, l, d = x.shape
    row = lambda v: v.reshape(1, -1)

    w_in = p["w_in"]
    off_u = P_KR + QK_ROPE_DIM
    off_gate = off_u + SSM_WIDTH
    w_kr = w_in[:, P_KR:off_u]
    w_low = jnp.concatenate([w_in[:, :P_KR], jnp.zeros((d, QK_NOPE_DIM), w_in.dtype), w_kr, w_kr],
                            axis=1).astype(BF16)
    w_u = w_in[:, off_u:off_gate].astype(BF16)
    w_gate = w_in[:, off_gate:].astype(BF16)
    w_uq = p["w_uq"].reshape(Q_LORA_RANK, N_HEADS, QK_HEAD_DIM)
    w_uq_p = _head_lane_groups(w_uq[:, :, :QK_NOPE_DIM], w_uq[:, :, QK_NOPE_DIM:])
    w_uk_p = _head_lane_groups(p["w_uk"].reshape(KV_LORA_RANK, N_HEADS, QK_NOPE_DIM), None)

    q_all, k_all, v_t, u_tm = _in_proj(
        x, pos_f, row(p["mix_norm_pre"]), w_low, w_u, row(p["q_norm"]), w_uq_p, row(p["kv_norm"]),
        w_uk_p, p["w_uv"].T.astype(BF16))

    attn = _attention(q_all, k_all, v_t)

    a_re_t, a_im_t, wb, wcr, wci = _s5_prep(
        p["ssm_lambda_re"], p["ssm_lambda_im"], p["ssm_log_dt"], p["ssm_b_re"], p["ssm_b_im"],
        p["ssm_c_re"], p["ssm_c_im"], b)
    ssm_tm = _s5(u_tm, a_re_t, a_im_t, wb, wcr, wci,
                 row(p["ssm_d"]), p["w_glu"].astype(BF16), row(p["b_glu"]), b)

    x1 = _merge(x, attn, ssm_tm, row(p["mix_norm_pre"]), w_gate, row(p["b_gate"]),
                p["w_branch_attn"].astype(BF16), p["w_branch_ssm"].astype(BF16),
                p["w_out"].astype(BF16), row(p["mix_norm_post"]))
    return _conv_ffn(x1, row(p["ffn_norm_pre"]), p["w_up"].astype(BF16), p["conv_w"], row(p["conv_b"]),
                     p["w_down"].astype(BF16), row(p["ffn_norm_post"]))


def kernel(x, positions, mix_norm_pre, w_in, q_norm, w_uq, kv_norm, w_uk, w_uv, ssm_lambda_re, ssm_lambda_im, ssm_log_dt, ssm_b_re, ssm_b_im, ssm_c_re, ssm_c_im, ssm_d, w_glu, b_glu, w_branch_attn, w_branch_ssm, b_gate, w_out, mix_norm_post, ffn_norm_pre, w_up, conv_w, conv_b, w_down, ffn_norm_post):
    b, l, _ = x.shape
    params = dict(mix_norm_pre=mix_norm_pre, w_in=w_in, q_norm=q_norm, w_uq=w_uq, kv_norm=kv_norm,
                  w_uk=w_uk, w_uv=w_uv, ssm_lambda_re=ssm_lambda_re, ssm_lambda_im=ssm_lambda_im,
                  ssm_log_dt=ssm_log_dt, ssm_b_re=ssm_b_re, ssm_b_im=ssm_b_im, ssm_c_re=ssm_c_re,
                  ssm_c_im=ssm_c_im, ssm_d=ssm_d, w_glu=w_glu, b_glu=b_glu, w_branch_attn=w_branch_attn,
                  w_branch_ssm=w_branch_ssm, b_gate=b_gate, w_out=w_out, mix_norm_post=mix_norm_post,
                  ffn_norm_pre=ffn_norm_pre, w_up=w_up, conv_w=conv_w, conv_b=conv_b, w_down=w_down,
                  ffn_norm_post=ffn_norm_post)
    pos_f = positions.astype(F32).reshape(b, 1, l)
    for layer in range(mix_norm_pre.shape[0]):
        x = _layer(x, pos_f, {k: v[layer] for k, v in params.items()})
    return x
```
